```python
import math
import jax
import jax.numpy as jnp
from jax import lax
import numpy as np


D_MODEL = 2048
BATCH = 2
SEQ = 8192
DEPTH = 2

GRID_W = 64
CTX_LEN = 256
HEAD_DIM = 128
MIX_HEADS = D_MODEL // HEAD_DIM
NORM_EPS = 1e-6
ROPE_THETA = 10000.0
DN_HEADS = MIX_HEADS // 2
DN_DK = 128
DN_DV = 128
DN_CONV = 5
DN_CHUNK = 64
DN_QK_W = DN_HEADS * DN_DK
DN_V_W = DN_HEADS * DN_DV
DN_CONV_CH = 2 * DN_QK_W + DN_V_W
DIFF_HEADS = MIX_HEADS // 2
DIFF_DQK = 64
DIFF_DV = 2 * DIFF_DQK
DIFF_Q_W = DIFF_HEADS * 2 * DIFF_DQK
DIFF_V_W = DIFF_HEADS * DIFF_DV
Q_BLOCK = 128
AB_SPLITS = (DN_CONV_CH, DN_V_W, 2 * DN_HEADS, 2 * DN_HEADS, DIFF_Q_W, DIFF_Q_W, DIFF_V_W)
AB_IN = sum(AB_SPLITS)
AB_OUT = DN_V_W + DIFF_V_W
GQA_Q_HEADS = MIX_HEADS
GQA_KV_HEADS = 4
GQA_GROUP = GQA_Q_HEADS // GQA_KV_HEADS
GQA_Q_W = GQA_Q_HEADS * HEAD_DIM
GQA_KV_W = GQA_KV_HEADS * HEAD_DIM
GQA_IN = GQA_Q_W + 2 * GQA_KV_W
GQA_OUT = GQA_Q_W
WINDOW = 128
WIN_BLOCK = 128
PEER_HEADS = 8
PEER_NKEYS = 128
PEER_N_EXPERTS = PEER_NKEYS * PEER_NKEYS
PEER_DKEY = 256
PEER_TOPK = 16
PEER_TOKEN_BLOCK = 128
N_EVEN = (DEPTH + 1) // 2
N_ODD = DEPTH // 2

kernel_name = 'hybrid_deltanet_diffattn_swa_peer_dit'

F32 = jnp.float32


def rmsnorm(x, w):
    xf = x.astype(F32)
    y = xf * lax.rsqrt(jnp.mean(xf * xf, axis=-1, keepdims=True) + NORM_EPS)
    return (y * w.astype(F32)).astype(x.dtype)


def l2norm(x):
    xf = x.astype(F32)
    return (xf * lax.rsqrt(jnp.sum(xf * xf, axis=-1, keepdims=True) + NORM_EPS)).astype(x.dtype)


def modulate(h, shift, scale):
    return h * (1 + scale) + shift


def split_cols(p, sizes):
    return jnp.split(p, np.cumsum(sizes)[:-1].tolist(), axis=-1)


def heads(x, n):
    b, t, _ = x.shape
    return x.reshape(b, t, n, -1).transpose(0, 2, 1, 3)


def merge(x):
    b, h, t, d = x.shape
    return x.transpose(0, 2, 1, 3).reshape(b, t, h * d)


def rope_1d(x, pos):
    f = x.shape[-1] // 2
    inv = ROPE_THETA ** (-jnp.arange(f, dtype=F32) / f)
    ang = pos.astype(F32)[:, None] * inv[None, :]
    cos = jnp.cos(ang).astype(x.dtype)
    sin = jnp.sin(ang).astype(x.dtype)
    x1, x2 = x[..., :f], x[..., f:]
    return jnp.concatenate([x1 * cos - x2 * sin, x2 * cos + x1 * sin], axis=-1)


def axial_rope(x, rows, cols):
    h = x.shape[-1] // 2
    return jnp.concatenate([rope_1d(x[..., :h], rows), rope_1d(x[..., h:], cols)], axis=-1)


def short_conv(x, w):
    k, ch = w.shape
    pad = (k - 1) // 2
    y = lax.conv_general_dilated(x, w[:, None, :].astype(x.dtype), window_strides=(1,),
                                 padding=[(pad, pad)], dimension_numbers=('NWC', 'WIO', 'NWC'),
                                 feature_group_count=ch)
    return jax.nn.silu(y)


def gated_delta_chunked(q, k, v, beta, g, s0):
    b, h, t, dk = q.shape
    dv = v.shape[-1]
    c = DN_CHUNK
    n = t // c
    q = q.astype(F32).reshape(b, h, n, c, dk)
    k = k.astype(F32).reshape(b, h, n, c, dk)
    v = v.astype(F32).reshape(b, h, n, c, dv)
    beta = beta.astype(F32).reshape(b, h, n, c)
    gc = jnp.cumsum(g.astype(F32).reshape(b, h, n, c), axis=-1)
    tril = jnp.tril(jnp.ones((c, c), dtype=bool))
    strict = jnp.tril(jnp.ones((c, c), dtype=bool), -1)
    decay = jnp.exp(jnp.where(tril, gc[..., :, None] - gc[..., None, :], -jnp.inf))
    kb = k * beta[..., None]
    a = jnp.where(strict, jnp.einsum('bhnid,bhnjd->bhnij', kb, k) * decay, 0.0)
    eye = jnp.eye(c, dtype=F32)
    inv = eye - a
    a_pow = a
    for _ in range(int(math.log2(c)) - 1):
        a_pow = a_pow @ a_pow
        inv = inv @ (eye + a_pow)
    u = inv @ (v * beta[..., None])
    w = inv @ (kb * jnp.exp(gc)[..., None])
    qk = jnp.where(tril, jnp.einsum('bhnid,bhnjd->bhnij', q, k) * decay, 0.0)
    q_dec = q * jnp.exp(gc)[..., None]
    k_dec = k * jnp.exp(gc[..., -1:] - gc)[..., None]
    g_last = jnp.exp(gc[..., -1])

    def step(state, inp):
        qk_i, qd_i, kd_i, u_i, w_i, gl_i = inp
        v_new = u_i - w_i @ state
        o_i = qd_i @ state + qk_i @ v_new
        state = state * gl_i[..., None, None] + jnp.swapaxes(kd_i, -1, -2) @ v_new
        return state, o_i

    xs = tuple(jnp.moveaxis(arr, 2, 0) for arr in (qk, q_dec, k_dec, u, w, g_last))
    s_final, o = lax.scan(step, s0.astype(F32), xs)
    return jnp.moveaxis(o, 0, 2).reshape(b, h, t, dv), s_final


def bidir_delta(q, k, v, beta, g, s0_fwd, s0_bwd):
    o_f, s_f = gated_delta_chunked(q, k, v, beta[0], g[0], s0_fwd)
    fl = lambda arr: jnp.flip(arr, axis=2)
    o_b, s_b = gated_delta_chunked(fl(q), fl(k), fl(v), fl(beta[1]), fl(g[1]), s0_bwd)
    return o_f + fl(o_b), s_f, s_b


def dn_prepare(p_qkv, p_z, p_beta, p_a, conv_w, a_log, dt_bias):
    qkv = short_conv(p_qkv, conv_w)
    q, k, v = split_cols(qkv, (DN_QK_W, DN_QK_W, DN_V_W))
    q = l2norm(heads(q, DN_HEADS)) * (DN_DK ** -0.5)
    k = l2norm(heads(k, DN_HEADS))
    v = heads(v, DN_HEADS)
    b, t, _ = p_beta.shape
    beta = jax.nn.sigmoid(p_beta.astype(F32)).reshape(b, t, 2, DN_HEADS)
    dt = jax.nn.softplus(p_a.astype(F32).reshape(b, t, 2, DN_HEADS) + dt_bias.astype(F32))
    g = -jnp.exp(a_log.astype(F32)) * dt
    return q, k, v, heads(p_z, DN_HEADS), beta.transpose(2, 0, 3, 1), g.transpose(2, 0, 3, 1)


def dn_output(o, z, norm_w):
    y = rmsnorm(o, norm_w) * jax.nn.silu(z.astype(F32))
    return merge(y.astype(z.dtype))


def diff_prepare(p_q, p_k, p_v, rows, cols):
    q = heads(p_q, DIFF_HEADS)
    k = heads(p_k, DIFF_HEADS)
    q1, q2 = q[..., :DIFF_DQK], q[..., DIFF_DQK:]
    k1, k2 = k[..., :DIFF_DQK], k[..., DIFF_DQK:]
    if rows is not None:
        q1, q2, k1, k2 = (axial_rope(arr, rows, cols) for arr in (q1, q2, k1, k2))
    return q1, q2, k1, k2, heads(p_v, DIFF_HEADS)


def diff_core(q1, q2, k1, k2, v, lam):
    scale = DIFF_DQK ** -0.5
    p1 = jax.nn.softmax(jnp.einsum('bhqd,bhkd->bhqk', q1, k1).astype(F32) * scale, axis=-1)
    p2 = jax.nn.softmax(jnp.einsum('bhqd,bhkd->bhqk', q2, k2).astype(F32) * scale, axis=-1)
    return jnp.einsum('bhqk,bhkd->bhqd', (p1 - lam * p2).astype(v.dtype), v)


def diff_latent(q1, q2, k1, k2, v, k1c, k2c, vc, lam):
    k1a = jnp.concatenate([k1, k1c], axis=2)
    k2a = jnp.concatenate([k2, k2c], axis=2)
    va = jnp.concatenate([v, vc], axis=2)
    b, h, t, _ = q1.shape

    def block(n):
        s = n * Q_BLOCK
        return diff_core(lax.dynamic_slice_in_dim(q1, s, Q_BLOCK, axis=2),
                         lax.dynamic_slice_in_dim(q2, s, Q_BLOCK, axis=2), k1a, k2a, va, lam)

    o = lax.map(block, jnp.arange(t // Q_BLOCK))
    return jnp.moveaxis(o, 0, 2).reshape(b, h, t, -1)


def diff_output(o, norm_w, lam_init):
    return merge(rmsnorm(o, norm_w) * (1 - lam_init))


def even_mixer(hx, hc, w_in, w_out, conv_w, a_log, dt_bias, dn_norm_w, lam_vec, diff_norm_w,
               lam_init, rows, cols, need_ctx):
    px = split_cols(hx @ w_in, AB_SPLITS)
    pc = split_cols(hc @ w_in, AB_SPLITS)
    b = hx.shape[0]
    qx, kx, vx, zx, bx, gx = dn_prepare(px[0], px[1], px[2], px[3], conv_w, a_log, dt_bias)
    qc, kc, vc, zc, bc, gc = dn_prepare(pc[0], pc[1], pc[2], pc[3], conv_w, a_log, dt_bias)
    s0 = jnp.zeros((b, DN_HEADS, DN_DK, DN_DV), F32)
    oc, s_f, s_b = bidir_delta(qc, kc, vc, bc, gc, s0, s0)
    ox, _, _ = bidir_delta(qx, kx, vx, bx, gx, s_f, s_b)
    lv = lam_vec.astype(F32)
    lam = jnp.exp(jnp.sum(lv[0] * lv[1])) - jnp.exp(jnp.sum(lv[2] * lv[3])) + lam_init
    q1x, q2x, k1x, k2x, vdx = diff_prepare(px[4], px[5], px[6], rows, cols)
    q1c, q2c, k1c, k2c, vdc = diff_prepare(pc[4], pc[5], pc[6], None, None)
    dx = diff_latent(q1x, q2x, k1x, k2x, vdx, k1c, k2c, vdc, lam)
    out_x = jnp.concatenate([dn_output(ox, zx, dn_norm_w), diff_output(dx, diff_norm_w, lam_init)], axis=-1) @ w_out
    out_c = None
    if need_ctx:
        dc = diff_core(q1c, q2c, k1c, k2c, vdc, lam)
        out_c = jnp.concatenate([dn_output(oc, zc, dn_norm_w), diff_output(dc, diff_norm_w, lam_init)], axis=-1) @ w_out
    return out_x, out_c


def softmax_with_sink(s, sink_g):
    sk = jnp.broadcast_to(sink_g[None, :, :, None, None].astype(F32), s.shape[:-1] + (1,))
    p = jax.nn.softmax(jnp.concatenate([s, sk], axis=-1), axis=-1)
    return p[..., :-1]


def window_attention(q, k, v, kc, vc, sink_g):
    b, hq, t, d = q.shape
    scale = d ** -0.5
    pad = ((0, 0), (0, 0), (WIN_BLOCK, WIN_BLOCK), (0, 0))
    kp = jnp.pad(k, pad)
    vp = jnp.pad(v, pad)
    qg = q.reshape(b, GQA_KV_HEADS, GQA_GROUP, t, d)
    i = jnp.arange(WIN_BLOCK)[:, None]
    j = jnp.arange(3 * WIN_BLOCK)[None, :]
    band = jnp.abs(j - WIN_BLOCK - i) <= WINDOW

    def block(n):
        qn = lax.dynamic_slice_in_dim(qg, n * WIN_BLOCK, WIN_BLOCK, axis=3)
        kn = lax.dynamic_slice_in_dim(kp, n * WIN_BLOCK, 3 * WIN_BLOCK, axis=2)
        vn = lax.dynamic_slice_in_dim(vp, n * WIN_BLOCK, 3 * WIN_BLOCK, axis=2)
        kpos = n * WIN_BLOCK - WIN_BLOCK + j
        valid = band & (kpos >= 0) & (kpos < t)
        s_loc = jnp.where(valid, jnp.einsum('bkgqd,bkjd->bkgqj', qn, kn).astype(F32) * scale, -jnp.inf)
        s_ctx = jnp.einsum('bkgqd,bkld->bkgql', qn, kc).astype(F32) * scale
        p = softmax_with_sink(jnp.concatenate([s_loc, s_ctx], axis=-1), sink_g).astype(v.dtype)
        return (jnp.einsum('bkgqj,bkjd->bkgqd', p[..., :3 * WIN_BLOCK], vn)
                + jnp.einsum('bkgql,bkld->bkgqd', p[..., 3 * WIN_BLOCK:], vc))

    o = lax.map(block, jnp.arange(t // WIN_BLOCK))
    return jnp.moveaxis(o, 0, 3).reshape(b, hq, t, d)


def context_attention(q, kc, vc, sink_g):
    b, hq, t, d = q.shape
    qg = q.reshape(b, GQA_KV_HEADS, GQA_GROUP, t, d)
    s = jnp.einsum('bkgqd,bkld->bkgql', qg, kc).astype(F32) * d ** -0.5
    p = softmax_with_sink(s, sink_g).astype(vc.dtype)
    return jnp.einsum('bkgql,bkld->bkgqd', p, vc).reshape(b, hq, t, d)


def odd_mixer(hx, hc, w_in, w_out, sink, rows, cols, need_ctx):
    sizes = (GQA_Q_W, GQA_KV_W, GQA_KV_W)
    qx, kx, vx = split_cols(hx @ w_in, sizes)
    qc, kc, vc = split_cols(hc @ w_in, sizes)
    qx = axial_rope(heads(qx, GQA_Q_HEADS), rows, cols)
    kx = axial_rope(heads(kx, GQA_KV_HEADS), rows, cols)
    vx = heads(vx, GQA_KV_HEADS)
    kc = heads(kc, GQA_KV_HEADS)
    vc = heads(vc, GQA_KV_HEADS)
    sink_g = sink.reshape(GQA_KV_HEADS, GQA_GROUP)
    out_x = merge(window_attention(qx, kx, vx, kc, vc, sink_g)) @ w_out
    out_c = None
    if need_ctx:
        out_c = merge(context_attention(heads(qc, GQA_Q_HEADS), kc, vc, sink_g)) @ w_out
    return out_x, out_c


def peer_ffn(h, wq, bq, subkeys, u_tab, v_tab):
    b, t, d = h.shape
    n = b * t
    x = h.reshape(n, d)
    q = (x @ wq + bq).reshape(n, PEER_HEADS, 2, PEER_DKEY // 2)
    s = jnp.einsum('nhpd,hpkd->nhpk', q, subkeys).astype(F32)
    sv, si = lax.top_k(s, PEER_TOPK)
    cand_s = (sv[:, :, 0, :, None] + sv[:, :, 1, None, :]).reshape(n, PEER_HEADS, PEER_TOPK * PEER_TOPK)
    cand_i = (si[:, :, 0, :, None] * PEER_NKEYS + si[:, :, 1, None, :]).reshape(n, PEER_HEADS, PEER_TOPK * PEER_TOPK)
    top_s, top_pos = lax.top_k(cand_s, PEER_TOPK)
    experts = jnp.take_along_axis(cand_i, top_pos, axis=-1).reshape(n, PEER_HEADS * PEER_TOPK)
    gates = jax.nn.softmax(top_s, axis=-1).reshape(n, PEER_HEADS * PEER_TOPK)
    nb = n // PEER_TOKEN_BLOCK

    def block(inp):
        xb, eb, gb = inp
        act = jax.nn.gelu(jnp.einsum('td,tkd->tk', xb, u_tab[eb]).astype(F32), approximate=False) * gb
        return jnp.einsum('tk,tkd->td', act.astype(xb.dtype), v_tab[eb])

    out = lax.map(block, (x.reshape(nb, PEER_TOKEN_BLOCK, d),
                          experts.reshape(nb, PEER_TOKEN_BLOCK, -1),
                          gates.reshape(nb, PEER_TOKEN_BLOCK, -1)))
    return out.reshape(b, t, d)


def setup_inputs(seed: int = 0) -> dict:
    key = jax.random.key(seed)
    ks = jax.random.split(key, 26)
    D = D_MODEL
    nrm = lambda k, shape, scale: jax.random.normal(k, shape, F32) * scale
    dt = jnp.exp(jax.random.uniform(ks[12], (N_EVEN, 2, DN_HEADS), F32, math.log(1e-3), math.log(1e-1)))
    return {
        'x': nrm(ks[0], (BATCH, SEQ, D), 1.0),
        'c': nrm(ks[1], (BATCH, D), 1.0),
        'ctx': nrm(ks[2], (BATCH, CTX_LEN, D), 1.0),
        'c_ctx': nrm(ks[3], (D,), 1.0),
        'ada_w': nrm(ks[4], (DEPTH, D, 6 * D), 0.5 * D ** -0.5),
        'ada_b': nrm(ks[5], (DEPTH, 6 * D), 0.02),
        'norm_mix_w': 1.0 + nrm(ks[6], (DEPTH, D), 0.02),
        'norm_ffn_w': 1.0 + nrm(ks[7], (DEPTH, D), 0.02),
        'ab_w_in': nrm(ks[8], (N_EVEN, D, AB_IN), D ** -0.5),
        'ab_w_out': nrm(ks[9], (N_EVEN, AB_OUT, D), AB_OUT ** -0.5),
        'dn_conv_w': nrm(ks[10], (N_EVEN, DN_CONV, DN_CONV_CH), DN_CONV ** -0.5),
        'dn_a_log': jnp.log(jax.random.uniform(ks[11], (N_EVEN, 2, DN_HEADS), F32, 1.0, 16.0)),
        'dn_dt_bias': dt + jnp.log(-jnp.expm1(-dt)),
        'dn_norm_w': 1.0 + nrm(ks[13], (N_EVEN, DN_DV), 0.02),
        'diff_lambda': nrm(ks[14], (N_EVEN, 4, DIFF_DQK), 0.1),
        'diff_norm_w': 1.0 + nrm(ks[15], (N_EVEN, DIFF_DV), 0.02),
        'gqa_w_in': nrm(ks[16], (N_ODD, D, GQA_IN), D ** -0.5),
        'gqa_w_out': nrm(ks[17], (N_ODD, GQA_OUT, D), GQA_OUT ** -0.5),
        'gqa_sink': nrm(ks[18], (N_ODD, GQA_Q_HEADS), 0.5),
        'peer_wq': nrm(ks[19], (DEPTH, D, PEER_HEADS * PEER_DKEY), D ** -0.5),
        'peer_bq': nrm(ks[20], (DEPTH, PEER_HEADS * PEER_DKEY), 0.01),
        'peer_subkeys': nrm(ks[21], (DEPTH, PEER_HEADS, 2, PEER_NKEYS, PEER_DKEY // 2), (PEER_DKEY // 2) ** -0.5),
        'peer_u': nrm(ks[22], (DEPTH, PEER_N_EXPERTS, D), D ** -0.5),
        'peer_v': nrm(ks[23], (DEPTH, PEER_N_EXPERTS, D), 1.0),
        'final_norm_w': 1.0 + nrm(ks[24], (D,), 0.02),
    }


def reference(x, c, ctx, c_ctx, ada_w, ada_b, norm_mix_w, norm_ffn_w, ab_w_in, ab_w_out, dn_conv_w,
              dn_a_log, dn_dt_bias, dn_norm_w, diff_lambda, diff_norm_w, gqa_w_in, gqa_w_out, gqa_sink,
              peer_wq, peer_bq, peer_subkeys, peer_u, peer_v, final_norm_w):
    t = x.shape[1]
    n_rows = t // GRID_W
    rows = jnp.broadcast_to(jnp.arange(n_rows, dtype=jnp.int32)[:, None], (n_rows, GRID_W)).reshape(-1)
    cols = jnp.broadcast_to(jnp.arange(GRID_W, dtype=jnp.int32)[None, :], (n_rows, GRID_W)).reshape(-1)
    cond_x = jax.nn.silu(c)
    cond_c = jax.nn.silu(c_ctx)
    for l in range(DEPTH):
        need_ctx = l < DEPTH - 1
        mod_x = jnp.split((cond_x @ ada_w[l] + ada_b[l])[:, None, :], 6, axis=-1)
        mod_c = jnp.split((cond_c @ ada_w[l] + ada_b[l])[None, None, :], 6, axis=-1)
        hx = modulate(rmsnorm(x, norm_mix_w[l]), mod_x[0], mod_x[1])
        hc = modulate(rmsnorm(ctx, norm_mix_w[l]), mod_c[0], mod_c[1])
        if l % 2 == 0:
            e = l // 2
            mx, mc = even_mixer(hx, hc, ab_w_in[e], ab_w_out[e], dn_conv_w[e], dn_a_log[e], dn_dt_bias[e],
                                dn_norm_w[e], diff_lambda[e], diff_norm_w[e],
                                0.8 - 0.6 * math.exp(-0.3 * l), rows, cols, need_ctx)
        else:
            o = l // 2
            mx, mc = odd_mixer(hx, hc, gqa_w_in[o], gqa_w_out[o], gqa_sink[o], rows, cols, need_ctx)
        x = x + mod_x[2] * mx
        fx = modulate(rmsnorm(x, norm_ffn_w[l]), mod_x[3], mod_x[4])
        x = x + mod_x[5] * peer_ffn(fx, peer_wq[l], peer_bq[l], peer_subkeys[l], peer_u[l], peer_v[l])
        if need_ctx:
            ctx = ctx + mod_c[2] * mc
            fc = modulate(rmsnorm(ctx, norm_ffn_w[l]), mod_c[3], mod_c[4])
            ctx = ctx + mod_c[5] * peer_ffn(fc, peer_wq[l], peer_bq[l], peer_subkeys[l], peer_u[l], peer_v[l])
    return rmsnorm(x, final_norm_w)
```

```python
import functools
import math

import jax
import jax.numpy as jnp
import numpy as np
from jax import lax
from jax.experimental import pallas as pl
from jax.experimental.pallas import tpu as pltpu

F32 = jnp.float32
BF16 = jnp.bfloat16

LANES = 128
HEAD_DIM = 128
NORM_EPS = 1e-6
ROPE_THETA = 10000.0
GRID_W = 64
DN_HEADS = 8
DN_CHUNK = 64
DN_CONV = 5
DIFF_HEADS = 8
DIFF_DQK = 64
GQA_Q_HEADS = 16
GQA_KV_HEADS = 4
GQA_GROUP = GQA_Q_HEADS // GQA_KV_HEADS
WINDOW = 128
PEER_HEADS = 8
PEER_NKEYS = 128
PEER_TOPK = 16
NEG_BIG = -1e30
VMEM_LIMIT = 56 * 1024 * 1024


def _params(sem):
    return pltpu.CompilerParams(dimension_semantics=sem, vmem_limit_bytes=VMEM_LIMIT)


def _dot(a, b):
    return jnp.dot(a.astype(BF16), b.astype(BF16), preferred_element_type=F32)


def _dot_nt(a, b):
    return lax.dot_general(a.astype(BF16), b.astype(BF16), (((1,), (1,)), ((), ())),
                           preferred_element_type=F32)


def _sigmoid(x):
    return 1.0 / (1.0 + jnp.exp(-x))


def _normmod_kernel(x_ref, w_ref, shift_ref, scale_ref, o_ref):
    x = x_ref[...]
    y = x * lax.rsqrt(jnp.mean(x * x, axis=-1, keepdims=True) + NORM_EPS)
    y = y * w_ref[...]
    o_ref[...] = (y * (1.0 + scale_ref[...]) + shift_ref[...]).astype(o_ref.dtype)


def normmod(x, w, shift, scale, rows=512):
    b, t, d = x.shape
    tr = min(rows, t)
    bm = shift.shape[0]
    mod_map = (lambda i, j: (i, 0, 0)) if bm == b else (lambda i, j: (0, 0, 0))
    return pl.pallas_call(
        _normmod_kernel,
        grid=(b, t // tr),
        in_specs=[pl.BlockSpec((None, tr, d), lambda i, j: (i, j, 0)),
                  pl.BlockSpec((1, d), lambda i, j: (0, 0)),
                  pl.BlockSpec((None, 1, d), mod_map),
                  pl.BlockSpec((None, 1, d), mod_map)],
        out_specs=pl.BlockSpec((None, tr, d), lambda i, j: (i, j, 0)),
        out_shape=jax.ShapeDtypeStruct((b, t, d), BF16),
        compiler_params=_params(("parallel", "parallel")),
        name="normmod",
    )(x, w.reshape(1, d), shift.reshape(bm, 1, d), scale.reshape(bm, 1, d))


def _mm_kernel(*refs, silu_in, has_bias, has_res):
    a_ref, w_ref = refs[0], refs[1]
    pos = 2
    a = a_ref[...]
    if silu_in:
        a = a * _sigmoid(a)
    acc = _dot(a, w_ref[...])
    if has_bias:
        acc = acc + refs[pos][...]
        pos += 1
    if has_res:
        acc = refs[pos][...] + refs[pos + 1][...] * acc
        pos += 2
    o_ref = refs[pos]
    o_ref[...] = acc.astype(o_ref.dtype)


def _pick_tn(n, cap):
    best = LANES
    for cand in range(LANES, min(n, cap) + 1, LANES):
        if n % cand == 0:
            best = cand
    return best


def matmul(a, w, *, out_dtype, bias=None, res=None, gate=None, rows_per_gate=None,
           silu_in=False, tm=512, tn_cap=2048, layer=0):
    m, k = a.shape
    n = w.shape[-1]
    tm = min(tm, m)
    tn = _pick_tn(n, tn_cap)
    if w.ndim == 3:
        w_spec = pl.BlockSpec((None, k, tn), lambda j, i: (layer, 0, j))
    else:
        w_spec = pl.BlockSpec((k, tn), lambda j, i: (0, j))
    in_specs = [pl.BlockSpec((tm, k), lambda j, i: (i, 0)), w_spec]
    args = [a, w]
    if bias is not None:
        in_specs.append(pl.BlockSpec((1, tn), lambda j, i: (0, j)))
        args.append(bias.reshape(1, n).astype(F32))
    if res is not None:
        blocks_per_gate = rows_per_gate // tm
        in_specs.append(pl.BlockSpec((tm, tn), lambda j, i: (i, j)))
        in_specs.append(pl.BlockSpec((None, 1, tn), lambda j, i: (i // blocks_per_gate, 0, j)))
        args += [res, gate.reshape(gate.shape[0], 1, n)]
    kern = functools.partial(_mm_kernel, silu_in=silu_in, has_bias=bias is not None,
                             has_res=res is not None)
    return pl.pallas_call(
        kern,
        grid=(n // tn, m // tm),
        in_specs=in_specs,
        out_specs=pl.BlockSpec((tm, tn), lambda j, i: (i, j)),
        out_shape=jax.ShapeDtypeStruct((m, n), out_dtype),
        compiler_params=_params(("parallel", "parallel")),
        name="matmul",
    )(*args)


def _conv_kernel(prev_ref, cur_ref, next_ref, w_ref, o_ref, *, tt, nheads_blk):
    i = pl.program_id(1)
    grp = pl.program_id(2)
    last = pl.num_programs(1) - 1
    prev = prev_ref[...] * (i > 0).astype(F32)
    nxt = next_ref[...] * (i < last).astype(F32)
    xx = jnp.concatenate([prev, cur_ref[...], nxt], axis=0)
    rows = tt + 16
    pad = (DN_CONV - 1) // 2
    y = None
    for k in range(DN_CONV):
        shift = (pad - k) % rows
        z = xx if shift == 0 else pltpu.roll(xx, shift, 0)
        term = z[8:8 + tt] * w_ref[k:k + 1, :]
        y = term if y is None else y + term
    y = y * _sigmoid(y)
    qscale = jnp.where(grp == 0, HEAD_DIM ** -0.5, 1.0).astype(F32)
    for h in range(nheads_blk):
        yh = y[:, h * HEAD_DIM:(h + 1) * HEAD_DIM]
        ss = jnp.sum(yh * yh, axis=-1, keepdims=True)
        fac = jnp.where(grp == 2, 1.0, lax.rsqrt(ss + NORM_EPS) * qscale)
        o_ref[:, h * HEAD_DIM:(h + 1) * HEAD_DIM] = yh * fac


def dn_conv(p, conv_w, tt=256):
    b, t, _ = p.shape
    tt = min(tt, t)
    cb = DN_HEADS * HEAD_DIM
    n8 = t // 8
    kern = functools.partial(_conv_kernel, tt=tt, nheads_blk=DN_HEADS)
    return pl.pallas_call(
        kern,
        grid=(b, t // tt, 3),
        in_specs=[pl.BlockSpec((None, 8, cb), lambda bi, i, g: (bi, jnp.maximum(i * (tt // 8) - 1, 0), g)),
                  pl.BlockSpec((None, tt, cb), lambda bi, i, g: (bi, i, g)),
                  pl.BlockSpec((None, 8, cb), lambda bi, i, g: (bi, jnp.minimum((i + 1) * (tt // 8), n8 - 1), g)),
                  pl.BlockSpec((DN_CONV, cb), lambda bi, i, g: (0, g))],
        out_specs=pl.BlockSpec((None, tt, cb), lambda bi, i, g: (bi, i, g)),
        out_shape=jax.ShapeDtypeStruct((b, t, 3 * cb), F32),
        compiler_params=_params(("parallel", "parallel", "parallel")),
        name="dn_conv",
    )(p, p, p, conv_w)


def _split3(x):
    hi = x.astype(BF16)
    r1 = x - hi.astype(F32)
    mid = r1.astype(BF16)
    lo = (r1 - mid.astype(F32)).astype(BF16)
    return hi, mid, lo


def _dn_scan_kernel(q_ref, k_ref, v_ref, p_ref, par_ref, s0_ref, o_ref, sfin_ref, s_scr, *, hb):
    d = pl.program_id(1)
    blk = pl.program_id(2)
    n = pl.program_id(3)
    c = DN_CHUNK

    @pl.when(n == 0)
    def _():
        s_scr[...] = s0_ref[...]

    sgn = 1 - 2 * d
    ii = lax.broadcasted_iota(jnp.int32, (c, c), 0)
    jj = lax.broadcasted_iota(jnp.int32, (c, c), 1)
    dij = (ii - jj) * sgn
    incl = dij >= 0
    strict = dij > 0
    eye = (ii == jj).astype(F32)

    raw = p_ref[...]
    beta_all = _sigmoid(raw)
    xa = raw + par_ref[0:1, :]
    softplus = jnp.maximum(xa, 0.0) + jnp.log1p(jnp.exp(-jnp.abs(xa)))
    g_all = -jnp.exp(par_ref[1:2, :]) * softplus
    inclb = incl.astype(BF16)
    hi, mid, lo = _split3(g_all)
    gc_all = (jnp.dot(inclb, hi, preferred_element_type=F32)
              + jnp.dot(inclb, mid, preferred_element_type=F32)
              + jnp.dot(inclb, lo, preferred_element_type=F32))
    gtot_all = jnp.sum(g_all, axis=0, keepdims=True)
    gc_t = gc_all.T
    lane = lax.broadcasted_iota(jnp.int32, (c, LANES), 1)
    lane1 = lax.broadcasted_iota(jnp.int32, (1, LANES), 1)
    sub = lax.broadcasted_iota(jnp.int32, (LANES, c), 0)

    for h in range(hb):
        idx_b = d * DN_HEADS + blk * hb + h
        idx_g = 2 * DN_HEADS + idx_b
        beta = jnp.sum(jnp.where(lane == idx_b, beta_all, 0.0), axis=1, keepdims=True)
        gc_c = jnp.sum(jnp.where(lane == idx_g, gc_all, 0.0), axis=1, keepdims=True)
        gc_r = jnp.sum(jnp.where(sub == idx_g, gc_t, 0.0), axis=0, keepdims=True)
        gtot = jnp.sum(jnp.where(lane1 == idx_g, gtot_all, 0.0), axis=1, keepdims=True)
        sl = slice(h * HEAD_DIM, (h + 1) * HEAD_DIM)
        q = q_ref[:, sl]
        k = k_ref[:, sl]
        v = v_ref[:, sl]
        decay = jnp.exp(jnp.where(incl, gc_c - gc_r, -jnp.inf))
        kb = k * beta
        a = jnp.where(strict, _dot_nt(kb, k) * decay, 0.0)
        inv = eye - a
        a_pow = a
        for _ in range(int(math.log2(c)) - 1):
            a_pow = _dot(a_pow, a_pow)
            inv = _dot(inv, eye + a_pow)
        egc = jnp.exp(gc_c)
        u = _dot(inv, v * beta)
        w = _dot(inv, kb * egc)
        qk = jnp.where(incl, _dot_nt(q, k) * decay, 0.0)
        q_dec = q * egc
        k_dec = k * jnp.exp(gtot - gc_c)
        state = s_scr[h]
        v_new = u - _dot(w, state)
        o_ref[:, sl] = _dot(q_dec, state) + _dot(qk, v_new)
        s_scr[h] = state * jnp.exp(gtot) + _dot(k_dec.T, v_new)

    @pl.when(n == pl.num_programs(3) - 1)
    def _():
        sfin_ref[...] = s_scr[...]


def dn_scan(qkv, p, par, s0, hb=4):
    b, t, _ = qkv.shape
    c = DN_CHUNK
    nc = t // c
    nblk = DN_HEADS // hb
    wblk = hb * HEAD_DIM

    def cidx(d, n):
        return jnp.where(d == 0, n, nc - 1 - n)

    kern = functools.partial(_dn_scan_kernel, hb=hb)
    return pl.pallas_call(
        kern,
        grid=(b, 2, nblk, nc),
        in_specs=[pl.BlockSpec((None, c, wblk), lambda bi, d, g, n: (bi, cidx(d, n), g)),
                  pl.BlockSpec((None, c, wblk), lambda bi, d, g, n: (bi, cidx(d, n), nblk + g)),
                  pl.BlockSpec((None, c, wblk), lambda bi, d, g, n: (bi, cidx(d, n), 2 * nblk + g)),
                  pl.BlockSpec((None, c, LANES), lambda bi, d, g, n: (bi, cidx(d, n), 32)),
                  pl.BlockSpec((8, LANES), lambda bi, d, g, n: (0, 0)),
                  pl.BlockSpec((None, None, hb, HEAD_DIM, HEAD_DIM), lambda bi, d, g, n: (bi, d, g, 0, 0))],
        out_specs=[pl.BlockSpec((None, None, c, wblk), lambda bi, d, g, n: (d, bi, cidx(d, n), g)),
                   pl.BlockSpec((None, None, hb, HEAD_DIM, HEAD_DIM), lambda bi, d, g, n: (bi, d, g, 0, 0))],
        out_shape=[jax.ShapeDtypeStruct((2, b, t, DN_HEADS * HEAD_DIM), F32),
                   jax.ShapeDtypeStruct((b, 2, DN_HEADS, HEAD_DIM, HEAD_DIM), F32)],
        scratch_shapes=[pltpu.VMEM((hb, HEAD_DIM, HEAD_DIM), F32)],
        compiler_params=_params(("parallel", "parallel", "parallel", "arbitrary")),
        name="dn_scan",
    )(qkv, qkv, qkv, p, par, s0)


def _dn_out_kernel(o_ref, z_ref, w_ref, y_ref):
    o = o_ref[0] + o_ref[1]
    z = z_ref[...]
    for h in range(DN_HEADS):
        sl = slice(h * HEAD_DIM, (h + 1) * HEAD_DIM)
        oh = o[:, sl]
        yh = oh * lax.rsqrt(jnp.mean(oh * oh, axis=-1, keepdims=True) + NORM_EPS) * w_ref[...]
        zh = z[:, sl]
        y_ref[:, sl] = (yh * (zh * _sigmoid(zh))).astype(y_ref.dtype)


def dn_out(o, p, norm_w, tr=256):
    _, b, t, w = o.shape
    tr = min(tr, t)
    return pl.pallas_call(
        _dn_out_kernel,
        grid=(b, t // tr),
        in_specs=[pl.BlockSpec((2, None, tr, w), lambda bi, i: (0, bi, i, 0)),
                  pl.BlockSpec((None, tr, w), lambda bi, i: (bi, i, 3)),
                  pl.BlockSpec((1, HEAD_DIM), lambda bi, i: (0, 0))],
        out_specs=pl.BlockSpec((None, tr, w), lambda bi, i: (bi, i, 0)),
        out_shape=jax.ShapeDtypeStruct((b, t, w), BF16),
        compiler_params=_params(("parallel", "parallel")),
        name="dn_out",
    )(o, p, norm_w.reshape(1, HEAD_DIM))


def _rope_kernel(x_ref, cos_ref, sin_ref, *o_refs, half, scale, nheads, split):
    lane = lax.broadcasted_iota(jnp.int32, (1, LANES), 1)
    first = (lane % (2 * half)) < half
    cos = cos_ref[...]
    sin = sin_ref[...]
    for h in range(nheads):
        sl = slice(h * LANES, (h + 1) * LANES)
        x = x_ref[:, sl]
        partner = jnp.where(first, pltpu.roll(x, LANES - half, 1), pltpu.roll(x, half, 1))
        y = (x * cos + partner * sin) * scale
        if split:
            o_refs[0][:, sl] = jnp.where(lane < LANES // 2, y, 0.0).astype(BF16)
            o_refs[1][:, sl] = jnp.where(lane >= LANES // 2, y, 0.0).astype(BF16)
        else:
            o_refs[0][:, sl] = y.astype(BF16)


def rope(x, col_block, width, cos, sin, *, half, scale=1.0, split=False, tr=512):
    b, t, _ = x.shape
    tr = min(tr, t)
    nheads = width // LANES
    nout = 2 if split else 1
    kern = functools.partial(_rope_kernel, half=half, scale=scale, nheads=nheads, split=split)
    out = pl.pallas_call(
        kern,
        grid=(b, t // tr),
        in_specs=[pl.BlockSpec((None, tr, width), lambda bi, i: (bi, i, col_block)),
                  pl.BlockSpec((tr, LANES), lambda bi, i: (i, 0)),
                  pl.BlockSpec((tr, LANES), lambda bi, i: (i, 0))],
        out_specs=[pl.BlockSpec((None, tr, width), lambda bi, i: (bi, i, 0))] * nout,
        out_shape=[jax.ShapeDtypeStruct((b, t, width), BF16)] * nout,
        compiler_params=_params(("parallel", "parallel")),
        name="rope",
    )(x, cos, sin)
    return out if split else out[0]


def rope_tables(t, block, half):
    pos = jnp.arange(t, dtype=jnp.int32)
    rows = (pos // GRID_W).astype(F32)
    cols = (pos % GRID_W).astype(F32)
    lane = np.arange(LANES)
    j = lane % block
    use_col = (j // (2 * half)) == 1
    i = j % (2 * half)
    inv = (ROPE_THETA ** (-jnp.arange(half, dtype=F32) / half))[i % half]
    p = jnp.where(jnp.asarray(use_col)[None, :], cols[:, None], rows[:, None])
    ang = p * inv[None, :]
    sign = jnp.asarray(np.where(i < half, -1.0, 1.0).astype(np.float32))[None, :]
    return jnp.cos(ang), jnp.sin(ang) * sign


def _diff_kernel(q1_ref, q2_ref, k_ref, v_ref, lam_ref, nw_ref, o_ref,
                 m1, l1, m2, l2, acc1, acc2, *, lam_init):
    j = pl.program_id(3)

    @pl.when(j == 0)
    def _():
        m1[...] = jnp.full_like(m1, -jnp.inf)
        m2[...] = jnp.full_like(m2, -jnp.inf)
        l1[...] = jnp.zeros_like(l1)
        l2[...] = jnp.zeros_like(l2)
        acc1[...] = jnp.zeros_like(acc1)
        acc2[...] = jnp.zeros_like(acc2)

    k = k_ref[...]
    v = v_ref[...]

    def update(q_ref, m, l, acc):
        s = _dot_nt(q_ref[...], k)
        m_new = jnp.maximum(m[...], jnp.max(s, axis=-1, keepdims=True))
        alpha = jnp.exp(m[...] - m_new)
        p = jnp.exp(s - m_new)
        l[...] = l[...] * alpha + jnp.sum(p, axis=-1, keepdims=True)
        acc[...] = acc[...] * alpha + _dot(p, v)
        m[...] = m_new

    update(q1_ref, m1, l1, acc1)
    update(q2_ref, m2, l2, acc2)

    @pl.when(j == pl.num_programs(3) - 1)
    def _():
        lv = lam_ref[...]
        s01 = jnp.sum(lv[0:1] * lv[1:2], axis=-1, keepdims=True)
        s23 = jnp.sum(lv[2:3] * lv[3:4], axis=-1, keepdims=True)
        lam = jnp.exp(s01) - jnp.exp(s23) + lam_init
        o = acc1[...] / l1[...] - lam * (acc2[...] / l2[...])
        y = o * lax.rsqrt(jnp.mean(o * o, axis=-1, keepdims=True) + NORM_EPS) * nw_ref[...]
        o_ref[...] = (y * (1.0 - lam_init)).astype(o_ref.dtype)


def diff_attention(q1, q2, k, v, lam_vec, norm_w, lam_init, tq=512, tk=768):
    b, tqa, w = q1.shape
    tka = k.shape[1]
    tq = min(tq, tqa)
    tk = min(tk, tka)
    nh = w // HEAD_DIM
    kern = functools.partial(_diff_kernel, lam_init=lam_init)
    return pl.pallas_call(
        kern,
        grid=(b, nh, tqa // tq, tka // tk),
        in_specs=[pl.BlockSpec((None, tq, HEAD_DIM), lambda bi, h, i, j: (bi, i, h)),
                  pl.BlockSpec((None, tq, HEAD_DIM), lambda bi, h, i, j: (bi, i, h)),
                  pl.BlockSpec((None, tk, HEAD_DIM), lambda bi, h, i, j: (bi, j, h)),
                  pl.BlockSpec((None, tk, HEAD_DIM), lambda bi, h, i, j: (bi, j, h)),
                  pl.BlockSpec((4, DIFF_DQK), lambda bi, h, i, j: (0, 0)),
                  pl.BlockSpec((1, HEAD_DIM), lambda bi, h, i, j: (0, 0))],
        out_specs=pl.BlockSpec((None, tq, HEAD_DIM), lambda bi, h, i, j: (bi, i, h)),
        out_shape=jax.ShapeDtypeStruct((b, tqa, w), BF16),
        scratch_shapes=[pltpu.VMEM((tq, 1), F32)] * 4 + [pltpu.VMEM((tq, HEAD_DIM), F32)] * 2,
        compiler_params=_params(("parallel", "parallel", "parallel", "arbitrary")),
        name="diff_attention",
    )(q1, q2, k, v, lam_vec, norm_w.reshape(1, HEAD_DIM))


def _win_kernel(sink_ref, q_ref, kp_ref, kc_ref, kn_ref, vp_ref, vc_ref, vn_ref, kx_ref, vx_ref, o_ref,
                *, t_total):
    kvh = pl.program_id(1)
    n = pl.program_id(2)
    wb = WINDOW
    keys = jnp.concatenate([kp_ref[...], kc_ref[...], kn_ref[...], kx_ref[...]], axis=0)
    vals = jnp.concatenate([vp_ref[...], vc_ref[...], vn_ref[...], vx_ref[...]], axis=0)
    nk = keys.shape[0]
    i = lax.broadcasted_iota(jnp.int32, (wb, nk), 0)
    j = lax.broadcasted_iota(jnp.int32, (wb, nk), 1)
    kpos = n * wb - wb + j
    valid = (jnp.abs(j - wb - i) <= WINDOW) & (kpos >= 0) & (kpos < t_total)
    valid = valid | (j >= 3 * wb)
    for g in range(GQA_GROUP):
        sl = slice(g * HEAD_DIM, (g + 1) * HEAD_DIM)
        sink = sink_ref[kvh * GQA_GROUP + g]
        s = jnp.where(valid, _dot_nt(q_ref[:, sl], keys), -jnp.inf)
        m = jnp.maximum(jnp.max(s, axis=-1, keepdims=True), sink)
        p = jnp.exp(s - m)
        denom = jnp.sum(p, axis=-1, keepdims=True) + jnp.exp(sink - m)
        o_ref[:, sl] = (_dot(p, vals) / denom).astype(o_ref.dtype)


def window_attention(q, k, v, kvx, sink):
    b, t, _ = q.shape
    lctx = kvx.shape[1]
    wb = WINDOW
    nb = t // wb
    qw = GQA_GROUP * HEAD_DIM
    kern = functools.partial(_win_kernel, t_total=t)
    prev = lambda bi, h, n: (bi, jnp.maximum(n - 1, 0), h)
    cur = lambda bi, h, n: (bi, n, h)
    nxt = lambda bi, h, n: (bi, jnp.minimum(n + 1, nb - 1), h)
    kv_spec = lambda f: pl.BlockSpec((None, wb, HEAD_DIM), f)
    return pl.pallas_call(
        kern,
        grid=(b, GQA_KV_HEADS, nb),
        in_specs=[pl.BlockSpec(memory_space=pltpu.SMEM),
                  pl.BlockSpec((None, wb, qw), cur),
                  kv_spec(prev), kv_spec(cur), kv_spec(nxt),
                  kv_spec(prev), kv_spec(cur), kv_spec(nxt),
                  pl.BlockSpec((None, lctx, HEAD_DIM), lambda bi, h, n: (bi, 0, h)),
                  pl.BlockSpec((None, lctx, HEAD_DIM), lambda bi, h, n: (bi, 0, GQA_KV_HEADS + h))],
        out_specs=pl.BlockSpec((None, wb, qw), cur),
        out_shape=jax.ShapeDtypeStruct(q.shape, BF16),
        compiler_params=_params(("parallel", "parallel", "parallel")),
        name="window_attention",
    )(sink, q, k, k, k, v, v, v, kvx, kvx)


def _top_values(s, count):
    vals = []
    work = s
    for _ in range(count):
        m = jnp.max(work, axis=0, keepdims=True)
        vals.append(m)
        work = jnp.where(work == m, -jnp.inf, work)
    return vals


def _route_kernel(q_ref, sub_ref, s_ref, e_ref, th_ref):
    kk = PEER_TOPK
    tb = q_ref.shape[0]
    row = lax.broadcasted_iota(jnp.int32, (kk, tb), 0)
    for h in range(PEER_HEADS):
        tabs = []
        tops = []
        for p in range(2):
            c0 = (h * 2 + p) * PEER_NKEYS
            s = _dot_nt(sub_ref[h, p], q_ref[:, c0:c0 + PEER_NKEYS])
            vals = _top_values(s, kk)
            tabs.append(jnp.where(s >= vals[kk - 1], s, NEG_BIG))
            tops.append(vals)
        v2 = jnp.zeros((kk, tb), F32)
        for i in range(kk):
            v2 = jnp.where(row == i, tops[1][i], v2)
        cand = jnp.concatenate([tops[0][i] + (v2 if i == 0 else v2[:8]) for i in range(kk)], axis=0)
        thr = _top_values(cand, kk)[kk - 1]
        top = tops[0][0] + tops[1][0]
        z = jnp.sum(jnp.where(cand >= thr, jnp.exp(cand - top), 0.0), axis=0, keepdims=True)
        s_ref[h, 0] = tabs[0]
        s_ref[h, 1] = tabs[1]
        e_ref[h, 0] = jnp.exp(tabs[0] - tops[0][0])
        e_ref[h, 1] = jnp.exp(tabs[1] - tops[1][0]) / z
        th_ref[h:h + 1, :] = thr


def peer_route(q, subkeys, tb=256):
    n = q.shape[0]
    tb = min(tb, n)
    hh = PEER_HEADS
    tab = jax.ShapeDtypeStruct((hh, 2, PEER_NKEYS, n), F32)
    tab_spec = pl.BlockSpec((hh, 2, PEER_NKEYS, tb), lambda i: (0, 0, 0, i))
    return pl.pallas_call(
        _route_kernel,
        grid=(n // tb,),
        in_specs=[pl.BlockSpec((tb, q.shape[1]), lambda i: (i, 0)),
                  pl.BlockSpec(subkeys.shape, lambda i: (0, 0, 0, 0))],
        out_specs=[tab_spec, tab_spec, pl.BlockSpec((hh, tb), lambda i: (0, i))],
        out_shape=[tab, tab, jax.ShapeDtypeStruct((hh, n), F32)],
        compiler_params=_params(("parallel",)),
        name="peer_route",
    )(q, subkeys)


def _gelu(x):
    return 0.5 * x * (1.0 + lax.erf(x * (2.0 ** -0.5)))


def _peer_kernel(x_ref, u_ref, vt_ref, s1_ref, e1_ref, s2_ref, e2_ref, th_ref, o_ref, acc, a_scr, *, rows):
    et = pl.program_id(1)

    @pl.when(et == 0)
    def _():
        acc[...] = jnp.zeros_like(acc)

    ht = _dot_nt(u_ref[...], x_ref[...])
    nk = PEER_NKEYS
    for r in range(rows):
        gate = None
        for h in range(PEER_HEADS):
            s = s1_ref[h, r:r + 1, :] + s2_ref[h]
            g = jnp.where(s >= th_ref[h:h + 1, :], e1_ref[h, r:r + 1, :] * e2_ref[h], 0.0)
            gate = g if gate is None else gate + g
        a_scr[r * nk:(r + 1) * nk, :] = (_gelu(ht[r * nk:(r + 1) * nk, :]) * gate).astype(BF16)
    acc[...] += jnp.dot(vt_ref[...], a_scr[...], preferred_element_type=F32)

    @pl.when(et == pl.num_programs(1) - 1)
    def _():
        o_ref[...] = acc[...].T


def peer_experts(x, u, vt, stab, etab, thr, tb=512, rows=8):
    n, d = x.shape
    e = u.shape[0]
    tb = min(tb, n)
    te = rows * PEER_NKEYS
    hh = PEER_HEADS
    kern = functools.partial(_peer_kernel, rows=rows)
    row_spec = pl.BlockSpec((hh, None, rows, tb), lambda i, j: (0, 0, j, i))
    full_spec = pl.BlockSpec((hh, None, PEER_NKEYS, tb), lambda i, j: (0, 1, 0, i))
    return pl.pallas_call(
        kern,
        grid=(n // tb, e // te),
        in_specs=[pl.BlockSpec((tb, d), lambda i, j: (i, 0)),
                  pl.BlockSpec((te, d), lambda i, j: (j, 0)),
                  pl.BlockSpec((d, te), lambda i, j: (0, j)),
                  row_spec, row_spec, full_spec, full_spec,
                  pl.BlockSpec((hh, tb), lambda i, j: (0, i))],
        out_specs=pl.BlockSpec((tb, d), lambda i, j: (i, 0)),
        out_shape=jax.ShapeDtypeStruct((n, d), F32),
        scratch_shapes=[pltpu.VMEM((d, tb), F32), pltpu.VMEM((te, tb), BF16)],
        compiler_params=_params(("parallel", "arbitrary")),
        name="peer_experts",
    )(x, u, vt, stab, etab, stab, etab, thr)


def _resid_kernel(x_ref, y_ref, g_ref, w_ref, o_ref, *, final):
    x = x_ref[...] + g_ref[...] * y_ref[...]
    if final:
        x = x * lax.rsqrt(jnp.mean(x * x, axis=-1, keepdims=True) + NORM_EPS) * w_ref[...]
    o_ref[...] = x


def resid(x, y, gate, final_w=None, rows=512):
    b, t, d = x.shape
    tr = min(rows, t)
    bm = gate.shape[0]
    mod_map = (lambda i, j: (i, 0, 0)) if bm == b else (lambda i, j: (0, 0, 0))
    w = jnp.ones((d,), F32) if final_w is None else final_w
    kern = functools.partial(_resid_kernel, final=final_w is not None)
    blk = pl.BlockSpec((None, tr, d), lambda i, j: (i, j, 0))
    return pl.pallas_call(
        kern,
        grid=(b, t // tr),
        in_specs=[blk, blk, pl.BlockSpec((None, 1, d), mod_map), pl.BlockSpec((1, d), lambda i, j: (0, 0))],
        out_specs=blk,
        out_shape=jax.ShapeDtypeStruct((b, t, d), F32),
        compiler_params=_params(("parallel", "parallel")),
        name="resid",
    )(x, y, gate.reshape(bm, 1, d), w.reshape(1, d))


def _peer_ffn(x, fx, gate, wq, bq, subkeys, u, vt, final_w=None):
    b, t, d = x.shape
    n = b * t
    q = matmul(fx.reshape(n, d), wq, out_dtype=BF16, bias=bq)
    stab, etab, thr = peer_route(q, subkeys)
    y = peer_experts(fx.reshape(n, d), u, vt, stab, etab, thr)
    return resid(x, y.reshape(b, t, d), gate, final_w=final_w)


def _out_proj(a, w_out, x, gate):
    b, t, d = x.shape
    g = gate if gate.shape[0] == b else jnp.broadcast_to(gate, (b, d))
    y = matmul(a.reshape(b * t, a.shape[-1]), w_out, out_dtype=F32, res=x.reshape(b * t, d), gate=g,
               rows_per_gate=t, tm=min(512, t))
    return y.reshape(b, t, d)


def kernel(x, c, ctx, c_ctx, ada_w, ada_b, norm_mix_w, norm_ffn_w, ab_w_in, ab_w_out, dn_conv_w, dn_a_log,
           dn_dt_bias, dn_norm_w, diff_lambda, diff_norm_w, gqa_w_in, gqa_w_out, gqa_sink, peer_wq, peer_bq,
           peer_subkeys, peer_u, peer_v, final_norm_w):
    b, t, d = x.shape
    lctx = ctx.shape[1]
    depth = ada_w.shape[0]
    n = b * t
    nctx = b * lctx

    cond = jnp.zeros((16, d), F32).at[:b].set(c).at[b].set(c_ctx)
    mods = []
    for l in range(depth):
        m = matmul(cond, ada_w, out_dtype=F32, bias=ada_b[l], silu_in=True, tn_cap=1024, layer=l)
        mods.append(m.reshape(16, 6, d))
    mod_x = lambda l, i: mods[l][:b, i]
    mod_c = lambda l, i: mods[l][b:b + 1, i]

    w_in = ab_w_in[0]
    qkv_w = DN_HEADS * HEAD_DIM * 3
    z_w = DN_HEADS * HEAD_DIM
    small = 4 * DN_HEADS
    off = qkv_w + z_w
    w_dn = jnp.concatenate([w_in[:, :off], w_in[:, off:off + small],
                            jnp.zeros((d, LANES - small), F32)], axis=1).astype(BF16)
    off += small
    dq_w = DIFF_HEADS * HEAD_DIM
    w_dqk = w_in[:, off:off + 2 * dq_w].astype(BF16)
    w_dv = w_in[:, off + 2 * dq_w:].astype(BF16)
    par = jnp.zeros((8, LANES), F32)
    par = par.at[0, 2 * DN_HEADS:4 * DN_HEADS].set(dn_dt_bias[0].reshape(-1))
    par = par.at[1, 2 * DN_HEADS:4 * DN_HEADS].set(dn_a_log[0].reshape(-1))
    lam_init = 0.8 - 0.6 * math.exp(-0.3 * 0)

    hx = normmod(x, norm_mix_w[0], mod_x(0, 0), mod_x(0, 1))
    hc = normmod(ctx, norm_mix_w[0], mod_c(0, 0), mod_c(0, 1))
    hx2 = hx.reshape(n, d)
    hc2 = hc.reshape(nctx, d)

    p_x = matmul(hx2, w_dn, out_dtype=F32).reshape(b, t, -1)
    p_c = matmul(hc2, w_dn, out_dtype=F32).reshape(b, lctx, -1)
    qkv_c = dn_conv(p_c, dn_conv_w[0])
    qkv_x = dn_conv(p_x, dn_conv_w[0])
    s0 = jnp.zeros((b, 2, DN_HEADS, HEAD_DIM, HEAD_DIM), F32)
    o_c, s_c = dn_scan(qkv_c, p_c, par, s0)
    o_x, _ = dn_scan(qkv_x, p_x, par, s_c)
    dn_x = dn_out(o_x, p_x, dn_norm_w[0])
    dn_c = dn_out(o_c, p_c, dn_norm_w[0])

    qk_x = matmul(hx2, w_dqk, out_dtype=F32).reshape(b, t, -1)
    qk_c = matmul(hc2, w_dqk, out_dtype=F32).reshape(b, lctx, -1)
    v_x = matmul(hx2, w_dv, out_dtype=BF16).reshape(b, t, -1)
    v_c = matmul(hc2, w_dv, out_dtype=BF16).reshape(b, lctx, -1)
    cos_d, sin_d = rope_tables(t, DIFF_DQK, DIFF_DQK // 4)
    one_c = jnp.ones((lctx, LANES), F32)
    zero_c = jnp.zeros((lctx, LANES), F32)
    dscale = DIFF_DQK ** -0.5
    q1_x, q2_x = rope(qk_x, 0, dq_w, cos_d, sin_d, half=DIFF_DQK // 4, scale=dscale, split=True)
    k_x = rope(qk_x, 1, dq_w, cos_d, sin_d, half=DIFF_DQK // 4)
    q1_c, q2_c = rope(qk_c, 0, dq_w, one_c, zero_c, half=DIFF_DQK // 4, scale=dscale, split=True)
    k_c = rope(qk_c, 1, dq_w, one_c, zero_c, half=DIFF_DQK // 4)
    k_all = jnp.concatenate([k_x, k_c], axis=1)
    v_all = jnp.concatenate([v_x, v_c], axis=1)
    d_x = diff_attention(q1_x, q2_x, k_all, v_all, diff_lambda[0], diff_norm_w[0], lam_init)
    d_c = diff_attention(q1_c, q2_c, k_c, v_c, diff_lambda[0], diff_norm_w[0], lam_init)

    w_out = ab_w_out[0].astype(BF16)
    x = _out_proj(jnp.concatenate([dn_x, d_x], axis=-1), w_out, x, mod_x(0, 2))
    ctx = _out_proj(jnp.concatenate([dn_c, d_c], axis=-1), w_out, ctx, mod_c(0, 2))

    wq = peer_wq[0].astype(BF16)
    sub = peer_subkeys[0].astype(BF16)
    u_tab = peer_u[0].astype(BF16)
    vt_tab = peer_v[0].T.astype(BF16)
    fx = normmod(x, norm_ffn_w[0], mod_x(0, 3), mod_x(0, 4))
    x = _peer_ffn(x, fx, mod_x(0, 5), wq, peer_bq[0], sub, u_tab, vt_tab)
    fc = normmod(ctx, norm_ffn_w[0], mod_c(0, 3), mod_c(0, 4))
    ctx = _peer_ffn(ctx, fc, mod_c(0, 5), wq, peer_bq[0], sub, u_tab, vt_tab)

    gq_w = GQA_Q_HEADS * HEAD_DIM
    gkv_w = GQA_KV_HEADS * HEAD_DIM
    w_g = gqa_w_in[0]
    w_gqk = w_g[:, :gq_w + gkv_w].astype(BF16)
    w_gv = w_g[:, gq_w + gkv_w:].astype(BF16)
    w_gkv = w_g[:, gq_w:].astype(BF16)
    hx = normmod(x, norm_mix_w[1], mod_x(1, 0), mod_x(1, 1))
    hc = normmod(ctx, norm_mix_w[1], mod_c(1, 0), mod_c(1, 1))
    hx2 = hx.reshape(n, d)
    qk = matmul(hx2, w_gqk, out_dtype=F32, tn_cap=1280).reshape(b, t, -1)
    gv = matmul(hx2, w_gv, out_dtype=BF16).reshape(b, t, -1)
    kvx = matmul(hc.reshape(nctx, d), w_gkv, out_dtype=BF16).reshape(b, lctx, -1)
    cos_g, sin_g = rope_tables(t, HEAD_DIM, HEAD_DIM // 4)
    gq = rope(qk, 0, gq_w, cos_g, sin_g, half=HEAD_DIM // 4, scale=HEAD_DIM ** -0.5)
    gk = rope(qk, gq_w // gkv_w, gkv_w, cos_g, sin_g, half=HEAD_DIM // 4)
    att = window_attention(gq, gk, gv, kvx, gqa_sink[0])
    x = _out_proj(att, gqa_w_out[0].astype(BF16), x, mod_x(1, 2))

    wq = peer_wq[1].astype(BF16)
    sub = peer_subkeys[1].astype(BF16)
    u_tab = peer_u[1].astype(BF16)
    vt_tab = peer_v[1].T.astype(BF16)
    fx = normmod(x, norm_ffn_w[1], mod_x(1, 3), mod_x(1, 4))
    return _peer_ffn(x, fx, mod_x(1, 5), wq, peer_bq[1], sub, u_tab, vt_tab, final_w=final_norm_w)
```

```python
import functools
import math

import jax
import jax.numpy as jnp
import numpy as np
from jax import lax
from jax.experimental import pallas as pl
from jax.experimental.pallas import tpu as pltpu

F32 = jnp.float32
BF16 = jnp.bfloat16

LANES = 128
HEAD_DIM = 128
NORM_EPS = 1e-6
ROPE_THETA = 10000.0
GRID_W = 64
DN_HEADS = 8
DN_CHUNK = 64
DN_CONV = 5
DIFF_HEADS = 8
DIFF_DQK = 64
GQA_Q_HEADS = 16
GQA_KV_HEADS = 4
GQA_GROUP = GQA_Q_HEADS // GQA_KV_HEADS
WINDOW = 128
PEER_HEADS = 8
PEER_NKEYS = 128
PEER_TOPK = 16
NEG_BIG = -1e30
VMEM_LIMIT = 56 * 1024 * 1024


def _params(sem):
    return pltpu.CompilerParams(dimension_semantics=sem, vmem_limit_bytes=VMEM_LIMIT)


def _dot(a, b):
    return jnp.dot(a.astype(BF16), b.astype(BF16), preferred_element_type=F32)


def _dot_nt(a, b):
    return lax.dot_general(a.astype(BF16), b.astype(BF16), (((1,), (1,)), ((), ())),
                           preferred_element_type=F32)


def _sigmoid(x):
    return 1.0 / (1.0 + jnp.exp(-x))


def _normmod_kernel(x_ref, w_ref, shift_ref, scale_ref, o_ref):
    x = x_ref[...]
    y = x * lax.rsqrt(jnp.mean(x * x, axis=-1, keepdims=True) + NORM_EPS)
    y = y * w_ref[...]
    o_ref[...] = (y * (1.0 + scale_ref[...]) + shift_ref[...]).astype(o_ref.dtype)


def normmod(x, w, shift, scale, rows=512):
    b, t, d = x.shape
    tr = min(rows, t)
    bm = shift.shape[0]
    mod_map = (lambda i, j: (i, 0, 0)) if bm == b else (lambda i, j: (0, 0, 0))
    return pl.pallas_call(
        _normmod_kernel,
        grid=(b, t // tr),
        in_specs=[pl.BlockSpec((None, tr, d), lambda i, j: (i, j, 0)),
                  pl.BlockSpec((1, d), lambda i, j: (0, 0)),
                  pl.BlockSpec((None, 1, d), mod_map),
                  pl.BlockSpec((None, 1, d), mod_map)],
        out_specs=pl.BlockSpec((None, tr, d), lambda i, j: (i, j, 0)),
        out_shape=jax.ShapeDtypeStruct((b, t, d), BF16),
        compiler_params=_params(("parallel", "parallel")),
        name="normmod",
    )(x, w.reshape(1, d), shift.reshape(bm, 1, d), scale.reshape(bm, 1, d))


def _mm_kernel(*refs, silu_in, has_bias, has_res):
    a_ref, w_ref = refs[0], refs[1]
    pos = 2
    a = a_ref[...]
    if silu_in:
        a = a * _sigmoid(a)
    acc = _dot(a, w_ref[...])
    if has_bias:
        acc = acc + refs[pos][...]
        pos += 1
    if has_res:
        acc = refs[pos][...] + refs[pos + 1][...] * acc
        pos += 2
    o_ref = refs[pos]
    o_ref[...] = acc.astype(o_ref.dtype)


def _pick_tn(n, cap):
    best = LANES
    for cand in range(LANES, min(n, cap) + 1, LANES):
        if n % cand == 0:
            best = cand
    return best


def matmul(a, w, *, out_dtype, bias=None, res=None, gate=None, rows_per_gate=None,
           silu_in=False, tm=512, tn_cap=2048, layer=0):
    m, k = a.shape
    n = w.shape[-1]
    tm = min(tm, m)
    tn = _pick_tn(n, tn_cap)
    if w.ndim == 3:
        w_spec = pl.BlockSpec((None, k, tn), lambda j, i: (layer, 0, j))
    else:
        w_spec = pl.BlockSpec((k, tn), lambda j, i: (0, j))
    in_specs = [pl.BlockSpec((tm, k), lambda j, i: (i, 0)), w_spec]
    args = [a, w]
    if bias is not None:
        in_specs.append(pl.BlockSpec((1, tn), lambda j, i: (0, j)))
        args.append(bias.reshape(1, n).astype(F32))
    if res is not None:
        blocks_per_gate = rows_per_gate // tm
        in_specs.append(pl.BlockSpec((tm, tn), lambda j, i: (i, j)))
        in_specs.append(pl.BlockSpec((None, 1, tn), lambda j, i: (i // blocks_per_gate, 0, j)))
        args += [res, gate.reshape(gate.shape[0], 1, n)]
    kern = functools.partial(_mm_kernel, silu_in=silu_in, has_bias=bias is not None,
                             has_res=res is not None)
    return pl.pallas_call(
        kern,
        grid=(n // tn, m // tm),
        in_specs=in_specs,
        out_specs=pl.BlockSpec((tm, tn), lambda j, i: (i, j)),
        out_shape=jax.ShapeDtypeStruct((m, n), out_dtype),
        compiler_params=_params(("parallel", "parallel")),
        name="matmul",
    )(*args)


def _conv_kernel(prev_ref, cur_ref, next_ref, w_ref, o_ref, *, tt, nheads_blk):
    i = pl.program_id(1)
    grp = pl.program_id(2)
    last = pl.num_programs(1) - 1
    prev = prev_ref[...] * (i > 0).astype(F32)
    nxt = next_ref[...] * (i < last).astype(F32)
    xx = jnp.concatenate([prev, cur_ref[...], nxt], axis=0)
    rows = tt + 16
    pad = (DN_CONV - 1) // 2
    y = None
    for k in range(DN_CONV):
        shift = (pad - k) % rows
        z = xx if shift == 0 else pltpu.roll(xx, shift, 0)
        term = z[8:8 + tt] * w_ref[k:k + 1, :]
        y = term if y is None else y + term
    y = y * _sigmoid(y)
    qscale = jnp.where(grp == 0, HEAD_DIM ** -0.5, 1.0).astype(F32)
    for h in range(nheads_blk):
        yh = y[:, h * HEAD_DIM:(h + 1) * HEAD_DIM]
        ss = jnp.sum(yh * yh, axis=-1, keepdims=True)
        fac = jnp.where(grp == 2, 1.0, lax.rsqrt(ss + NORM_EPS) * qscale)
        o_ref[:, h * HEAD_DIM:(h + 1) * HEAD_DIM] = yh * fac


def dn_conv(p, conv_w, tt=256):
    b, t, _ = p.shape
    tt = min(tt, t)
    cb = DN_HEADS * HEAD_DIM
    n8 = t // 8
    kern = functools.partial(_conv_kernel, tt=tt, nheads_blk=DN_HEADS)
    return pl.pallas_call(
        kern,
        grid=(b, t // tt, 3),
        in_specs=[pl.BlockSpec((None, 8, cb), lambda bi, i, g: (bi, jnp.maximum(i * (tt // 8) - 1, 0), g)),
                  pl.BlockSpec((None, tt, cb), lambda bi, i, g: (bi, i, g)),
                  pl.BlockSpec((None, 8, cb), lambda bi, i, g: (bi, jnp.minimum((i + 1) * (tt // 8), n8 - 1), g)),
                  pl.BlockSpec((DN_CONV, cb), lambda bi, i, g: (0, g))],
        out_specs=pl.BlockSpec((None, tt, cb), lambda bi, i, g: (bi, i, g)),
        out_shape=jax.ShapeDtypeStruct((b, t, 3 * cb), F32),
        compiler_params=_params(("parallel", "parallel", "parallel")),
        name="dn_conv",
    )(p, p, p, conv_w)


def _split3(x):
    hi = x.astype(BF16)
    r1 = x - hi.astype(F32)
    mid = r1.astype(BF16)
    lo = (r1 - mid.astype(F32)).astype(BF16)
    return hi, mid, lo


def _dn_chains(direction, q_ref, k_ref, v_ref, p_ref, par_ref, o_ref, s_scr):
    c = DN_CHUNK
    ii = lax.broadcasted_iota(jnp.int32, (c, c), 0)
    jj = lax.broadcasted_iota(jnp.int32, (c, c), 1)
    incl = (ii >= jj) if direction == 0 else (ii <= jj)
    strict = (ii > jj) if direction == 0 else (ii < jj)

    raw = p_ref[...]
    beta_all = _sigmoid(raw)
    xa = raw + par_ref[0:1, :]
    softplus = jnp.maximum(xa, 0.0) + jnp.log1p(jnp.exp(-jnp.abs(xa)))
    g_all = -jnp.exp(par_ref[1:2, :]) * softplus
    inclb = incl.astype(BF16)
    hi, mid, lo = _split3(g_all)
    gc_all = (jnp.dot(inclb, hi, preferred_element_type=F32)
              + jnp.dot(inclb, mid, preferred_element_type=F32)
              + jnp.dot(inclb, lo, preferred_element_type=F32))
    gtot_all = jnp.sum(g_all, axis=0, keepdims=True)
    gc_t = gc_all.T

    chains = []
    for h in range(DN_HEADS):
        ib = direction * DN_HEADS + h
        ig = 2 * DN_HEADS + ib
        sl = slice(h * HEAD_DIM, (h + 1) * HEAD_DIM)
        gc_c = gc_all[:, ig:ig + 1]
        chains.append(dict(
            incl=incl, strict=strict, beta=beta_all[:, ib:ib + 1], gc_c=gc_c,
            decay=jnp.exp(jnp.where(incl, gc_c - gc_t[ig:ig + 1, :], -jnp.inf)),
            gtot=gtot_all[:, ig:ig + 1], q=q_ref[:, sl], k=k_ref[:, sl], v=v_ref[:, sl],
            o_ref=o_ref, sl=sl, state_idx=(direction, h)))
    return chains


def _dn_step(chains, s_scr):
    c = DN_CHUNK
    ii = lax.broadcasted_iota(jnp.int32, (c, c), 0)
    jj = lax.broadcasted_iota(jnp.int32, (c, c), 1)
    eye = (ii == jj).astype(F32)
    for ch in chains:
        ch["kb"] = ch["k"] * ch["beta"]
        ch["kq"] = _dot_nt(jnp.concatenate([ch["kb"], ch["q"]], axis=0), ch["k"])
    for ch in chains:
        a = jnp.where(ch["strict"], ch["kq"][:c] * ch["decay"], 0.0)
        ch["qk"] = jnp.where(ch["incl"], ch["kq"][c:] * ch["decay"], 0.0)
        ch["inv"] = eye - a
        ch["a_pow"] = a
    for _ in range(int(math.log2(c)) - 1):
        for ch in chains:
            ch["a_pow"] = _dot(ch["a_pow"], ch["a_pow"])
        for ch in chains:
            ch["inv"] = _dot(ch["inv"], eye + ch["a_pow"])
    for ch in chains:
        egc = jnp.exp(ch["gc_c"])
        ch["uw"] = _dot(ch["inv"], jnp.concatenate([ch["v"] * ch["beta"], ch["kb"] * egc], axis=1))
        ch["q_dec"] = ch["q"] * egc
    for ch in chains:
        ch["state"] = s_scr[ch["state_idx"]]
        w = ch["uw"][:, HEAD_DIM:]
        ch["ws"] = _dot(jnp.concatenate([w, ch["q_dec"]], axis=0), ch["state"])
    for ch in chains:
        ch["v_new"] = ch["uw"][:, :HEAD_DIM] - ch["ws"][:c]
        ch["o_ref"][:, ch["sl"]] = ch["ws"][c:] + _dot(ch["qk"], ch["v_new"])
    for ch in chains:
        k_dec = ch["k"] * jnp.exp(ch["gtot"] - ch["gc_c"])
        s_scr[ch["state_idx"]] = ch["state"] * jnp.exp(ch["gtot"]) + _dot(k_dec.T, ch["v_new"])


def _dn_scan_kernel(qf_ref, kf_ref, vf_ref, pf_ref, qb_ref, kb_ref, vb_ref, pb_ref, par_ref, s0_ref,
                    of_ref, ob_ref, sfin_ref, s_scr):
    n = pl.program_id(1)

    @pl.when(n == 0)
    def _():
        s_scr[...] = s0_ref[...]

    chains = (_dn_chains(0, qf_ref, kf_ref, vf_ref, pf_ref, par_ref, of_ref, s_scr)
              + _dn_chains(1, qb_ref, kb_ref, vb_ref, pb_ref, par_ref, ob_ref, s_scr))
    _dn_step(chains, s_scr)

    @pl.when(n == pl.num_programs(1) - 1)
    def _():
        sfin_ref[...] = s_scr[...]


def dn_scan(qkv, p, par, s0):
    b, t, _ = qkv.shape
    c = DN_CHUNK
    nc = t // c
    w = DN_HEADS * HEAD_DIM
    fwd = lambda col: (lambda bi, n: (bi, n, col))
    bwd = lambda col: (lambda bi, n: (bi, nc - 1 - n, col))
    blk = lambda f: pl.BlockSpec((None, c, w), f)
    small = lambda f: pl.BlockSpec((None, c, LANES), f)
    state_spec = pl.BlockSpec((None, 2, DN_HEADS, HEAD_DIM, HEAD_DIM), lambda bi, n: (bi, 0, 0, 0, 0))
    o_shape = jax.ShapeDtypeStruct((b, t, w), F32)
    return pl.pallas_call(
        _dn_scan_kernel,
        grid=(b, nc),
        in_specs=[blk(fwd(0)), blk(fwd(1)), blk(fwd(2)), small(fwd(32)),
                  blk(bwd(0)), blk(bwd(1)), blk(bwd(2)), small(bwd(32)),
                  pl.BlockSpec((8, LANES), lambda bi, n: (0, 0)), state_spec],
        out_specs=[blk(fwd(0)), blk(bwd(0)), state_spec],
        out_shape=[o_shape, o_shape, jax.ShapeDtypeStruct(s0.shape, F32)],
        scratch_shapes=[pltpu.VMEM((2, DN_HEADS, HEAD_DIM, HEAD_DIM), F32)],
        compiler_params=_params(("parallel", "arbitrary")),
        name="dn_scan",
    )(qkv, qkv, qkv, p, qkv, qkv, qkv, p, par, s0)


def _dn_out_kernel(of_ref, ob_ref, z_ref, w_ref, y_ref):
    o = of_ref[...] + ob_ref[...]
    z = z_ref[...]
    for h in range(DN_HEADS):
        sl = slice(h * HEAD_DIM, (h + 1) * HEAD_DIM)
        oh = o[:, sl]
        yh = oh * lax.rsqrt(jnp.mean(oh * oh, axis=-1, keepdims=True) + NORM_EPS) * w_ref[...]
        zh = z[:, sl]
        y_ref[:, sl] = (yh * (zh * _sigmoid(zh))).astype(y_ref.dtype)


def dn_out(o_f, o_b, p, norm_w, tr=256):
    b, t, w = o_f.shape
    tr = min(tr, t)
    return pl.pallas_call(
        _dn_out_kernel,
        grid=(b, t // tr),
        in_specs=[pl.BlockSpec((None, tr, w), lambda bi, i: (bi, i, 0)),
                  pl.BlockSpec((None, tr, w), lambda bi, i: (bi, i, 0)),
                  pl.BlockSpec((None, tr, w), lambda bi, i: (bi, i, 3)),
                  pl.BlockSpec((1, HEAD_DIM), lambda bi, i: (0, 0))],
        out_specs=pl.BlockSpec((None, tr, w), lambda bi, i: (bi, i, 0)),
        out_shape=jax.ShapeDtypeStruct((b, t, w), BF16),
        compiler_params=_params(("parallel", "parallel")),
        name="dn_out",
    )(o_f, o_b, p, norm_w.reshape(1, HEAD_DIM))


def _rope_kernel(x_ref, cos_ref, sin_ref, *o_refs, half, scale, nheads, split):
    lane = lax.broadcasted_iota(jnp.int32, (1, LANES), 1)
    first = (lane % (2 * half)) < half
    cos = cos_ref[...]
    sin = sin_ref[...]
    for h in range(nheads):
        sl = slice(h * LANES, (h + 1) * LANES)
        x = x_ref[:, sl]
        partner = jnp.where(first, pltpu.roll(x, LANES - half, 1), pltpu.roll(x, half, 1))
        y = (x * cos + partner * sin) * scale
        if split:
            o_refs[0][:, sl] = jnp.where(lane < LANES // 2, y, 0.0).astype(BF16)
            o_refs[1][:, sl] = jnp.where(lane >= LANES // 2, y, 0.0).astype(BF16)
        else:
            o_refs[0][:, sl] = y.astype(BF16)


def rope(x, col_block, width, cos, sin, *, half, scale=1.0, split=False, tr=512):
    b, t, _ = x.shape
    tr = min(tr, t)
    nheads = width // LANES
    nout = 2 if split else 1
    kern = functools.partial(_rope_kernel, half=half, scale=scale, nheads=nheads, split=split)
    out = pl.pallas_call(
        kern,
        grid=(b, t // tr),
        in_specs=[pl.BlockSpec((None, tr, width), lambda bi, i: (bi, i, col_block)),
                  pl.BlockSpec((tr, LANES), lambda bi, i: (i, 0)),
                  pl.BlockSpec((tr, LANES), lambda bi, i: (i, 0))],
        out_specs=[pl.BlockSpec((None, tr, width), lambda bi, i: (bi, i, 0))] * nout,
        out_shape=[jax.ShapeDtypeStruct((b, t, width), BF16)] * nout,
        compiler_params=_params(("parallel", "parallel")),
        name="rope",
    )(x, cos, sin)
    return out if split else out[0]


def rope_tables(t, block, half):
    pos = jnp.arange(t, dtype=jnp.int32)
    rows = (pos // GRID_W).astype(F32)
    cols = (pos % GRID_W).astype(F32)
    lane = np.arange(LANES)
    j = lane % block
    use_col = (j // (2 * half)) == 1
    i = j % (2 * half)
    inv = (ROPE_THETA ** (-jnp.arange(half, dtype=F32) / half))[i % half]
    p = jnp.where(jnp.asarray(use_col)[None, :], cols[:, None], rows[:, None])
    ang = p * inv[None, :]
    sign = jnp.asarray(np.where(i < half, -1.0, 1.0).astype(np.float32))[None, :]
    return jnp.cos(ang), jnp.sin(ang) * sign


def _diff_kernel(q1_ref, q2_ref, k_ref, vt_ref, lam_ref, nw_ref, o_ref, acc1, acc2, *, lam_init, tk, nkv):
    tq = q1_ref.shape[0]
    acc1[...] = jnp.zeros_like(acc1)
    acc2[...] = jnp.zeros_like(acc2)

    def body(c, carry):
        m1, l1, m2, l2 = carry
        kc = k_ref[pl.ds(pl.multiple_of(c * tk, tk), tk), :]
        vtc = vt_ref[c]

        def update(q_ref, m, l, acc):
            s = _dot_nt(kc, q_ref[...])
            m_new = jnp.maximum(m, jnp.max(s, axis=0, keepdims=True))
            alpha = jnp.exp(m - m_new)
            p = jnp.exp(s - m_new)
            l_new = l * alpha + jnp.sum(p, axis=0, keepdims=True)
            acc[...] = acc[...] * alpha + _dot(vtc, p)
            return m_new, l_new

        m1, l1 = update(q1_ref, m1, l1, acc1)
        m2, l2 = update(q2_ref, m2, l2, acc2)
        return m1, l1, m2, l2

    neg = jnp.full((1, tq), -jnp.inf, F32)
    zero = jnp.zeros((1, tq), F32)
    m1, l1, m2, l2 = lax.fori_loop(0, nkv, body, (neg, zero, neg, zero))

    lv = lam_ref[...]
    s01 = jnp.sum(lv[0:1] * lv[1:2], axis=-1, keepdims=True)
    s23 = jnp.sum(lv[2:3] * lv[3:4], axis=-1, keepdims=True)
    lam = jnp.exp(s01) - jnp.exp(s23) + lam_init
    o = acc1[...] / l1 - lam * (acc2[...] / l2)
    y = o * lax.rsqrt(jnp.mean(o * o, axis=0, keepdims=True) + NORM_EPS)
    o_ref[...] = (y.T * nw_ref[...] * (1.0 - lam_init)).astype(o_ref.dtype)


def diff_attention(q1, q2, k, v, lam_vec, norm_w, lam_init, tq=1024, tk=768):
    b, tqa, w = q1.shape
    tka = k.shape[1]
    tq = min(tq, tqa)
    tk = min(tk, tka)
    nkv = tka // tk
    nh = w // HEAD_DIM
    vt = v.reshape(b, nkv, tk, nh, HEAD_DIM).transpose(0, 3, 1, 4, 2)
    kern = functools.partial(_diff_kernel, lam_init=lam_init, tk=tk, nkv=nkv)
    return pl.pallas_call(
        kern,
        grid=(b, nh, tqa // tq),
        in_specs=[pl.BlockSpec((None, tq, HEAD_DIM), lambda bi, h, i: (bi, i, h)),
                  pl.BlockSpec((None, tq, HEAD_DIM), lambda bi, h, i: (bi, i, h)),
                  pl.BlockSpec((None, tka, HEAD_DIM), lambda bi, h, i: (bi, 0, h)),
                  pl.BlockSpec((None, None, nkv, HEAD_DIM, tk), lambda bi, h, i: (bi, h, 0, 0, 0)),
                  pl.BlockSpec((4, DIFF_DQK), lambda bi, h, i: (0, 0)),
                  pl.BlockSpec((1, HEAD_DIM), lambda bi, h, i: (0, 0))],
        out_specs=pl.BlockSpec((None, tq, HEAD_DIM), lambda bi, h, i: (bi, i, h)),
        out_shape=jax.ShapeDtypeStruct((b, tqa, w), BF16),
        scratch_shapes=[pltpu.VMEM((HEAD_DIM, tq), F32)] * 2,
        compiler_params=_params(("parallel", "parallel", "parallel")),
        name="diff_attention",
    )(q1, q2, k, vt, lam_vec, norm_w.reshape(1, HEAD_DIM))


def _win_kernel(sink_ref, q_ref, kp_ref, kc_ref, kn_ref, vp_ref, vc_ref, vn_ref, kx_ref, vx_ref, o_ref,
                *, t_total):
    kvh = pl.program_id(1)
    n = pl.program_id(2)
    wb = WINDOW
    keys = jnp.concatenate([kp_ref[...], kc_ref[...], kn_ref[...], kx_ref[...]], axis=0)
    vals = jnp.concatenate([vp_ref[...], vc_ref[...], vn_ref[...], vx_ref[...]], axis=0)
    nk = keys.shape[0]
    i = lax.broadcasted_iota(jnp.int32, (wb, nk), 0)
    j = lax.broadcasted_iota(jnp.int32, (wb, nk), 1)
    kpos = n * wb - wb + j
    valid = (jnp.abs(j - wb - i) <= WINDOW) & (kpos >= 0) & (kpos < t_total)
    valid = valid | (j >= 3 * wb)
    for g in range(GQA_GROUP):
        sl = slice(g * HEAD_DIM, (g + 1) * HEAD_DIM)
        sink = sink_ref[kvh * GQA_GROUP + g]
        s = jnp.where(valid, _dot_nt(q_ref[:, sl], keys), -jnp.inf)
        m = jnp.maximum(jnp.max(s, axis=-1, keepdims=True), sink)
        p = jnp.exp(s - m)
        denom = jnp.sum(p, axis=-1, keepdims=True) + jnp.exp(sink - m)
        o_ref[:, sl] = (_dot(p, vals) / denom).astype(o_ref.dtype)


def window_attention(q, k, v, kvx, sink):
    b, t, _ = q.shape
    lctx = kvx.shape[1]
    wb = WINDOW
    nb = t // wb
    qw = GQA_GROUP * HEAD_DIM
    kern = functools.partial(_win_kernel, t_total=t)
    prev = lambda bi, h, n: (bi, jnp.maximum(n - 1, 0), h)
    cur = lambda bi, h, n: (bi, n, h)
    nxt = lambda bi, h, n: (bi, jnp.minimum(n + 1, nb - 1), h)
    kv_spec = lambda f: pl.BlockSpec((None, wb, HEAD_DIM), f)
    return pl.pallas_call(
        kern,
        grid=(b, GQA_KV_HEADS, nb),
        in_specs=[pl.BlockSpec(memory_space=pltpu.SMEM),
                  pl.BlockSpec((None, wb, qw), cur),
                  kv_spec(prev), kv_spec(cur), kv_spec(nxt),
                  kv_spec(prev), kv_spec(cur), kv_spec(nxt),
                  pl.BlockSpec((None, lctx, HEAD_DIM), lambda bi, h, n: (bi, 0, h)),
                  pl.BlockSpec((None, lctx, HEAD_DIM), lambda bi, h, n: (bi, 0, GQA_KV_HEADS + h))],
        out_specs=pl.BlockSpec((None, wb, qw), cur),
        out_shape=jax.ShapeDtypeStruct(q.shape, BF16),
        compiler_params=_params(("parallel", "parallel", "parallel")),
        name="window_attention",
    )(sink, q, k, k, k, v, v, v, kvx, kvx)


def _top_values(s, count):
    vals = []
    work = s
    for _ in range(count):
        m = jnp.max(work, axis=0, keepdims=True)
        vals.append(m)
        work = jnp.where(work == m, -jnp.inf, work)
    return vals


def _route_kernel(q_ref, sub_ref, s_ref, e_ref, th_ref):
    kk = PEER_TOPK
    tb = q_ref.shape[0]
    row = lax.broadcasted_iota(jnp.int32, (kk, tb), 0)
    for h in range(PEER_HEADS):
        tabs = []
        tops = []
        for p in range(2):
            c0 = (h * 2 + p) * PEER_NKEYS
            s = _dot_nt(sub_ref[h, p], q_ref[:, c0:c0 + PEER_NKEYS])
            vals = _top_values(s, kk)
            tabs.append(jnp.where(s >= vals[kk - 1], s, NEG_BIG))
            tops.append(vals)
        v2 = jnp.zeros((kk, tb), F32)
        for i in range(kk):
            v2 = jnp.where(row == i, tops[1][i], v2)
        cand = jnp.concatenate([tops[0][i] + (v2 if i == 0 else v2[:8]) for i in range(kk)], axis=0)
        thr = _top_values(cand, kk)[kk - 1]
        top = tops[0][0] + tops[1][0]
        z = jnp.sum(jnp.where(cand >= thr, jnp.exp(cand - top), 0.0), axis=0, keepdims=True)
        s_ref[h, 0] = tabs[0]
        s_ref[h, 1] = tabs[1]
        e_ref[h, 0] = jnp.exp(tabs[0] - tops[0][0])
        e_ref[h, 1] = jnp.exp(tabs[1] - tops[1][0]) / z
        th_ref[h:h + 1, :] = thr


def peer_route(q, subkeys, tb=256):
    n = q.shape[0]
    tb = min(tb, n)
    hh = PEER_HEADS
    tab = jax.ShapeDtypeStruct((hh, 2, PEER_NKEYS, n), F32)
    tab_spec = pl.BlockSpec((hh, 2, PEER_NKEYS, tb), lambda i: (0, 0, 0, i))
    return pl.pallas_call(
        _route_kernel,
        grid=(n // tb,),
        in_specs=[pl.BlockSpec((tb, q.shape[1]), lambda i: (i, 0)),
                  pl.BlockSpec(subkeys.shape, lambda i: (0, 0, 0, 0))],
        out_specs=[tab_spec, tab_spec, pl.BlockSpec((hh, tb), lambda i: (0, i))],
        out_shape=[tab, tab, jax.ShapeDtypeStruct((hh, n), F32)],
        compiler_params=_params(("parallel",)),
        name="peer_route",
    )(q, subkeys)


def _gelu(x):
    return 0.5 * x * (1.0 + lax.erf(x * (2.0 ** -0.5)))


def _peer_kernel(x_ref, u_ref, vt_ref, s1_ref, e1_ref, s2_ref, e2_ref, th_ref, o_ref, acc, a_scr, *, rows):
    et = pl.program_id(1)

    @pl.when(et == 0)
    def _():
        acc[...] = jnp.zeros_like(acc)

    ht = _dot_nt(u_ref[...], x_ref[...])
    nk = PEER_NKEYS
    for r in range(rows):
        gate = None
        for h in range(PEER_HEADS):
            s = s1_ref[h, r:r + 1, :] + s2_ref[h]
            g = jnp.where(s >= th_ref[h:h + 1, :], e1_ref[h, r:r + 1, :] * e2_ref[h], 0.0)
            gate = g if gate is None else gate + g
        a_scr[r * nk:(r + 1) * nk, :] = (_gelu(ht[r * nk:(r + 1) * nk, :]) * gate).astype(BF16)
    acc[...] += jnp.dot(vt_ref[...], a_scr[...], preferred_element_type=F32)

    @pl.when(et == pl.num_programs(1) - 1)
    def _():
        o_ref[...] = acc[...].T


def peer_experts(x, u, vt, stab, etab, thr, tb=512, rows=8):
    n, d = x.shape
    e = u.shape[0]
    tb = min(tb, n)
    te = rows * PEER_NKEYS
    hh = PEER_HEADS
    kern = functools.partial(_peer_kernel, rows=rows)
    row_spec = pl.BlockSpec((hh, None, rows, tb), lambda i, j: (0, 0, j, i))
    full_spec = pl.BlockSpec((hh, None, PEER_NKEYS, tb), lambda i, j: (0, 1, 0, i))
    return pl.pallas_call(
        kern,
        grid=(n // tb, e // te),
        in_specs=[pl.BlockSpec((tb, d), lambda i, j: (i, 0)),
                  pl.BlockSpec((te, d), lambda i, j: (j, 0)),
                  pl.BlockSpec((d, te), lambda i, j: (0, j)),
                  row_spec, row_spec, full_spec, full_spec,
                  pl.BlockSpec((hh, tb), lambda i, j: (0, i))],
        out_specs=pl.BlockSpec((tb, d), lambda i, j: (i, 0)),
        out_shape=jax.ShapeDtypeStruct((n, d), F32),
        scratch_shapes=[pltpu.VMEM((d, tb), F32), pltpu.VMEM((te, tb), BF16)],
        compiler_params=_params(("parallel", "arbitrary")),
        name="peer_experts",
    )(x, u, vt, stab, etab, stab, etab, thr)


def _resid_kernel(x_ref, y_ref, g_ref, w_ref, o_ref, *, final):
    x = x_ref[...] + g_ref[...] * y_ref[...]
    if final:
        x = x * lax.rsqrt(jnp.mean(x * x, axis=-1, keepdims=True) + NORM_EPS) * w_ref[...]
    o_ref[...] = x


def resid(x, y, gate, final_w=None, rows=512):
    b, t, d = x.shape
    tr = min(rows, t)
    bm = gate.shape[0]
    mod_map = (lambda i, j: (i, 0, 0)) if bm == b else (lambda i, j: (0, 0, 0))
    w = jnp.ones((d,), F32) if final_w is None else final_w
    kern = functools.partial(_resid_kernel, final=final_w is not None)
    blk = pl.BlockSpec((None, tr, d), lambda i, j: (i, j, 0))
    return pl.pallas_call(
        kern,
        grid=(b, t // tr),
        in_specs=[blk, blk, pl.BlockSpec((None, 1, d), mod_map), pl.BlockSpec((1, d), lambda i, j: (0, 0))],
        out_specs=blk,
        out_shape=jax.ShapeDtypeStruct((b, t, d), F32),
        compiler_params=_params(("parallel", "parallel")),
        name="resid",
    )(x, y, gate.reshape(bm, 1, d), w.reshape(1, d))


def _peer_ffn(x, fx, gate, wq, bq, subkeys, u, vt, final_w=None):
    b, t, d = x.shape
    n = b * t
    q = matmul(fx.reshape(n, d), wq, out_dtype=BF16, bias=bq)
    stab, etab, thr = peer_route(q, subkeys)
    y = peer_experts(fx.reshape(n, d), u, vt, stab, etab, thr)
    return resid(x, y.reshape(b, t, d), gate, final_w=final_w)


def _out_proj(a, w_out, x, gate):
    b, t, d = x.shape
    g = gate if gate.shape[0] == b else jnp.broadcast_to(gate, (b, d))
    y = matmul(a.reshape(b * t, a.shape[-1]), w_out, out_dtype=F32, res=x.reshape(b * t, d), gate=g,
               rows_per_gate=t, tm=min(512, t))
    return y.reshape(b, t, d)


def kernel(x, c, ctx, c_ctx, ada_w, ada_b, norm_mix_w, norm_ffn_w, ab_w_in, ab_w_out, dn_conv_w, dn_a_log,
           dn_dt_bias, dn_norm_w, diff_lambda, diff_norm_w, gqa_w_in, gqa_w_out, gqa_sink, peer_wq, peer_bq,
           peer_subkeys, peer_u, peer_v, final_norm_w):
    b, t, d = x.shape
    lctx = ctx.shape[1]
    depth = ada_w.shape[0]
    n = b * t
    nctx = b * lctx

    cond = jnp.zeros((16, d), F32).at[:b].set(c).at[b].set(c_ctx)
    mods = []
    for l in range(depth):
        m = matmul(cond, ada_w, out_dtype=F32, bias=ada_b[l], silu_in=True, tn_cap=1024, layer=l)
        mods.append(m.reshape(16, 6, d))
    mod_x = lambda l, i: mods[l][:b, i]
    mod_c = lambda l, i: mods[l][b:b + 1, i]

    w_in = ab_w_in[0]
    qkv_w = DN_HEADS * HEAD_DIM * 3
    z_w = DN_HEADS * HEAD_DIM
    small = 4 * DN_HEADS
    off = qkv_w + z_w
    w_dn = jnp.concatenate([w_in[:, :off], w_in[:, off:off + small],
                            jnp.zeros((d, LANES - small), F32)], axis=1).astype(BF16)
    off += small
    dq_w = DIFF_HEADS * HEAD_DIM
    w_dqk = w_in[:, off:off + 2 * dq_w].astype(BF16)
    w_dv = w_in[:, off + 2 * dq_w:].astype(BF16)
    par = jnp.zeros((8, LANES), F32)
    par = par.at[0, 2 * DN_HEADS:4 * DN_HEADS].set(dn_dt_bias[0].reshape(-1))
    par = par.at[1, 2 * DN_HEADS:4 * DN_HEADS].set(dn_a_log[0].reshape(-1))
    lam_init = 0.8 - 0.6 * math.exp(-0.3 * 0)

    hx = normmod(x, norm_mix_w[0], mod_x(0, 0), mod_x(0, 1))
    hc = normmod(ctx, norm_mix_w[0], mod_c(0, 0), mod_c(0, 1))
    hx2 = hx.reshape(n, d)
    hc2 = hc.reshape(nctx, d)

    p_x = matmul(hx2, w_dn, out_dtype=F32).reshape(b, t, -1)
    p_c = matmul(hc2, w_dn, out_dtype=F32).reshape(b, lctx, -1)
    qkv_c = dn_conv(p_c, dn_conv_w[0])
    qkv_x = dn_conv(p_x, dn_conv_w[0])
    s0 = jnp.zeros((b, 2, DN_HEADS, HEAD_DIM, HEAD_DIM), F32)
    of_c, ob_c, s_c = dn_scan(qkv_c, p_c, par, s0)
    of_x, ob_x, _ = dn_scan(qkv_x, p_x, par, s_c)
    dn_x = dn_out(of_x, ob_x, p_x, dn_norm_w[0])
    dn_c = dn_out(of_c, ob_c, p_c, dn_norm_w[0])

    qk_x = matmul(hx2, w_dqk, out_dtype=F32).reshape(b, t, -1)
    qk_c = matmul(hc2, w_dqk, out_dtype=F32).reshape(b, lctx, -1)
    v_x = matmul(hx2, w_dv, out_dtype=BF16).reshape(b, t, -1)
    v_c = matmul(hc2, w_dv, out_dtype=BF16).reshape(b, lctx, -1)
    cos_d, sin_d = rope_tables(t, DIFF_DQK, DIFF_DQK // 4)
    one_c = jnp.ones((lctx, LANES), F32)
    zero_c = jnp.zeros((lctx, LANES), F32)
    dscale = DIFF_DQK ** -0.5
    q1_x, q2_x = rope(qk_x, 0, dq_w, cos_d, sin_d, half=DIFF_DQK // 4, scale=dscale, split=True)
    k_x = rope(qk_x, 1, dq_w, cos_d, sin_d, half=DIFF_DQK // 4)
    q1_c, q2_c = rope(qk_c, 0, dq_w, one_c, zero_c, half=DIFF_DQK // 4, scale=dscale, split=True)
    k_c = rope(qk_c, 1, dq_w, one_c, zero_c, half=DIFF_DQK // 4)
    k_all = jnp.concatenate([k_x, k_c], axis=1)
    v_all = jnp.concatenate([v_x, v_c], axis=1)
    d_x = diff_attention(q1_x, q2_x, k_all, v_all, diff_lambda[0], diff_norm_w[0], lam_init)
    d_c = diff_attention(q1_c, q2_c, k_c, v_c, diff_lambda[0], diff_norm_w[0], lam_init)

    w_out = ab_w_out[0].astype(BF16)
    x = _out_proj(jnp.concatenate([dn_x, d_x], axis=-1), w_out, x, mod_x(0, 2))
    ctx = _out_proj(jnp.concatenate([dn_c, d_c], axis=-1), w_out, ctx, mod_c(0, 2))

    wq = peer_wq[0].astype(BF16)
    sub = peer_subkeys[0].astype(BF16)
    u_tab = peer_u[0].astype(BF16)
    vt_tab = peer_v[0].T.astype(BF16)
    fx = normmod(x, norm_ffn_w[0], mod_x(0, 3), mod_x(0, 4))
    x = _peer_ffn(x, fx, mod_x(0, 5), wq, peer_bq[0], sub, u_tab, vt_tab)
    fc = normmod(ctx, norm_ffn_w[0], mod_c(0, 3), mod_c(0, 4))
    ctx = _peer_ffn(ctx, fc, mod_c(0, 5), wq, peer_bq[0], sub, u_tab, vt_tab)

    gq_w = GQA_Q_HEADS * HEAD_DIM
    gkv_w = GQA_KV_HEADS * HEAD_DIM
    w_g = gqa_w_in[0]
    w_gqk = w_g[:, :gq_w + gkv_w].astype(BF16)
    w_gv = w_g[:, gq_w + gkv_w:].astype(BF16)
    w_gkv = w_g[:, gq_w:].astype(BF16)
    hx = normmod(x, norm_mix_w[1], mod_x(1, 0), mod_x(1, 1))
    hc = normmod(ctx, norm_mix_w[1], mod_c(1, 0), mod_c(1, 1))
    hx2 = hx.reshape(n, d)
    qk = matmul(hx2, w_gqk, out_dtype=F32, tn_cap=1280).reshape(b, t, -1)
    gv = matmul(hx2, w_gv, out_dtype=BF16).reshape(b, t, -1)
    kvx = matmul(hc.reshape(nctx, d), w_gkv, out_dtype=BF16).reshape(b, lctx, -1)
    cos_g, sin_g = rope_tables(t, HEAD_DIM, HEAD_DIM // 4)
    gq = rope(qk, 0, gq_w, cos_g, sin_g, half=HEAD_DIM // 4, scale=HEAD_DIM ** -0.5)
    gk = rope(qk, gq_w // gkv_w, gkv_w, cos_g, sin_g, half=HEAD_DIM // 4)
    att = window_attention(gq, gk, gv, kvx, gqa_sink[0])
    x = _out_proj(att, gqa_w_out[0].astype(BF16), x, mod_x(1, 2))

    wq = peer_wq[1].astype(BF16)
    sub = peer_subkeys[1].astype(BF16)
    u_tab = peer_u[1].astype(BF16)
    vt_tab = peer_v[1].T.astype(BF16)
    fx = normmod(x, norm_ffn_w[1], mod_x(1, 3), mod_x(1, 4))
    return _peer_ffn(x, fx, mod_x(1, 5), wq, peer_bq[1], sub, u_tab, vt_tab, final_w=final_norm_w)
```

```python
import functools
import math

import jax
import jax.numpy as jnp
import numpy as np
from jax import lax
from jax.experimental import pallas as pl
from jax.experimental.pallas import tpu as pltpu

F32 = jnp.float32
BF16 = jnp.bfloat16

LANES = 128
HEAD_DIM = 128
NORM_EPS = 1e-6
ROPE_THETA = 10000.0
GRID_W = 64
DN_HEADS = 8
DN_CHUNK = 64
DN_CONV = 5
DIFF_HEADS = 8
DIFF_DQK = 64
GQA_Q_HEADS = 16
GQA_KV_HEADS = 4
GQA_GROUP = GQA_Q_HEADS // GQA_KV_HEADS
WINDOW = 128
PEER_HEADS = 8
PEER_NKEYS = 128
PEER_TOPK = 16
NEG_BIG = -1e30
VMEM_LIMIT = 56 * 1024 * 1024


def _params(sem):
    return pltpu.CompilerParams(dimension_semantics=sem, vmem_limit_bytes=VMEM_LIMIT)


def _dot(a, b):
    return jnp.dot(a.astype(BF16), b.astype(BF16), preferred_element_type=F32)


def _dot_nt(a, b):
    return lax.dot_general(a.astype(BF16), b.astype(BF16), (((1,), (1,)), ((), ())),
                           preferred_element_type=F32)


def _sigmoid(x):
    return 1.0 / (1.0 + jnp.exp(-x))


def _normmod_kernel(x_ref, w_ref, shift_ref, scale_ref, o_ref):
    x = x_ref[...]
    y = x * lax.rsqrt(jnp.mean(x * x, axis=-1, keepdims=True) + NORM_EPS)
    y = y * w_ref[...]
    o_ref[...] = (y * (1.0 + scale_ref[...]) + shift_ref[...]).astype(o_ref.dtype)


def normmod(x, w, shift, scale, rows=512):
    b, t, d = x.shape
    tr = min(rows, t)
    bm = shift.shape[0]
    mod_map = (lambda i, j: (i, 0, 0)) if bm == b else (lambda i, j: (0, 0, 0))
    return pl.pallas_call(
        _normmod_kernel,
        grid=(b, t // tr),
        in_specs=[pl.BlockSpec((None, tr, d), lambda i, j: (i, j, 0)),
                  pl.BlockSpec((1, d), lambda i, j: (0, 0)),
                  pl.BlockSpec((None, 1, d), mod_map),
                  pl.BlockSpec((None, 1, d), mod_map)],
        out_specs=pl.BlockSpec((None, tr, d), lambda i, j: (i, j, 0)),
        out_shape=jax.ShapeDtypeStruct((b, t, d), BF16),
        compiler_params=_params(("parallel", "parallel")),
        name="normmod",
    )(x, w.reshape(1, d), shift.reshape(bm, 1, d), scale.reshape(bm, 1, d))


def _mm_kernel(*refs, silu_in, has_bias, has_res):
    a_ref, w_ref = refs[0], refs[1]
    pos = 2
    a = a_ref[...]
    if silu_in:
        a = a * _sigmoid(a)
    acc = _dot(a, w_ref[...])
    if has_bias:
        acc = acc + refs[pos][...]
        pos += 1
    if has_res:
        acc = refs[pos][...] + refs[pos + 1][...] * acc
        pos += 2
    o_ref = refs[pos]
    o_ref[...] = acc.astype(o_ref.dtype)


def _pick_tn(n, cap):
    best = LANES
    for cand in range(LANES, min(n, cap) + 1, LANES):
        if n % cand == 0:
            best = cand
    return best


def matmul(a, w, *, out_dtype, bias=None, res=None, gate=None, rows_per_gate=None,
           silu_in=False, tm=512, tn_cap=2048, layer=0):
    m, k = a.shape
    n = w.shape[-1]
    tm = min(tm, m)
    tn = _pick_tn(n, tn_cap)
    if w.ndim == 3:
        w_spec = pl.BlockSpec((None, k, tn), lambda j, i: (layer, 0, j))
    else:
        w_spec = pl.BlockSpec((k, tn), lambda j, i: (0, j))
    in_specs = [pl.BlockSpec((tm, k), lambda j, i: (i, 0)), w_spec]
    args = [a, w]
    if bias is not None:
        in_specs.append(pl.BlockSpec((1, tn), lambda j, i: (0, j)))
        args.append(bias.reshape(1, n).astype(F32))
    if res is not None:
        blocks_per_gate = rows_per_gate // tm
        in_specs.append(pl.BlockSpec((tm, tn), lambda j, i: (i, j)))
        in_specs.append(pl.BlockSpec((None, 1, tn), lambda j, i: (i // blocks_per_gate, 0, j)))
        args += [res, gate.reshape(gate.shape[0], 1, n)]
    kern = functools.partial(_mm_kernel, silu_in=silu_in, has_bias=bias is not None,
                             has_res=res is not None)
    return pl.pallas_call(
        kern,
        grid=(n // tn, m // tm),
        in_specs=in_specs,
        out_specs=pl.BlockSpec((tm, tn), lambda j, i: (i, j)),
        out_shape=jax.ShapeDtypeStruct((m, n), out_dtype),
        compiler_params=_params(("parallel", "parallel")),
        name="matmul",
    )(*args)


def _conv_kernel(prev_ref, cur_ref, next_ref, w_ref, o_ref, *, tt, nheads_blk):
    i = pl.program_id(1)
    grp = pl.program_id(2)
    last = pl.num_programs(1) - 1
    prev = prev_ref[...] * (i > 0).astype(F32)
    nxt = next_ref[...] * (i < last).astype(F32)
    xx = jnp.concatenate([prev, cur_ref[...], nxt], axis=0)
    rows = tt + 16
    pad = (DN_CONV - 1) // 2
    y = None
    for k in range(DN_CONV):
        shift = (pad - k) % rows
        z = xx if shift == 0 else pltpu.roll(xx, shift, 0)
        term = z[8:8 + tt] * w_ref[k:k + 1, :]
        y = term if y is None else y + term
    y = y * _sigmoid(y)
    qscale = jnp.where(grp == 0, HEAD_DIM ** -0.5, 1.0).astype(F32)
    for h in range(nheads_blk):
        yh = y[:, h * HEAD_DIM:(h + 1) * HEAD_DIM]
        ss = jnp.sum(yh * yh, axis=-1, keepdims=True)
        fac = jnp.where(grp == 2, 1.0, lax.rsqrt(ss + NORM_EPS) * qscale)
        o_ref[:, h * HEAD_DIM:(h + 1) * HEAD_DIM] = yh * fac


def dn_conv(p, conv_w, tt=256):
    b, t, _ = p.shape
    tt = min(tt, t)
    cb = DN_HEADS * HEAD_DIM
    n8 = t // 8
    kern = functools.partial(_conv_kernel, tt=tt, nheads_blk=DN_HEADS)
    return pl.pallas_call(
        kern,
        grid=(b, t // tt, 3),
        in_specs=[pl.BlockSpec((None, 8, cb), lambda bi, i, g: (bi, jnp.maximum(i * (tt // 8) - 1, 0), g)),
                  pl.BlockSpec((None, tt, cb), lambda bi, i, g: (bi, i, g)),
                  pl.BlockSpec((None, 8, cb), lambda bi, i, g: (bi, jnp.minimum((i + 1) * (tt // 8), n8 - 1), g)),
                  pl.BlockSpec((DN_CONV, cb), lambda bi, i, g: (0, g))],
        out_specs=pl.BlockSpec((None, tt, cb), lambda bi, i, g: (bi, i, g)),
        out_shape=jax.ShapeDtypeStruct((b, t, 3 * cb), F32),
        compiler_params=_params(("parallel", "parallel", "parallel")),
        name="dn_conv",
    )(p, p, p, conv_w)


def _split3(x):
    hi = x.astype(BF16)
    r1 = x - hi.astype(F32)
    mid = r1.astype(BF16)
    lo = (r1 - mid.astype(F32)).astype(BF16)
    return hi, mid, lo


def _dn_chains(direction, q_ref, k_ref, v_ref, p_ref, par_ref, o_ref, s_scr):
    c = DN_CHUNK
    ii = lax.broadcasted_iota(jnp.int32, (c, c), 0)
    jj = lax.broadcasted_iota(jnp.int32, (c, c), 1)
    incl = (ii >= jj) if direction == 0 else (ii <= jj)
    strict = (ii > jj) if direction == 0 else (ii < jj)

    raw = p_ref[...]
    beta_all = _sigmoid(raw)
    xa = raw + par_ref[0:1, :]
    softplus = jnp.maximum(xa, 0.0) + jnp.log1p(jnp.exp(-jnp.abs(xa)))
    g_all = -jnp.exp(par_ref[1:2, :]) * softplus
    inclb = incl.astype(BF16)
    hi, mid, lo = _split3(g_all)
    gc_all = (jnp.dot(inclb, hi, preferred_element_type=F32)
              + jnp.dot(inclb, mid, preferred_element_type=F32)
              + jnp.dot(inclb, lo, preferred_element_type=F32))
    gtot_all = jnp.sum(g_all, axis=0, keepdims=True)
    gc_t = gc_all.T

    chains = []
    for h in range(DN_HEADS):
        ib = direction * DN_HEADS + h
        ig = 2 * DN_HEADS + ib
        sl = slice(h * HEAD_DIM, (h + 1) * HEAD_DIM)
        gc_c = gc_all[:, ig:ig + 1]
        chains.append(dict(
            incl=incl, strict=strict, beta=beta_all[:, ib:ib + 1], gc_c=gc_c,
            decay=jnp.exp(jnp.where(incl, gc_c - gc_t[ig:ig + 1, :], -jnp.inf)),
            gtot=gtot_all[:, ig:ig + 1], q=q_ref[:, sl], k=k_ref[:, sl], v=v_ref[:, sl],
            o_ref=o_ref, sl=sl, state_idx=(direction, h)))
    return chains


def _dn_step(chains, s_scr):
    c = DN_CHUNK
    ii = lax.broadcasted_iota(jnp.int32, (c, c), 0)
    jj = lax.broadcasted_iota(jnp.int32, (c, c), 1)
    eye = (ii == jj).astype(F32)
    for ch in chains:
        ch["kb"] = ch["k"] * ch["beta"]
        ch["kq"] = _dot_nt(jnp.concatenate([ch["kb"], ch["q"]], axis=0), ch["k"])
    for ch in chains:
        a = jnp.where(ch["strict"], ch["kq"][:c] * ch["decay"], 0.0)
        ch["qk"] = jnp.where(ch["incl"], ch["kq"][c:] * ch["decay"], 0.0)
        ch["inv"] = eye - a
        ch["a_pow"] = a
    for _ in range(int(math.log2(c)) - 1):
        for ch in chains:
            ch["a_pow"] = _dot(ch["a_pow"], ch["a_pow"])
        for ch in chains:
            ch["inv"] = _dot(ch["inv"], eye + ch["a_pow"])
    for ch in chains:
        egc = jnp.exp(ch["gc_c"])
        ch["uw"] = _dot(ch["inv"], jnp.concatenate([ch["v"] * ch["beta"], ch["kb"] * egc], axis=1))
        ch["q_dec"] = ch["q"] * egc
    for ch in chains:
        ch["state"] = s_scr[ch["state_idx"]]
        w = ch["uw"][:, HEAD_DIM:]
        ch["ws"] = _dot(jnp.concatenate([w, ch["q_dec"]], axis=0), ch["state"])
    for ch in chains:
        ch["v_new"] = ch["uw"][:, :HEAD_DIM] - ch["ws"][:c]
        ch["o_ref"][:, ch["sl"]] = ch["ws"][c:] + _dot(ch["qk"], ch["v_new"])
    for ch in chains:
        k_dec = ch["k"] * jnp.exp(ch["gtot"] - ch["gc_c"])
        s_scr[ch["state_idx"]] = ch["state"] * jnp.exp(ch["gtot"]) + _dot(k_dec.T, ch["v_new"])


def _dn_scan_kernel(qf_ref, kf_ref, vf_ref, pf_ref, qb_ref, kb_ref, vb_ref, pb_ref, par_ref, s0_ref,
                    of_ref, ob_ref, sfin_ref, s_scr):
    n = pl.program_id(1)

    @pl.when(n == 0)
    def _():
        s_scr[...] = s0_ref[...]

    chains = (_dn_chains(0, qf_ref, kf_ref, vf_ref, pf_ref, par_ref, of_ref, s_scr)
              + _dn_chains(1, qb_ref, kb_ref, vb_ref, pb_ref, par_ref, ob_ref, s_scr))
    _dn_step(chains, s_scr)

    @pl.when(n == pl.num_programs(1) - 1)
    def _():
        sfin_ref[...] = s_scr[...]


def dn_scan(qkv, p, par, s0):
    b, t, _ = qkv.shape
    c = DN_CHUNK
    nc = t // c
    w = DN_HEADS * HEAD_DIM
    fwd = lambda col: (lambda bi, n: (bi, n, col))
    bwd = lambda col: (lambda bi, n: (bi, nc - 1 - n, col))
    blk = lambda f: pl.BlockSpec((None, c, w), f)
    small = lambda f: pl.BlockSpec((None, c, LANES), f)
    state_spec = pl.BlockSpec((None, 2, DN_HEADS, HEAD_DIM, HEAD_DIM), lambda bi, n: (bi, 0, 0, 0, 0))
    o_shape = jax.ShapeDtypeStruct((b, t, w), F32)
    return pl.pallas_call(
        _dn_scan_kernel,
        grid=(b, nc),
        in_specs=[blk(fwd(0)), blk(fwd(1)), blk(fwd(2)), small(fwd(32)),
                  blk(bwd(0)), blk(bwd(1)), blk(bwd(2)), small(bwd(32)),
                  pl.BlockSpec((8, LANES), lambda bi, n: (0, 0)), state_spec],
        out_specs=[blk(fwd(0)), blk(bwd(0)), state_spec],
        out_shape=[o_shape, o_shape, jax.ShapeDtypeStruct(s0.shape, F32)],
        scratch_shapes=[pltpu.VMEM((2, DN_HEADS, HEAD_DIM, HEAD_DIM), F32)],
        compiler_params=_params(("parallel", "arbitrary")),
        name="dn_scan",
    )(qkv, qkv, qkv, p, qkv, qkv, qkv, p, par, s0)


def _dn_out_kernel(of_ref, ob_ref, z_ref, w_ref, y_ref):
    o = of_ref[...] + ob_ref[...]
    z = z_ref[...]
    for h in range(DN_HEADS):
        sl = slice(h * HEAD_DIM, (h + 1) * HEAD_DIM)
        oh = o[:, sl]
        yh = oh * lax.rsqrt(jnp.mean(oh * oh, axis=-1, keepdims=True) + NORM_EPS) * w_ref[...]
        zh = z[:, sl]
        y_ref[:, sl] = (yh * (zh * _sigmoid(zh))).astype(y_ref.dtype)


def dn_out(o_f, o_b, p, norm_w, tr=256):
    b, t, w = o_f.shape
    tr = min(tr, t)
    return pl.pallas_call(
        _dn_out_kernel,
        grid=(b, t // tr),
        in_specs=[pl.BlockSpec((None, tr, w), lambda bi, i: (bi, i, 0)),
                  pl.BlockSpec((None, tr, w), lambda bi, i: (bi, i, 0)),
                  pl.BlockSpec((None, tr, w), lambda bi, i: (bi, i, 3)),
                  pl.BlockSpec((1, HEAD_DIM), lambda bi, i: (0, 0))],
        out_specs=pl.BlockSpec((None, tr, w), lambda bi, i: (bi, i, 0)),
        out_shape=jax.ShapeDtypeStruct((b, t, w), BF16),
        compiler_params=_params(("parallel", "parallel")),
        name="dn_out",
    )(o_f, o_b, p, norm_w.reshape(1, HEAD_DIM))


def _rope_kernel(x_ref, cos_ref, sin_ref, *o_refs, half, scale, nheads, split):
    lane = lax.broadcasted_iota(jnp.int32, (1, LANES), 1)
    first = (lane % (2 * half)) < half
    cos = cos_ref[...]
    sin = sin_ref[...]
    for h in range(nheads):
        sl = slice(h * LANES, (h + 1) * LANES)
        x = x_ref[:, sl]
        partner = jnp.where(first, pltpu.roll(x, LANES - half, 1), pltpu.roll(x, half, 1))
        y = (x * cos + partner * sin) * scale
        if split:
            o_refs[0][:, sl] = jnp.where(lane < LANES // 2, y, 0.0).astype(BF16)
            o_refs[1][:, sl] = jnp.where(lane >= LANES // 2, y, 0.0).astype(BF16)
        else:
            o_refs[0][:, sl] = y.astype(BF16)


def rope(x, col_block, width, cos, sin, *, half, scale=1.0, split=False, tr=512):
    b, t, _ = x.shape
    tr = min(tr, t)
    nheads = width // LANES
    nout = 2 if split else 1
    kern = functools.partial(_rope_kernel, half=half, scale=scale, nheads=nheads, split=split)
    out = pl.pallas_call(
        kern,
        grid=(b, t // tr),
        in_specs=[pl.BlockSpec((None, tr, width), lambda bi, i: (bi, i, col_block)),
                  pl.BlockSpec((tr, LANES), lambda bi, i: (i, 0)),
                  pl.BlockSpec((tr, LANES), lambda bi, i: (i, 0))],
        out_specs=[pl.BlockSpec((None, tr, width), lambda bi, i: (bi, i, 0))] * nout,
        out_shape=[jax.ShapeDtypeStruct((b, t, width), BF16)] * nout,
        compiler_params=_params(("parallel", "parallel")),
        name="rope",
    )(x, cos, sin)
    return out if split else out[0]


def rope_tables(t, block, half):
    pos = jnp.arange(t, dtype=jnp.int32)
    rows = (pos // GRID_W).astype(F32)
    cols = (pos % GRID_W).astype(F32)
    lane = np.arange(LANES)
    j = lane % block
    use_col = (j // (2 * half)) == 1
    i = j % (2 * half)
    inv = (ROPE_THETA ** (-jnp.arange(half, dtype=F32) / half))[i % half]
    p = jnp.where(jnp.asarray(use_col)[None, :], cols[:, None], rows[:, None])
    ang = p * inv[None, :]
    sign = jnp.asarray(np.where(i < half, -1.0, 1.0).astype(np.float32))[None, :]
    return jnp.cos(ang), jnp.sin(ang) * sign


def _diff_kernel(q1_ref, q2_ref, k_ref, vt_ref, lam_ref, nw_ref, o_ref, acc1, acc2, *, lam_init, tk, nkv):
    tq = q1_ref.shape[0]
    acc1[...] = jnp.zeros_like(acc1)
    acc2[...] = jnp.zeros_like(acc2)

    def body(c, carry):
        m1, l1, m2, l2 = carry
        kc = k_ref[pl.ds(pl.multiple_of(c * tk, tk), tk), :]
        vtc = vt_ref[c]

        def update(s, m, l, acc):
            m_new = jnp.maximum(m, jnp.max(s, axis=0, keepdims=True))
            alpha = jnp.exp2(m - m_new)
            p = jnp.exp2(s - m_new)
            l_new = l * alpha + jnp.sum(p, axis=0, keepdims=True)
            acc[...] = acc[...] * alpha + _dot(vtc, p)
            return m_new, l_new

        s1 = _dot_nt(kc, q1_ref[...])
        s2 = _dot_nt(kc, q2_ref[...])
        m1, l1 = update(s1, m1, l1, acc1)
        m2, l2 = update(s2, m2, l2, acc2)
        return m1, l1, m2, l2

    neg = jnp.full((1, tq), -jnp.inf, F32)
    zero = jnp.zeros((1, tq), F32)
    m1, l1, m2, l2 = lax.fori_loop(0, nkv, body, (neg, zero, neg, zero))

    lv = lam_ref[...]
    s01 = jnp.sum(lv[0:1] * lv[1:2], axis=-1, keepdims=True)
    s23 = jnp.sum(lv[2:3] * lv[3:4], axis=-1, keepdims=True)
    lam = jnp.exp(s01) - jnp.exp(s23) + lam_init
    o = acc1[...] / l1 - lam * (acc2[...] / l2)
    y = o * lax.rsqrt(jnp.mean(o * o, axis=0, keepdims=True) + NORM_EPS)
    o_ref[...] = (y.T * nw_ref[...] * (1.0 - lam_init)).astype(o_ref.dtype)


def diff_attention(q1, q2, k, v, lam_vec, norm_w, lam_init, tq=1024, tk=768):
    b, tqa, w = q1.shape
    tka = k.shape[1]
    tq = min(tq, tqa)
    tk = min(tk, tka)
    nkv = tka // tk
    nh = w // HEAD_DIM
    vt = v.reshape(b, nkv, tk, nh, HEAD_DIM).transpose(0, 3, 1, 4, 2)
    kern = functools.partial(_diff_kernel, lam_init=lam_init, tk=tk, nkv=nkv)
    return pl.pallas_call(
        kern,
        grid=(b, nh, tqa // tq),
        in_specs=[pl.BlockSpec((None, tq, HEAD_DIM), lambda bi, h, i: (bi, i, h)),
                  pl.BlockSpec((None, tq, HEAD_DIM), lambda bi, h, i: (bi, i, h)),
                  pl.BlockSpec((None, tka, HEAD_DIM), lambda bi, h, i: (bi, 0, h)),
                  pl.BlockSpec((None, None, nkv, HEAD_DIM, tk), lambda bi, h, i: (bi, h, 0, 0, 0)),
                  pl.BlockSpec((4, DIFF_DQK), lambda bi, h, i: (0, 0)),
                  pl.BlockSpec((1, HEAD_DIM), lambda bi, h, i: (0, 0))],
        out_specs=pl.BlockSpec((None, tq, HEAD_DIM), lambda bi, h, i: (bi, i, h)),
        out_shape=jax.ShapeDtypeStruct((b, tqa, w), BF16),
        scratch_shapes=[pltpu.VMEM((HEAD_DIM, tq), F32)] * 2,
        compiler_params=_params(("parallel", "parallel", "parallel")),
        name="diff_attention",
    )(q1, q2, k, vt, lam_vec, norm_w.reshape(1, HEAD_DIM))


def _win_kernel(sink_ref, q_ref, kp_ref, kc_ref, kn_ref, vp_ref, vc_ref, vn_ref, kx_ref, vx_ref, o_ref,
                *, t_total):
    kvh = pl.program_id(1)
    n = pl.program_id(2)
    wb = WINDOW
    tq = q_ref.shape[0]
    keys = jnp.concatenate([kp_ref[...], kc_ref[...], kn_ref[...], kx_ref[...]], axis=0)
    vals = jnp.concatenate([vp_ref[...], vc_ref[...], vn_ref[...], vx_ref[...]], axis=0)
    nk = keys.shape[0]
    nlocal = tq + 2 * wb
    i = lax.broadcasted_iota(jnp.int32, (tq, nk), 0)
    j = lax.broadcasted_iota(jnp.int32, (tq, nk), 1)
    kpos = n * tq - wb + j
    valid = (jnp.abs(j - wb - i) <= WINDOW) & (kpos >= 0) & (kpos < t_total)
    valid = valid | (j >= nlocal)
    for g in range(GQA_GROUP):
        sl = slice(g * HEAD_DIM, (g + 1) * HEAD_DIM)
        sink = sink_ref[kvh * GQA_GROUP + g]
        s = jnp.where(valid, _dot_nt(q_ref[:, sl], keys), -jnp.inf)
        m = jnp.maximum(jnp.max(s, axis=-1, keepdims=True), sink)
        p = jnp.exp(s - m)
        denom = jnp.sum(p, axis=-1, keepdims=True) + jnp.exp(sink - m)
        o_ref[:, sl] = (_dot(p, vals) / denom).astype(o_ref.dtype)


def window_attention(q, k, v, kvx, sink, tq=512):
    b, t, _ = q.shape
    lctx = kvx.shape[1]
    wb = WINDOW
    tq = min(tq, t)
    per = tq // wb
    nb = t // wb
    qw = GQA_GROUP * HEAD_DIM
    kern = functools.partial(_win_kernel, t_total=t)
    prev = lambda bi, h, n: (bi, jnp.maximum(n * per - 1, 0), h)
    cur = lambda bi, h, n: (bi, n, h)
    nxt = lambda bi, h, n: (bi, jnp.minimum((n + 1) * per, nb - 1), h)
    edge_spec = lambda f: pl.BlockSpec((None, wb, HEAD_DIM), f)
    main_spec = pl.BlockSpec((None, tq, HEAD_DIM), cur)
    return pl.pallas_call(
        kern,
        grid=(b, GQA_KV_HEADS, t // tq),
        in_specs=[pl.BlockSpec(memory_space=pltpu.SMEM),
                  pl.BlockSpec((None, tq, qw), cur),
                  edge_spec(prev), main_spec, edge_spec(nxt),
                  edge_spec(prev), main_spec, edge_spec(nxt),
                  pl.BlockSpec((None, lctx, HEAD_DIM), lambda bi, h, n: (bi, 0, h)),
                  pl.BlockSpec((None, lctx, HEAD_DIM), lambda bi, h, n: (bi, 0, GQA_KV_HEADS + h))],
        out_specs=pl.BlockSpec((None, tq, qw), cur),
        out_shape=jax.ShapeDtypeStruct(q.shape, BF16),
        compiler_params=_params(("parallel", "parallel", "parallel")),
        name="window_attention",
    )(sink, q, k, k, k, v, v, v, kvx, kvx)


RANK_NONE = 255.0


def _top_values(s, count, want_rank=False):
    vals = []
    work = s
    rank = jnp.full(s.shape, RANK_NONE, F32) if want_rank else None
    for r in range(count):
        m = jnp.max(work, axis=0, keepdims=True)
        vals.append(m)
        hit = work == m
        if want_rank:
            rank = jnp.where(hit, float(r), rank)
        work = jnp.where(hit, -jnp.inf, work)
    return (vals, rank) if want_rank else vals


def _route_kernel(q_ref, sub_ref, cut_ref, e1_ref, r2_ref, e2_ref):
    kk = PEER_TOPK
    tb = q_ref.shape[0]
    row = lax.broadcasted_iota(jnp.int32, (kk, tb), 0)
    for h in range(PEER_HEADS):
        scores, tops, ranks = [], [], []
        for p in range(2):
            c0 = (h * 2 + p) * PEER_NKEYS
            s = _dot_nt(sub_ref[h, p], q_ref[:, c0:c0 + PEER_NKEYS])
            vals, rank = _top_values(s, kk, want_rank=True)
            scores.append(s)
            tops.append(vals)
            ranks.append(rank)
        v2 = jnp.zeros((kk, tb), F32)
        for i in range(kk):
            v2 = jnp.where(row == i, tops[1][i], v2)
        cand = jnp.concatenate([tops[0][i] + (v2 if i == 0 else v2[:8]) for i in range(kk)], axis=0)
        thr = _top_values(cand, kk)[kk - 1]
        top = tops[0][0] + tops[1][0]
        z = jnp.sum(jnp.where(cand >= thr, jnp.exp(cand - top), 0.0), axis=0, keepdims=True)
        cut = jnp.zeros_like(scores[0])
        for i in range(kk):
            count = jnp.sum(((tops[0][i] + v2) >= thr).astype(F32), axis=0, keepdims=True)
            cut = jnp.where(ranks[0] == float(i), count, cut)
        in1 = ranks[0] < float(kk)
        in2 = ranks[1] < float(kk)
        cut_ref[h] = cut
        e1_ref[h] = jnp.where(in1, jnp.exp(scores[0] - tops[0][0]), 0.0)
        r2_ref[h] = ranks[1].astype(r2_ref.dtype)
        e2_ref[h] = jnp.where(in2, jnp.exp(scores[1] - tops[1][0]) / z, 0.0).astype(e2_ref.dtype)


def peer_route(q, subkeys, tb=256):
    n = q.shape[0]
    tb = min(tb, n)
    hh = PEER_HEADS
    tab32 = jax.ShapeDtypeStruct((hh, PEER_NKEYS, n), F32)
    tab16 = jax.ShapeDtypeStruct((hh, PEER_NKEYS, n), BF16)
    tab_spec = pl.BlockSpec((hh, PEER_NKEYS, tb), lambda i: (0, 0, i))
    return pl.pallas_call(
        _route_kernel,
        grid=(n // tb,),
        in_specs=[pl.BlockSpec((tb, q.shape[1]), lambda i: (i, 0)),
                  pl.BlockSpec(subkeys.shape, lambda i: (0, 0, 0, 0))],
        out_specs=[tab_spec] * 4,
        out_shape=[tab32, tab32, tab16, tab16],
        compiler_params=_params(("parallel",)),
        name="peer_route",
    )(q, subkeys)


def _gelu(x):
    return 0.5 * x * (1.0 + lax.erf(x * (2.0 ** -0.5)))


def _peer_kernel(x_ref, u_ref, vt_ref, cut_ref, e1_ref, r2_ref, e2_ref, o_ref, acc, *, rows, rows_per_part):
    et = pl.program_id(1)

    @pl.when(et == 0)
    def _():
        acc[...] = jnp.zeros_like(acc)

    ht = _dot_nt(u_ref[...], x_ref[...])
    nk = PEER_NKEYS
    gdt = r2_ref.dtype
    total = None
    for part in range(rows // rows_per_part):
        a_rows = []
        for r in range(part * rows_per_part, (part + 1) * rows_per_part):
            gate = None
            for h in range(PEER_HEADS):
                cut = cut_ref[h, r:r + 1, :].astype(gdt)
                e1 = e1_ref[h, r:r + 1, :].astype(gdt)
                g = jnp.where(r2_ref[h] < cut, e1 * e2_ref[h], jnp.zeros((), gdt))
                gate = g if gate is None else gate + g
            a_rows.append((_gelu(ht[r * nk:(r + 1) * nk, :]) * gate.astype(F32)).astype(BF16))
        k0 = part * rows_per_part * nk
        prod = jnp.dot(vt_ref[:, k0:k0 + rows_per_part * nk], jnp.concatenate(a_rows, axis=0),
                       preferred_element_type=F32)
        total = prod if total is None else total + prod
    acc[...] += total

    @pl.when(et == pl.num_programs(1) - 1)
    def _():
        o_ref[...] = acc[...].T


def peer_experts(x, u, vt, cut, e1, r2, e2, tb=512, rows=8, rows_per_part=2):
    n, d = x.shape
    e = u.shape[0]
    tb = min(tb, n)
    te = rows * PEER_NKEYS
    hh = PEER_HEADS
    kern = functools.partial(_peer_kernel, rows=rows, rows_per_part=rows_per_part)
    row_spec = pl.BlockSpec((hh, rows, tb), lambda i, j: (0, j, i))
    full_spec = pl.BlockSpec((hh, PEER_NKEYS, tb), lambda i, j: (0, 0, i))
    return pl.pallas_call(
        kern,
        grid=(n // tb, e // te),
        in_specs=[pl.BlockSpec((tb, d), lambda i, j: (i, 0)),
                  pl.BlockSpec((te, d), lambda i, j: (j, 0)),
                  pl.BlockSpec((d, te), lambda i, j: (0, j)),
                  row_spec, row_spec, full_spec, full_spec],
        out_specs=pl.BlockSpec((tb, d), lambda i, j: (i, 0)),
        out_shape=jax.ShapeDtypeStruct((n, d), F32),
        scratch_shapes=[pltpu.VMEM((d, tb), F32)],
        compiler_params=_params(("parallel", "arbitrary")),
        name="peer_experts",
    )(x, u, vt, cut, e1, r2, e2)


def _resid_kernel(x_ref, y_ref, g_ref, w_ref, o_ref, *, final):
    x = x_ref[...] + g_ref[...] * y_ref[...]
    if final:
        x = x * lax.rsqrt(jnp.mean(x * x, axis=-1, keepdims=True) + NORM_EPS) * w_ref[...]
    o_ref[...] = x


def resid(x, y, gate, final_w=None, rows=512):
    b, t, d = x.shape
    tr = min(rows, t)
    bm = gate.shape[0]
    mod_map = (lambda i, j: (i, 0, 0)) if bm == b else (lambda i, j: (0, 0, 0))
    w = jnp.ones((d,), F32) if final_w is None else final_w
    kern = functools.partial(_resid_kernel, final=final_w is not None)
    blk = pl.BlockSpec((None, tr, d), lambda i, j: (i, j, 0))
    return pl.pallas_call(
        kern,
        grid=(b, t // tr),
        in_specs=[blk, blk, pl.BlockSpec((None, 1, d), mod_map), pl.BlockSpec((1, d), lambda i, j: (0, 0))],
        out_specs=blk,
        out_shape=jax.ShapeDtypeStruct((b, t, d), F32),
        compiler_params=_params(("parallel", "parallel")),
        name="resid",
    )(x, y, gate.reshape(bm, 1, d), w.reshape(1, d))


def _peer_ffn(x, fx, gate, wq, bq, subkeys, u, vt, final_w=None):
    b, t, d = x.shape
    n = b * t
    q = matmul(fx.reshape(n, d), wq, out_dtype=BF16, bias=bq)
    cut, e1, r2, e2 = peer_route(q, subkeys)
    y = peer_experts(fx.reshape(n, d), u, vt, cut, e1, r2, e2)
    return resid(x, y.reshape(b, t, d), gate, final_w=final_w)


def _out_proj(a, w_out, x, gate):
    b, t, d = x.shape
    g = gate if gate.shape[0] == b else jnp.broadcast_to(gate, (b, d))
    y = matmul(a.reshape(b * t, a.shape[-1]), w_out, out_dtype=F32, res=x.reshape(b * t, d), gate=g,
               rows_per_gate=t, tm=min(512, t))
    return y.reshape(b, t, d)


def kernel(x, c, ctx, c_ctx, ada_w, ada_b, norm_mix_w, norm_ffn_w, ab_w_in, ab_w_out, dn_conv_w, dn_a_log,
           dn_dt_bias, dn_norm_w, diff_lambda, diff_norm_w, gqa_w_in, gqa_w_out, gqa_sink, peer_wq, peer_bq,
           peer_subkeys, peer_u, peer_v, final_norm_w):
    b, t, d = x.shape
    lctx = ctx.shape[1]
    depth = ada_w.shape[0]
    n = b * t
    nctx = b * lctx

    cond = jnp.zeros((16, d), F32).at[:b].set(c).at[b].set(c_ctx)
    mods = []
    for l in range(depth):
        m = matmul(cond, ada_w, out_dtype=F32, bias=ada_b[l], silu_in=True, tn_cap=1024, layer=l)
        mods.append(m.reshape(16, 6, d))
    mod_x = lambda l, i: mods[l][:b, i]
    mod_c = lambda l, i: mods[l][b:b + 1, i]

    w_in = ab_w_in[0]
    qkv_w = DN_HEADS * HEAD_DIM * 3
    z_w = DN_HEADS * HEAD_DIM
    small = 4 * DN_HEADS
    off = qkv_w + z_w
    w_dn = jnp.concatenate([w_in[:, :off], w_in[:, off:off + small],
                            jnp.zeros((d, LANES - small), F32)], axis=1).astype(BF16)
    off += small
    dq_w = DIFF_HEADS * HEAD_DIM
    w_dqk = w_in[:, off:off + 2 * dq_w].astype(BF16)
    w_dv = w_in[:, off + 2 * dq_w:].astype(BF16)
    par = jnp.zeros((8, LANES), F32)
    par = par.at[0, 2 * DN_HEADS:4 * DN_HEADS].set(dn_dt_bias[0].reshape(-1))
    par = par.at[1, 2 * DN_HEADS:4 * DN_HEADS].set(dn_a_log[0].reshape(-1))
    lam_init = 0.8 - 0.6 * math.exp(-0.3 * 0)

    hx = normmod(x, norm_mix_w[0], mod_x(0, 0), mod_x(0, 1))
    hc = normmod(ctx, norm_mix_w[0], mod_c(0, 0), mod_c(0, 1))
    hx2 = hx.reshape(n, d)
    hc2 = hc.reshape(nctx, d)

    p_x = matmul(hx2, w_dn, out_dtype=F32).reshape(b, t, -1)
    p_c = matmul(hc2, w_dn, out_dtype=F32).reshape(b, lctx, -1)
    qkv_c = dn_conv(p_c, dn_conv_w[0])
    qkv_x = dn_conv(p_x, dn_conv_w[0])
    s0 = jnp.zeros((b, 2, DN_HEADS, HEAD_DIM, HEAD_DIM), F32)
    of_c, ob_c, s_c = dn_scan(qkv_c, p_c, par, s0)
    of_x, ob_x, _ = dn_scan(qkv_x, p_x, par, s_c)
    dn_x = dn_out(of_x, ob_x, p_x, dn_norm_w[0])
    dn_c = dn_out(of_c, ob_c, p_c, dn_norm_w[0])

    qk_x = matmul(hx2, w_dqk, out_dtype=F32).reshape(b, t, -1)
    qk_c = matmul(hc2, w_dqk, out_dtype=F32).reshape(b, lctx, -1)
    v_x = matmul(hx2, w_dv, out_dtype=BF16).reshape(b, t, -1)
    v_c = matmul(hc2, w_dv, out_dtype=BF16).reshape(b, lctx, -1)
    cos_d, sin_d = rope_tables(t, DIFF_DQK, DIFF_DQK // 4)
    one_c = jnp.ones((lctx, LANES), F32)
    zero_c = jnp.zeros((lctx, LANES), F32)
    dscale = DIFF_DQK ** -0.5 * math.log2(math.e)
    q1_x, q2_x = rope(qk_x, 0, dq_w, cos_d, sin_d, half=DIFF_DQK // 4, scale=dscale, split=True)
    k_x = rope(qk_x, 1, dq_w, cos_d, sin_d, half=DIFF_DQK // 4)
    q1_c, q2_c = rope(qk_c, 0, dq_w, one_c, zero_c, half=DIFF_DQK // 4, scale=dscale, split=True)
    k_c = rope(qk_c, 1, dq_w, one_c, zero_c, half=DIFF_DQK // 4)
    k_all = jnp.concatenate([k_x, k_c], axis=1)
    v_all = jnp.concatenate([v_x, v_c], axis=1)
    d_x = diff_attention(q1_x, q2_x, k_all, v_all, diff_lambda[0], diff_norm_w[0], lam_init)
    d_c = diff_attention(q1_c, q2_c, k_c, v_c, diff_lambda[0], diff_norm_w[0], lam_init)

    w_out = ab_w_out[0].astype(BF16)
    x = _out_proj(jnp.concatenate([dn_x, d_x], axis=-1), w_out, x, mod_x(0, 2))
    ctx = _out_proj(jnp.concatenate([dn_c, d_c], axis=-1), w_out, ctx, mod_c(0, 2))

    wq = peer_wq[0].astype(BF16)
    sub = peer_subkeys[0].astype(BF16)
    u_tab = peer_u[0].astype(BF16)
    vt_tab = peer_v[0].T.astype(BF16)
    fx = normmod(x, norm_ffn_w[0], mod_x(0, 3), mod_x(0, 4))
    x = _peer_ffn(x, fx, mod_x(0, 5), wq, peer_bq[0], sub, u_tab, vt_tab)
    fc = normmod(ctx, norm_ffn_w[0], mod_c(0, 3), mod_c(0, 4))
    ctx = _peer_ffn(ctx, fc, mod_c(0, 5), wq, peer_bq[0], sub, u_tab, vt_tab)

    gq_w = GQA_Q_HEADS * HEAD_DIM
    gkv_w = GQA_KV_HEADS * HEAD_DIM
    w_g = gqa_w_in[0]
    w_gqk = w_g[:, :gq_w + gkv_w].astype(BF16)
    w_gv = w_g[:, gq_w + gkv_w:].astype(BF16)
    w_gkv = w_g[:, gq_w:].astype(BF16)
    hx = normmod(x, norm_mix_w[1], mod_x(1, 0), mod_x(1, 1))
    hc = normmod(ctx, norm_mix_w[1], mod_c(1, 0), mod_c(1, 1))
    hx2 = hx.reshape(n, d)
    qk = matmul(hx2, w_gqk, out_dtype=F32, tn_cap=1280).reshape(b, t, -1)
    gv = matmul(hx2, w_gv, out_dtype=BF16).reshape(b, t, -1)
    kvx = matmul(hc.reshape(nctx, d), w_gkv, out_dtype=BF16).reshape(b, lctx, -1)
    cos_g, sin_g = rope_tables(t, HEAD_DIM, HEAD_DIM // 4)
    gq = rope(qk, 0, gq_w, cos_g, sin_g, half=HEAD_DIM // 4, scale=HEAD_DIM ** -0.5)
    gk = rope(qk, gq_w // gkv_w, gkv_w, cos_g, sin_g, half=HEAD_DIM // 4)
    att = window_attention(gq, gk, gv, kvx, gqa_sink[0])
    x = _out_proj(att, gqa_w_out[0].astype(BF16), x, mod_x(1, 2))

    wq = peer_wq[1].astype(BF16)
    sub = peer_subkeys[1].astype(BF16)
    u_tab = peer_u[1].astype(BF16)
    vt_tab = peer_v[1].T.astype(BF16)
    fx = normmod(x, norm_ffn_w[1], mod_x(1, 3), mod_x(1, 4))
    return _peer_ffn(x, fx, mod_x(1, 5), wq, peer_bq[1], sub, u_tab, vt_tab, final_w=final_norm_w)
```

```python
import functools
import math

import jax
import jax.numpy as jnp
import numpy as np
from jax import lax
from jax.experimental import pallas as pl
from jax.experimental.pallas import tpu as pltpu

F32 = jnp.float32
BF16 = jnp.bfloat16

LANES = 128
HEAD_DIM = 128
NORM_EPS = 1e-6
ROPE_THETA = 10000.0
GRID_W = 64
DN_HEADS = 8
DN_CHUNK = 64
DN_CONV = 5
DIFF_HEADS = 8
DIFF_DQK = 64
GQA_Q_HEADS = 16
GQA_KV_HEADS = 4
GQA_GROUP = GQA_Q_HEADS // GQA_KV_HEADS
WINDOW = 128
PEER_HEADS = 8
PEER_NKEYS = 128
PEER_TOPK = 16
NEG_BIG = -1e30
VMEM_LIMIT = 56 * 1024 * 1024


def _params(sem):
    return pltpu.CompilerParams(dimension_semantics=sem, vmem_limit_bytes=VMEM_LIMIT)


def _dot(a, b):
    return jnp.dot(a.astype(BF16), b.astype(BF16), preferred_element_type=F32)


def _dot_nt(a, b):
    return lax.dot_general(a.astype(BF16), b.astype(BF16), (((1,), (1,)), ((), ())),
                           preferred_element_type=F32)


def _sigmoid(x):
    return 1.0 / (1.0 + jnp.exp(-x))


def _normmod_kernel(x_ref, w_ref, shift_ref, scale_ref, o_ref):
    x = x_ref[...]
    y = x * lax.rsqrt(jnp.mean(x * x, axis=-1, keepdims=True) + NORM_EPS)
    y = y * w_ref[...]
    o_ref[...] = (y * (1.0 + scale_ref[...]) + shift_ref[...]).astype(o_ref.dtype)


def normmod(x, w, shift, scale, rows=512):
    b, t, d = x.shape
    tr = min(rows, t)
    bm = shift.shape[0]
    mod_map = (lambda i, j: (i, 0, 0)) if bm == b else (lambda i, j: (0, 0, 0))
    return pl.pallas_call(
        _normmod_kernel,
        grid=(b, t // tr),
        in_specs=[pl.BlockSpec((None, tr, d), lambda i, j: (i, j, 0)),
                  pl.BlockSpec((1, d), lambda i, j: (0, 0)),
                  pl.BlockSpec((None, 1, d), mod_map),
                  pl.BlockSpec((None, 1, d), mod_map)],
        out_specs=pl.BlockSpec((None, tr, d), lambda i, j: (i, j, 0)),
        out_shape=jax.ShapeDtypeStruct((b, t, d), BF16),
        compiler_params=_params(("parallel", "parallel")),
        name="normmod",
    )(x, w.reshape(1, d), shift.reshape(bm, 1, d), scale.reshape(bm, 1, d))


def _mm_kernel(*refs, silu_in, has_bias, has_res):
    a_ref, w_ref = refs[0], refs[1]
    pos = 2
    a = a_ref[...]
    if silu_in:
        a = a * _sigmoid(a)
    acc = _dot(a, w_ref[...])
    if has_bias:
        acc = acc + refs[pos][...]
        pos += 1
    if has_res:
        acc = refs[pos][...] + refs[pos + 1][...] * acc
        pos += 2
    o_ref = refs[pos]
    o_ref[...] = acc.astype(o_ref.dtype)


def _pick_tn(n, cap):
    best = LANES
    for cand in range(LANES, min(n, cap) + 1, LANES):
        if n % cand == 0:
            best = cand
    return best


def matmul(a, w, *, out_dtype, bias=None, res=None, gate=None, rows_per_gate=None,
           silu_in=False, tm=512, tn_cap=2048, layer=0):
    m, k = a.shape
    n = w.shape[-1]
    tm = min(tm, m)
    tn = _pick_tn(n, tn_cap)
    if w.ndim == 3:
        w_spec = pl.BlockSpec((None, k, tn), lambda j, i: (layer, 0, j))
    else:
        w_spec = pl.BlockSpec((k, tn), lambda j, i: (0, j))
    in_specs = [pl.BlockSpec((tm, k), lambda j, i: (i, 0)), w_spec]
    args = [a, w]
    if bias is not None:
        in_specs.append(pl.BlockSpec((1, tn), lambda j, i: (0, j)))
        args.append(bias.reshape(1, n).astype(F32))
    if res is not None:
        blocks_per_gate = rows_per_gate // tm
        in_specs.append(pl.BlockSpec((tm, tn), lambda j, i: (i, j)))
        in_specs.append(pl.BlockSpec((None, 1, tn), lambda j, i: (i // blocks_per_gate, 0, j)))
        args += [res, gate.reshape(gate.shape[0], 1, n)]
    kern = functools.partial(_mm_kernel, silu_in=silu_in, has_bias=bias is not None,
                             has_res=res is not None)
    return pl.pallas_call(
        kern,
        grid=(n // tn, m // tm),
        in_specs=in_specs,
        out_specs=pl.BlockSpec((tm, tn), lambda j, i: (i, j)),
        out_shape=jax.ShapeDtypeStruct((m, n), out_dtype),
        compiler_params=_params(("parallel", "parallel")),
        name="matmul",
    )(*args)


def _conv_kernel(prev_ref, cur_ref, next_ref, w_ref, o_ref, *, tt, nheads_blk):
    i = pl.program_id(1)
    grp = pl.program_id(2)
    last = pl.num_programs(1) - 1
    prev = prev_ref[...] * (i > 0).astype(F32)
    nxt = next_ref[...] * (i < last).astype(F32)
    xx = jnp.concatenate([prev, cur_ref[...], nxt], axis=0)
    rows = tt + 16
    pad = (DN_CONV - 1) // 2
    y = None
    for k in range(DN_CONV):
        shift = (pad - k) % rows
        z = xx if shift == 0 else pltpu.roll(xx, shift, 0)
        term = z[8:8 + tt] * w_ref[k:k + 1, :]
        y = term if y is None else y + term
    y = y * _sigmoid(y)
    qscale = jnp.where(grp == 0, HEAD_DIM ** -0.5, 1.0).astype(F32)
    for h in range(nheads_blk):
        yh = y[:, h * HEAD_DIM:(h + 1) * HEAD_DIM]
        ss = jnp.sum(yh * yh, axis=-1, keepdims=True)
        fac = jnp.where(grp == 2, 1.0, lax.rsqrt(ss + NORM_EPS) * qscale)
        o_ref[:, h * HEAD_DIM:(h + 1) * HEAD_DIM] = yh * fac


def dn_conv(p, conv_w, tt=256):
    b, t, _ = p.shape
    tt = min(tt, t)
    cb = DN_HEADS * HEAD_DIM
    n8 = t // 8
    kern = functools.partial(_conv_kernel, tt=tt, nheads_blk=DN_HEADS)
    return pl.pallas_call(
        kern,
        grid=(b, t // tt, 3),
        in_specs=[pl.BlockSpec((None, 8, cb), lambda bi, i, g: (bi, jnp.maximum(i * (tt // 8) - 1, 0), g)),
                  pl.BlockSpec((None, tt, cb), lambda bi, i, g: (bi, i, g)),
                  pl.BlockSpec((None, 8, cb), lambda bi, i, g: (bi, jnp.minimum((i + 1) * (tt // 8), n8 - 1), g)),
                  pl.BlockSpec((DN_CONV, cb), lambda bi, i, g: (0, g))],
        out_specs=pl.BlockSpec((None, tt, cb), lambda bi, i, g: (bi, i, g)),
        out_shape=jax.ShapeDtypeStruct((b, t, 3 * cb), F32),
        compiler_params=_params(("parallel", "parallel", "parallel")),
        name="dn_conv",
    )(p, p, p, conv_w)


def _split3(x):
    hi = x.astype(BF16)
    r1 = x - hi.astype(F32)
    mid = r1.astype(BF16)
    lo = (r1 - mid.astype(F32)).astype(BF16)
    return hi, mid, lo


def _dn_chains(direction, q_ref, k_ref, v_ref, p_ref, par_ref, o_ref, s_scr):
    c = DN_CHUNK
    ii = lax.broadcasted_iota(jnp.int32, (c, c), 0)
    jj = lax.broadcasted_iota(jnp.int32, (c, c), 1)
    incl = (ii >= jj) if direction == 0 else (ii <= jj)
    strict = (ii > jj) if direction == 0 else (ii < jj)

    raw = p_ref[...]
    beta_all = _sigmoid(raw)
    xa = raw + par_ref[0:1, :]
    softplus = jnp.maximum(xa, 0.0) + jnp.log1p(jnp.exp(-jnp.abs(xa)))
    g_all = -jnp.exp(par_ref[1:2, :]) * softplus
    inclb = incl.astype(BF16)
    hi, mid, lo = _split3(g_all)
    gc_all = (jnp.dot(inclb, hi, preferred_element_type=F32)
              + jnp.dot(inclb, mid, preferred_element_type=F32)
              + jnp.dot(inclb, lo, preferred_element_type=F32))
    gtot_all = jnp.sum(g_all, axis=0, keepdims=True)
    gc_t = gc_all.T

    chains = []
    for h in range(DN_HEADS):
        ib = direction * DN_HEADS + h
        ig = 2 * DN_HEADS + ib
        sl = slice(h * HEAD_DIM, (h + 1) * HEAD_DIM)
        gc_c = gc_all[:, ig:ig + 1]
        chains.append(dict(
            incl=incl, strict=strict, beta=beta_all[:, ib:ib + 1], gc_c=gc_c,
            decay=jnp.exp(jnp.where(incl, gc_c - gc_t[ig:ig + 1, :], -jnp.inf)),
            gtot=gtot_all[:, ig:ig + 1], q=q_ref[:, sl], k=k_ref[:, sl], v=v_ref[:, sl],
            o_ref=o_ref, sl=sl, state_idx=(direction, h)))
    return chains


def _dn_step(chains, s_scr):
    c = DN_CHUNK
    ii = lax.broadcasted_iota(jnp.int32, (c, c), 0)
    jj = lax.broadcasted_iota(jnp.int32, (c, c), 1)
    eye = (ii == jj).astype(F32)
    for ch in chains:
        ch["kb"] = ch["k"] * ch["beta"]
        ch["kq"] = _dot_nt(jnp.concatenate([ch["kb"], ch["q"]], axis=0), ch["k"])
    for ch in chains:
        a = jnp.where(ch["strict"], ch["kq"][:c] * ch["decay"], 0.0)
        ch["qk"] = jnp.where(ch["incl"], ch["kq"][c:] * ch["decay"], 0.0)
        ch["inv"] = eye - a
        ch["a_pow"] = a
    for _ in range(int(math.log2(c)) - 1):
        for ch in chains:
            ch["a_pow"] = _dot(ch["a_pow"], ch["a_pow"])
        for ch in chains:
            ch["inv"] = _dot(ch["inv"], eye + ch["a_pow"])
    for ch in chains:
        egc = jnp.exp(ch["gc_c"])
        ch["uw"] = _dot(ch["inv"], jnp.concatenate([ch["v"] * ch["beta"], ch["kb"] * egc], axis=1))
        ch["q_dec"] = ch["q"] * egc
    for ch in chains:
        ch["state"] = s_scr[ch["state_idx"]]
        w = ch["uw"][:, HEAD_DIM:]
        ch["ws"] = _dot(jnp.concatenate([w, ch["q_dec"]], axis=0), ch["state"])
    for ch in chains:
        ch["v_new"] = ch["uw"][:, :HEAD_DIM] - ch["ws"][:c]
        ch["o_ref"][:, ch["sl"]] = ch["ws"][c:] + _dot(ch["qk"], ch["v_new"])
    for ch in chains:
        k_dec = ch["k"] * jnp.exp(ch["gtot"] - ch["gc_c"])
        s_scr[ch["state_idx"]] = ch["state"] * jnp.exp(ch["gtot"]) + _dot(k_dec.T, ch["v_new"])


def _dn_scan_kernel(qf_ref, kf_ref, vf_ref, pf_ref, qb_ref, kb_ref, vb_ref, pb_ref, par_ref, s0_ref,
                    of_ref, ob_ref, sfin_ref, s_scr):
    n = pl.program_id(1)

    @pl.when(n == 0)
    def _():
        s_scr[...] = s0_ref[...]

    chains = (_dn_chains(0, qf_ref, kf_ref, vf_ref, pf_ref, par_ref, of_ref, s_scr)
              + _dn_chains(1, qb_ref, kb_ref, vb_ref, pb_ref, par_ref, ob_ref, s_scr))
    _dn_step(chains, s_scr)

    @pl.when(n == pl.num_programs(1) - 1)
    def _():
        sfin_ref[...] = s_scr[...]


def dn_scan(qkv, p, par, s0):
    b, t, _ = qkv.shape
    c = DN_CHUNK
    nc = t // c
    w = DN_HEADS * HEAD_DIM
    fwd = lambda col: (lambda bi, n: (bi, n, col))
    bwd = lambda col: (lambda bi, n: (bi, nc - 1 - n, col))
    blk = lambda f: pl.BlockSpec((None, c, w), f)
    small = lambda f: pl.BlockSpec((None, c, LANES), f)
    state_spec = pl.BlockSpec((None, 2, DN_HEADS, HEAD_DIM, HEAD_DIM), lambda bi, n: (bi, 0, 0, 0, 0))
    o_shape = jax.ShapeDtypeStruct((b, t, w), F32)
    return pl.pallas_call(
        _dn_scan_kernel,
        grid=(b, nc),
        in_specs=[blk(fwd(0)), blk(fwd(1)), blk(fwd(2)), small(fwd(32)),
                  blk(bwd(0)), blk(bwd(1)), blk(bwd(2)), small(bwd(32)),
                  pl.BlockSpec((8, LANES), lambda bi, n: (0, 0)), state_spec],
        out_specs=[blk(fwd(0)), blk(bwd(0)), state_spec],
        out_shape=[o_shape, o_shape, jax.ShapeDtypeStruct(s0.shape, F32)],
        scratch_shapes=[pltpu.VMEM((2, DN_HEADS, HEAD_DIM, HEAD_DIM), F32)],
        compiler_params=_params(("parallel", "arbitrary")),
        name="dn_scan",
    )(qkv, qkv, qkv, p, qkv, qkv, qkv, p, par, s0)


def _dn_out_kernel(of_ref, ob_ref, z_ref, w_ref, y_ref):
    o = of_ref[...] + ob_ref[...]
    z = z_ref[...]
    for h in range(DN_HEADS):
        sl = slice(h * HEAD_DIM, (h + 1) * HEAD_DIM)
        oh = o[:, sl]
        yh = oh * lax.rsqrt(jnp.mean(oh * oh, axis=-1, keepdims=True) + NORM_EPS) * w_ref[...]
        zh = z[:, sl]
        y_ref[:, sl] = (yh * (zh * _sigmoid(zh))).astype(y_ref.dtype)


def dn_out(o_f, o_b, p, norm_w, tr=256):
    b, t, w = o_f.shape
    tr = min(tr, t)
    return pl.pallas_call(
        _dn_out_kernel,
        grid=(b, t // tr),
        in_specs=[pl.BlockSpec((None, tr, w), lambda bi, i: (bi, i, 0)),
                  pl.BlockSpec((None, tr, w), lambda bi, i: (bi, i, 0)),
                  pl.BlockSpec((None, tr, w), lambda bi, i: (bi, i, 3)),
                  pl.BlockSpec((1, HEAD_DIM), lambda bi, i: (0, 0))],
        out_specs=pl.BlockSpec((None, tr, w), lambda bi, i: (bi, i, 0)),
        out_shape=jax.ShapeDtypeStruct((b, t, w), BF16),
        compiler_params=_params(("parallel", "parallel")),
        name="dn_out",
    )(o_f, o_b, p, norm_w.reshape(1, HEAD_DIM))


def _rope_kernel(x_ref, cos_ref, sin_ref, *o_refs, half, scale, nheads, split):
    lane = lax.broadcasted_iota(jnp.int32, (1, LANES), 1)
    first = (lane % (2 * half)) < half
    cos = cos_ref[...]
    sin = sin_ref[...]
    for h in range(nheads):
        sl = slice(h * LANES, (h + 1) * LANES)
        x = x_ref[:, sl]
        partner = jnp.where(first, pltpu.roll(x, LANES - half, 1), pltpu.roll(x, half, 1))
        y = (x * cos + partner * sin) * scale
        if split:
            o_refs[0][:, sl] = jnp.where(lane < LANES // 2, y, 0.0).astype(BF16)
            o_refs[1][:, sl] = jnp.where(lane >= LANES // 2, y, 0.0).astype(BF16)
        else:
            o_refs[0][:, sl] = y.astype(BF16)


def rope(x, col_block, width, cos, sin, *, half, scale=1.0, split=False, tr=512):
    b, t, _ = x.shape
    tr = min(tr, t)
    nheads = width // LANES
    nout = 2 if split else 1
    kern = functools.partial(_rope_kernel, half=half, scale=scale, nheads=nheads, split=split)
    out = pl.pallas_call(
        kern,
        grid=(b, t // tr),
        in_specs=[pl.BlockSpec((None, tr, width), lambda bi, i: (bi, i, col_block)),
                  pl.BlockSpec((tr, LANES), lambda bi, i: (i, 0)),
                  pl.BlockSpec((tr, LANES), lambda bi, i: (i, 0))],
        out_specs=[pl.BlockSpec((None, tr, width), lambda bi, i: (bi, i, 0))] * nout,
        out_shape=[jax.ShapeDtypeStruct((b, t, width), BF16)] * nout,
        compiler_params=_params(("parallel", "parallel")),
        name="rope",
    )(x, cos, sin)
    return out if split else out[0]


def rope_tables(t, block, half):
    pos = jnp.arange(t, dtype=jnp.int32)
    rows = (pos // GRID_W).astype(F32)
    cols = (pos % GRID_W).astype(F32)
    lane = np.arange(LANES)
    j = lane % block
    use_col = (j // (2 * half)) == 1
    i = j % (2 * half)
    inv = (ROPE_THETA ** (-jnp.arange(half, dtype=F32) / half))[i % half]
    p = jnp.where(jnp.asarray(use_col)[None, :], cols[:, None], rows[:, None])
    ang = p * inv[None, :]
    sign = jnp.asarray(np.where(i < half, -1.0, 1.0).astype(np.float32))[None, :]
    return jnp.cos(ang), jnp.sin(ang) * sign


def _diff_kernel(q1_ref, q2_ref, k_ref, vt_ref, lam_ref, nw_ref, o_ref, acc1, acc2, s_scr,
                 *, lam_init, tk, nkv):
    tq = q1_ref.shape[0]
    acc1[...] = jnp.zeros_like(acc1)
    acc2[...] = jnp.zeros_like(acc2)

    def scores(c, slot):
        kc = k_ref[pl.ds(pl.multiple_of(c * tk, tk), tk), :]
        s_scr[slot, 0] = _dot_nt(kc, q1_ref[...])
        s_scr[slot, 1] = _dot_nt(kc, q2_ref[...])

    def update(s, vtc, m, l, acc):
        m_new = jnp.maximum(m, jnp.max(s, axis=0, keepdims=True))
        alpha = jnp.exp2(m - m_new)
        p = jnp.exp2(s - m_new)
        l_new = l * alpha + jnp.sum(p, axis=0, keepdims=True)
        acc[...] = acc[...] * alpha + _dot(vtc, p)
        return m_new, l_new

    def softmax(c, slot, carry):
        m1, l1, m2, l2 = carry
        vtc = vt_ref[c]
        m1, l1 = update(s_scr[slot, 0], vtc, m1, l1, acc1)
        m2, l2 = update(s_scr[slot, 1], vtc, m2, l2, acc2)
        return m1, l1, m2, l2

    def body(i, carry):
        c = 2 * i
        scores(c + 1, 1)
        carry = softmax(c, 0, carry)
        scores(c + 2, 0)
        return softmax(c + 1, 1, carry)

    neg = jnp.full((1, tq), -jnp.inf, F32)
    zero = jnp.zeros((1, tq), F32)
    scores(0, 0)
    pairs = (nkv - 1) // 2
    carry = lax.fori_loop(0, pairs, body, (neg, zero, neg, zero))
    if nkv % 2 == 0:
        scores(nkv - 1, 1)
        carry = softmax(nkv - 2, 0, carry)
        carry = softmax(nkv - 1, 1, carry)
    else:
        carry = softmax(nkv - 1, 0, carry)
    m1, l1, m2, l2 = carry

    lv = lam_ref[...]
    s01 = jnp.sum(lv[0:1] * lv[1:2], axis=-1, keepdims=True)
    s23 = jnp.sum(lv[2:3] * lv[3:4], axis=-1, keepdims=True)
    lam = jnp.exp(s01) - jnp.exp(s23) + lam_init
    o = acc1[...] / l1 - lam * (acc2[...] / l2)
    y = o * lax.rsqrt(jnp.mean(o * o, axis=0, keepdims=True) + NORM_EPS)
    o_ref[...] = (y.T * nw_ref[...] * (1.0 - lam_init)).astype(o_ref.dtype)


def diff_attention(q1, q2, k, v, lam_vec, norm_w, lam_init, tq=1024, tk=768):
    b, tqa, w = q1.shape
    tka = k.shape[1]
    tq = min(tq, tqa)
    tk = min(tk, tka)
    nkv = tka // tk
    nh = w // HEAD_DIM
    vt = v.reshape(b, nkv, tk, nh, HEAD_DIM).transpose(0, 3, 1, 4, 2)
    kern = functools.partial(_diff_kernel, lam_init=lam_init, tk=tk, nkv=nkv)
    return pl.pallas_call(
        kern,
        grid=(b, nh, tqa // tq),
        in_specs=[pl.BlockSpec((None, tq, HEAD_DIM), lambda bi, h, i: (bi, i, h)),
                  pl.BlockSpec((None, tq, HEAD_DIM), lambda bi, h, i: (bi, i, h)),
                  pl.BlockSpec((None, tka, HEAD_DIM), lambda bi, h, i: (bi, 0, h)),
                  pl.BlockSpec((None, None, nkv, HEAD_DIM, tk), lambda bi, h, i: (bi, h, 0, 0, 0)),
                  pl.BlockSpec((4, DIFF_DQK), lambda bi, h, i: (0, 0)),
                  pl.BlockSpec((1, HEAD_DIM), lambda bi, h, i: (0, 0))],
        out_specs=pl.BlockSpec((None, tq, HEAD_DIM), lambda bi, h, i: (bi, i, h)),
        out_shape=jax.ShapeDtypeStruct((b, tqa, w), BF16),
        scratch_shapes=[pltpu.VMEM((HEAD_DIM, tq), F32)] * 2 + [pltpu.VMEM((2, 2, tk, tq), F32)],
        compiler_params=_params(("parallel", "parallel", "parallel")),
        name="diff_attention",
    )(q1, q2, k, vt, lam_vec, norm_w.reshape(1, HEAD_DIM))


def _win_kernel(sink_ref, q_ref, kp_ref, kc_ref, kn_ref, vp_ref, vc_ref, vn_ref, kx_ref, vx_ref, o_ref,
                *, t_total):
    kvh = pl.program_id(1)
    n = pl.program_id(2)
    wb = WINDOW
    tq = q_ref.shape[0]
    keys = jnp.concatenate([kp_ref[...], kc_ref[...], kn_ref[...], kx_ref[...]], axis=0)
    vals = jnp.concatenate([vp_ref[...], vc_ref[...], vn_ref[...], vx_ref[...]], axis=0)
    nk = keys.shape[0]
    nlocal = tq + 2 * wb
    i = lax.broadcasted_iota(jnp.int32, (tq, nk), 0)
    j = lax.broadcasted_iota(jnp.int32, (tq, nk), 1)
    kpos = n * tq - wb + j
    valid = (jnp.abs(j - wb - i) <= WINDOW) & (kpos >= 0) & (kpos < t_total)
    valid = valid | (j >= nlocal)
    for g in range(GQA_GROUP):
        sl = slice(g * HEAD_DIM, (g + 1) * HEAD_DIM)
        sink = sink_ref[kvh * GQA_GROUP + g]
        s = jnp.where(valid, _dot_nt(q_ref[:, sl], keys), -jnp.inf)
        m = jnp.maximum(jnp.max(s, axis=-1, keepdims=True), sink)
        p = jnp.exp(s - m)
        denom = jnp.sum(p, axis=-1, keepdims=True) + jnp.exp(sink - m)
        o_ref[:, sl] = (_dot(p, vals) / denom).astype(o_ref.dtype)


def window_attention(q, k, v, kvx, sink, tq=512):
    b, t, _ = q.shape
    lctx = kvx.shape[1]
    wb = WINDOW
    tq = min(tq, t)
    per = tq // wb
    nb = t // wb
    qw = GQA_GROUP * HEAD_DIM
    kern = functools.partial(_win_kernel, t_total=t)
    prev = lambda bi, h, n: (bi, jnp.maximum(n * per - 1, 0), h)
    cur = lambda bi, h, n: (bi, n, h)
    nxt = lambda bi, h, n: (bi, jnp.minimum((n + 1) * per, nb - 1), h)
    edge_spec = lambda f: pl.BlockSpec((None, wb, HEAD_DIM), f)
    main_spec = pl.BlockSpec((None, tq, HEAD_DIM), cur)
    return pl.pallas_call(
        kern,
        grid=(b, GQA_KV_HEADS, t // tq),
        in_specs=[pl.BlockSpec(memory_space=pltpu.SMEM),
                  pl.BlockSpec((None, tq, qw), cur),
                  edge_spec(prev), main_spec, edge_spec(nxt),
                  edge_spec(prev), main_spec, edge_spec(nxt),
                  pl.BlockSpec((None, lctx, HEAD_DIM), lambda bi, h, n: (bi, 0, h)),
                  pl.BlockSpec((None, lctx, HEAD_DIM), lambda bi, h, n: (bi, 0, GQA_KV_HEADS + h))],
        out_specs=pl.BlockSpec((None, tq, qw), cur),
        out_shape=jax.ShapeDtypeStruct(q.shape, BF16),
        compiler_params=_params(("parallel", "parallel", "parallel")),
        name="window_attention",
    )(sink, q, k, k, k, v, v, v, kvx, kvx)


RANK_NONE = 255.0
RANK_CODE_BASE = 1e30
RANK_CODE_STEP = 1e28


def _top_values(s, count, want_rank=False):
    vals = []
    work = s
    for r in range(count):
        m = jnp.max(work, axis=0, keepdims=True)
        vals.append(m)
        work = jnp.where(work == m, -(RANK_CODE_BASE + r * RANK_CODE_STEP), work)
    if not want_rank:
        return vals
    rank = jnp.where(work < -0.5 * RANK_CODE_BASE,
                     jnp.round((-work - RANK_CODE_BASE) * (1.0 / RANK_CODE_STEP)), RANK_NONE)
    return vals, rank


def _route_kernel(q_ref, sub_ref, cut_ref, e1_ref, r2_ref, e2_ref):
    kk = PEER_TOPK
    tb = q_ref.shape[0]
    row = lax.broadcasted_iota(jnp.int32, (kk, tb), 0)
    for h in range(PEER_HEADS):
        scores, tops, ranks = [], [], []
        for p in range(2):
            c0 = (h * 2 + p) * PEER_NKEYS
            s = _dot_nt(sub_ref[h, p], q_ref[:, c0:c0 + PEER_NKEYS])
            vals, rank = _top_values(s, kk, want_rank=True)
            scores.append(s)
            tops.append(vals)
            ranks.append(rank)
        v1 = jnp.zeros((kk, tb), F32)
        v2 = jnp.zeros((kk, tb), F32)
        for i in range(kk):
            v1 = jnp.where(row == i, tops[0][i], v1)
            v2 = jnp.where(row == i, tops[1][i], v2)
        mid = jnp.where(row[:8] >= 4, v1[:8], -jnp.inf)
        cand = jnp.concatenate([tops[0][0] + v2] + [tops[0][i] + v2[:8] for i in range(1, 4)]
                               + [mid + tops[1][j] for j in range(3)] + [v1[8:] + tops[1][0]], axis=0)
        thr = _top_values(cand, kk)[kk - 1]
        top = tops[0][0] + tops[1][0]
        z = jnp.sum(jnp.where(cand >= thr, jnp.exp(cand - top), 0.0), axis=0, keepdims=True)
        gdt = r2_ref.dtype
        rank1 = ranks[0].astype(gdt)
        cut = jnp.zeros(rank1.shape, gdt)
        for i in range(kk):
            count = jnp.sum(((tops[0][i] + v2) >= thr).astype(F32), axis=0, keepdims=True)
            cut = jnp.where(rank1 == i, count.astype(gdt), cut)
        in1 = ranks[0] < float(kk)
        in2 = ranks[1] < float(kk)
        cut_ref[h] = cut.astype(F32)
        e1_ref[h] = jnp.where(in1, jnp.exp(scores[0] - tops[0][0]), 0.0)
        r2_ref[h] = ranks[1].astype(r2_ref.dtype)
        e2_ref[h] = jnp.where(in2, jnp.exp(scores[1] - tops[1][0]) / z, 0.0).astype(e2_ref.dtype)


def peer_route(q, subkeys, tb=128):
    n = q.shape[0]
    tb = min(tb, n)
    hh = PEER_HEADS
    tab32 = jax.ShapeDtypeStruct((hh, PEER_NKEYS, n), F32)
    tab16 = jax.ShapeDtypeStruct((hh, PEER_NKEYS, n), BF16)
    tab_spec = pl.BlockSpec((hh, PEER_NKEYS, tb), lambda i: (0, 0, i))
    return pl.pallas_call(
        _route_kernel,
        grid=(n // tb,),
        in_specs=[pl.BlockSpec((tb, q.shape[1]), lambda i: (i, 0)),
                  pl.BlockSpec(subkeys.shape, lambda i: (0, 0, 0, 0))],
        out_specs=[tab_spec] * 4,
        out_shape=[tab32, tab32, tab16, tab16],
        compiler_params=_params(("parallel",)),
        name="peer_route",
    )(q, subkeys)


def _gelu(x):
    return 0.5 * x * (1.0 + lax.erf(x * (2.0 ** -0.5)))


def _peer_kernel(xt_ref, u_ref, vt_ref, cut_ref, e1_ref, r2_ref, e2_ref, o_ref, acc, *, rows, rows_per_part):
    et = pl.program_id(1)

    @pl.when(et == 0)
    def _():
        acc[...] = jnp.zeros_like(acc)

    nk = PEER_NKEYS
    gdt = r2_ref.dtype
    nparts = rows // rows_per_part
    pk = rows_per_part * nk

    def gates(part):
        out = []
        for r in range(part * rows_per_part, (part + 1) * rows_per_part):
            gate = None
            for h in range(PEER_HEADS):
                cut = cut_ref[h, r:r + 1, :].astype(gdt)
                e1 = e1_ref[h, r:r + 1, :].astype(gdt)
                g = jnp.where(r2_ref[h] < cut, e1 * e2_ref[h], jnp.zeros((), gdt))
                gate = g if gate is None else gate + g
            out.append(gate)
        return out

    def hidden(part):
        return jnp.dot(u_ref[part * pk:(part + 1) * pk, :], xt_ref[...], preferred_element_type=F32)

    def project(part, ht, gate_rows):
        a_rows = [(_gelu(ht[i * nk:(i + 1) * nk, :]) * g.astype(F32)).astype(BF16)
                  for i, g in enumerate(gate_rows)]
        return jnp.dot(vt_ref[:, part * pk:(part + 1) * pk], jnp.concatenate(a_rows, axis=0),
                       preferred_element_type=F32)

    g_cur = gates(0)
    h_cur = hidden(0)
    total = None
    for part in range(nparts):
        if part + 1 < nparts:
            g_next = gates(part + 1)
            h_next = hidden(part + 1)
        prod = project(part, h_cur, g_cur)
        total = prod if total is None else total + prod
        if part + 1 < nparts:
            g_cur, h_cur = g_next, h_next
    acc[...] += total

    @pl.when(et == pl.num_programs(1) - 1)
    def _():
        o_ref[...] = acc[...].T


def peer_experts(xt, u, vt, cut, e1, r2, e2, tb=512, rows=8, rows_per_part=2):
    d, n = xt.shape
    e = u.shape[0]
    tb = min(tb, n)
    te = rows * PEER_NKEYS
    hh = PEER_HEADS
    kern = functools.partial(_peer_kernel, rows=rows, rows_per_part=rows_per_part)
    row_spec = pl.BlockSpec((hh, rows, tb), lambda i, j: (0, j, i))
    full_spec = pl.BlockSpec((hh, PEER_NKEYS, tb), lambda i, j: (0, 0, i))
    return pl.pallas_call(
        kern,
        grid=(n // tb, e // te),
        in_specs=[pl.BlockSpec((d, tb), lambda i, j: (0, i)),
                  pl.BlockSpec((te, d), lambda i, j: (j, 0)),
                  pl.BlockSpec((d, te), lambda i, j: (0, j)),
                  row_spec, row_spec, full_spec, full_spec],
        out_specs=pl.BlockSpec((tb, d), lambda i, j: (i, 0)),
        out_shape=jax.ShapeDtypeStruct((n, d), F32),
        scratch_shapes=[pltpu.VMEM((d, tb), F32)],
        compiler_params=_params(("parallel", "arbitrary")),
        name="peer_experts",
    )(xt, u, vt, cut, e1, r2, e2)


def _resid_kernel(x_ref, y_ref, g_ref, w_ref, o_ref, *, final):
    x = x_ref[...] + g_ref[...] * y_ref[...]
    if final:
        x = x * lax.rsqrt(jnp.mean(x * x, axis=-1, keepdims=True) + NORM_EPS) * w_ref[...]
    o_ref[...] = x


def resid(x, y, gate, final_w=None, rows=512):
    b, t, d = x.shape
    tr = min(rows, t)
    bm = gate.shape[0]
    mod_map = (lambda i, j: (i, 0, 0)) if bm == b else (lambda i, j: (0, 0, 0))
    w = jnp.ones((d,), F32) if final_w is None else final_w
    kern = functools.partial(_resid_kernel, final=final_w is not None)
    blk = pl.BlockSpec((None, tr, d), lambda i, j: (i, j, 0))
    return pl.pallas_call(
        kern,
        grid=(b, t // tr),
        in_specs=[blk, blk, pl.BlockSpec((None, 1, d), mod_map), pl.BlockSpec((1, d), lambda i, j: (0, 0))],
        out_specs=blk,
        out_shape=jax.ShapeDtypeStruct((b, t, d), F32),
        compiler_params=_params(("parallel", "parallel")),
        name="resid",
    )(x, y, gate.reshape(bm, 1, d), w.reshape(1, d))


def _peer_ffn(x, fx, gate, wq, bq, subkeys, u, vt, final_w=None):
    b, t, d = x.shape
    n = b * t
    q = matmul(fx.reshape(n, d), wq, out_dtype=BF16, bias=bq)
    cut, e1, r2, e2 = peer_route(q, subkeys)
    y = peer_experts(fx.reshape(n, d).T, u, vt, cut, e1, r2, e2)
    return resid(x, y.reshape(b, t, d), gate, final_w=final_w)


def _out_proj(a, w_out, x, gate):
    b, t, d = x.shape
    g = gate if gate.shape[0] == b else jnp.broadcast_to(gate, (b, d))
    y = matmul(a.reshape(b * t, a.shape[-1]), w_out, out_dtype=F32, res=x.reshape(b * t, d), gate=g,
               rows_per_gate=t, tm=min(512, t))
    return y.reshape(b, t, d)


def kernel(x, c, ctx, c_ctx, ada_w, ada_b, norm_mix_w, norm_ffn_w, ab_w_in, ab_w_out, dn_conv_w, dn_a_log,
           dn_dt_bias, dn_norm_w, diff_lambda, diff_norm_w, gqa_w_in, gqa_w_out, gqa_sink, peer_wq, peer_bq,
           peer_subkeys, peer_u, peer_v, final_norm_w):
    b, t, d = x.shape
    lctx = ctx.shape[1]
    depth = ada_w.shape[0]
    n = b * t
    nctx = b * lctx

    cond = jnp.zeros((16, d), F32).at[:b].set(c).at[b].set(c_ctx)
    mods = []
    for l in range(depth):
        m = matmul(cond, ada_w, out_dtype=F32, bias=ada_b[l], silu_in=True, tn_cap=1024, layer=l)
        mods.append(m.reshape(16, 6, d))
    mod_x = lambda l, i: mods[l][:b, i]
    mod_c = lambda l, i: mods[l][b:b + 1, i]

    w_in = ab_w_in[0]
    qkv_w = DN_HEADS * HEAD_DIM * 3
    z_w = DN_HEADS * HEAD_DIM
    small = 4 * DN_HEADS
    off = qkv_w + z_w
    w_dn = jnp.concatenate([w_in[:, :off], w_in[:, off:off + small],
                            jnp.zeros((d, LANES - small), F32)], axis=1).astype(BF16)
    off += small
    dq_w = DIFF_HEADS * HEAD_DIM
    w_dqk = w_in[:, off:off + 2 * dq_w].astype(BF16)
    w_dv = w_in[:, off + 2 * dq_w:].astype(BF16)
    par = jnp.zeros((8, LANES), F32)
    par = par.at[0, 2 * DN_HEADS:4 * DN_HEADS].set(dn_dt_bias[0].reshape(-1))
    par = par.at[1, 2 * DN_HEADS:4 * DN_HEADS].set(dn_a_log[0].reshape(-1))
    lam_init = 0.8 - 0.6 * math.exp(-0.3 * 0)

    hx = normmod(x, norm_mix_w[0], mod_x(0, 0), mod_x(0, 1))
    hc = normmod(ctx, norm_mix_w[0], mod_c(0, 0), mod_c(0, 1))
    hx2 = hx.reshape(n, d)
    hc2 = hc.reshape(nctx, d)

    p_x = matmul(hx2, w_dn, out_dtype=F32).reshape(b, t, -1)
    p_c = matmul(hc2, w_dn, out_dtype=F32).reshape(b, lctx, -1)
    qkv_c = dn_conv(p_c, dn_conv_w[0])
    qkv_x = dn_conv(p_x, dn_conv_w[0])
    s0 = jnp.zeros((b, 2, DN_HEADS, HEAD_DIM, HEAD_DIM), F32)
    of_c, ob_c, s_c = dn_scan(qkv_c, p_c, par, s0)
    of_x, ob_x, _ = dn_scan(qkv_x, p_x, par, s_c)
    dn_x = dn_out(of_x, ob_x, p_x, dn_norm_w[0])
    dn_c = dn_out(of_c, ob_c, p_c, dn_norm_w[0])

    qk_x = matmul(hx2, w_dqk, out_dtype=F32).reshape(b, t, -1)
    qk_c = matmul(hc2, w_dqk, out_dtype=F32).reshape(b, lctx, -1)
    v_x = matmul(hx2, w_dv, out_dtype=BF16).reshape(b, t, -1)
    v_c = matmul(hc2, w_dv, out_dtype=BF16).reshape(b, lctx, -1)
    cos_d, sin_d = rope_tables(t, DIFF_DQK, DIFF_DQK // 4)
    one_c = jnp.ones((lctx, LANES), F32)
    zero_c = jnp.zeros((lctx, LANES), F32)
    dscale = DIFF_DQK ** -0.5 * math.log2(math.e)
    q1_x, q2_x = rope(qk_x, 0, dq_w, cos_d, sin_d, half=DIFF_DQK // 4, scale=dscale, split=True)
    k_x = rope(qk_x, 1, dq_w, cos_d, sin_d, half=DIFF_DQK // 4)
    q1_c, q2_c = rope(qk_c, 0, dq_w, one_c, zero_c, half=DIFF_DQK // 4, scale=dscale, split=True)
    k_c = rope(qk_c, 1, dq_w, one_c, zero_c, half=DIFF_DQK // 4)
    k_all = jnp.concatenate([k_x, k_c], axis=1)
    v_all = jnp.concatenate([v_x, v_c], axis=1)
    d_x = diff_attention(q1_x, q2_x, k_all, v_all, diff_lambda[0], diff_norm_w[0], lam_init)
    d_c = diff_attention(q1_c, q2_c, k_c, v_c, diff_lambda[0], diff_norm_w[0], lam_init)

    w_out = ab_w_out[0].astype(BF16)
    x = _out_proj(jnp.concatenate([dn_x, d_x], axis=-1), w_out, x, mod_x(0, 2))
    ctx = _out_proj(jnp.concatenate([dn_c, d_c], axis=-1), w_out, ctx, mod_c(0, 2))

    wq = peer_wq[0].astype(BF16)
    sub = peer_subkeys[0].astype(BF16)
    u_tab = peer_u[0].astype(BF16)
    vt_tab = peer_v[0].T.astype(BF16)
    fx = normmod(x, norm_ffn_w[0], mod_x(0, 3), mod_x(0, 4))
    x = _peer_ffn(x, fx, mod_x(0, 5), wq, peer_bq[0], sub, u_tab, vt_tab)
    fc = normmod(ctx, norm_ffn_w[0], mod_c(0, 3), mod_c(0, 4))
    ctx = _peer_ffn(ctx, fc, mod_c(0, 5), wq, peer_bq[0], sub, u_tab, vt_tab)

    gq_w = GQA_Q_HEADS * HEAD_DIM
    gkv_w = GQA_KV_HEADS * HEAD_DIM
    w_g = gqa_w_in[0]
    w_gqk = w_g[:, :gq_w + gkv_w].astype(BF16)
    w_gv = w_g[:, gq_w + gkv_w:].astype(BF16)
    w_gkv = w_g[:, gq_w:].astype(BF16)
    hx = normmod(x, norm_mix_w[1], mod_x(1, 0), mod_x(1, 1))
    hc = normmod(ctx, norm_mix_w[1], mod_c(1, 0), mod_c(1, 1))
    hx2 = hx.reshape(n, d)
    qk = matmul(hx2, w_gqk, out_dtype=F32, tn_cap=1280).reshape(b, t, -1)
    gv = matmul(hx2, w_gv, out_dtype=BF16).reshape(b, t, -1)
    kvx = matmul(hc.reshape(nctx, d), w_gkv, out_dtype=BF16).reshape(b, lctx, -1)
    cos_g, sin_g = rope_tables(t, HEAD_DIM, HEAD_DIM // 4)
    gq = rope(qk, 0, gq_w, cos_g, sin_g, half=HEAD_DIM // 4, scale=HEAD_DIM ** -0.5)
    gk = rope(qk, gq_w // gkv_w, gkv_w, cos_g, sin_g, half=HEAD_DIM // 4)
    att = window_attention(gq, gk, gv, kvx, gqa_sink[0])
    x = _out_proj(att, gqa_w_out[0].astype(BF16), x, mod_x(1, 2))

    wq = peer_wq[1].astype(BF16)
    sub = peer_subkeys[1].astype(BF16)
    u_tab = peer_u[1].astype(BF16)
    vt_tab = peer_v[1].T.astype(BF16)
    fx = normmod(x, norm_ffn_w[1], mod_x(1, 3), mod_x(1, 4))
    return _peer_ffn(x, fx, mod_x(1, 5), wq, peer_bq[1], sub, u_tab, vt_tab, final_w=final_norm_w)
```

```python
import functools
import math

import jax
import jax.numpy as jnp
import numpy as np
from jax import lax
from jax.experimental import pallas as pl
from jax.experimental.pallas import tpu as pltpu

F32 = jnp.float32
BF16 = jnp.bfloat16

LANES = 128
HEAD_DIM = 128
NORM_EPS = 1e-6
ROPE_THETA = 10000.0
GRID_W = 64
DN_HEADS = 8
DN_CHUNK = 64
DN_CONV = 5
DIFF_HEADS = 8
DIFF_DQK = 64
GQA_Q_HEADS = 16
GQA_KV_HEADS = 4
GQA_GROUP = GQA_Q_HEADS // GQA_KV_HEADS
WINDOW = 128
PEER_HEADS = 8
PEER_NKEYS = 128
PEER_TOPK = 16
NEG_BIG = -1e30
VMEM_LIMIT = 56 * 1024 * 1024


def _params(sem):
    return pltpu.CompilerParams(dimension_semantics=sem, vmem_limit_bytes=VMEM_LIMIT)


def _dot(a, b):
    return jnp.dot(a.astype(BF16), b.astype(BF16), preferred_element_type=F32)


def _dot_nt(a, b):
    return lax.dot_general(a.astype(BF16), b.astype(BF16), (((1,), (1,)), ((), ())),
                           preferred_element_type=F32)


def _sigmoid(x):
    return 1.0 / (1.0 + jnp.exp(-x))


def _normmod_kernel(x_ref, w_ref, shift_ref, scale_ref, o_ref):
    x = x_ref[...]
    y = x * lax.rsqrt(jnp.mean(x * x, axis=-1, keepdims=True) + NORM_EPS)
    y = y * w_ref[...]
    o_ref[...] = (y * (1.0 + scale_ref[...]) + shift_ref[...]).astype(o_ref.dtype)


def normmod(x, w, shift, scale, rows=512):
    b, t, d = x.shape
    tr = min(rows, t)
    bm = shift.shape[0]
    mod_map = (lambda i, j: (i, 0, 0)) if bm == b else (lambda i, j: (0, 0, 0))
    return pl.pallas_call(
        _normmod_kernel,
        grid=(b, t // tr),
        in_specs=[pl.BlockSpec((None, tr, d), lambda i, j: (i, j, 0)),
                  pl.BlockSpec((1, d), lambda i, j: (0, 0)),
                  pl.BlockSpec((None, 1, d), mod_map),
                  pl.BlockSpec((None, 1, d), mod_map)],
        out_specs=pl.BlockSpec((None, tr, d), lambda i, j: (i, j, 0)),
        out_shape=jax.ShapeDtypeStruct((b, t, d), BF16),
        compiler_params=_params(("parallel", "parallel")),
        name="normmod",
    )(x, w.reshape(1, d), shift.reshape(bm, 1, d), scale.reshape(bm, 1, d))


def _mm_kernel(*refs, silu_in, has_bias, has_res):
    a_ref, w_ref = refs[0], refs[1]
    pos = 2
    a = a_ref[...]
    if silu_in:
        a = a * _sigmoid(a)
    acc = _dot(a, w_ref[...])
    if has_bias:
        acc = acc + refs[pos][...]
        pos += 1
    if has_res:
        acc = refs[pos][...] + refs[pos + 1][...] * acc
        pos += 2
    o_ref = refs[pos]
    o_ref[...] = acc.astype(o_ref.dtype)


def _pick_tn(n, cap):
    best = LANES
    for cand in range(LANES, min(n, cap) + 1, LANES):
        if n % cand == 0:
            best = cand
    return best


def matmul(a, w, *, out_dtype, bias=None, res=None, gate=None, rows_per_gate=None,
           silu_in=False, tm=512, tn_cap=2048, layer=0):
    m, k = a.shape
    n = w.shape[-1]
    tm = min(tm, m)
    tn = _pick_tn(n, tn_cap)
    if w.ndim == 3:
        w_spec = pl.BlockSpec((None, k, tn), lambda j, i: (layer, 0, j))
    else:
        w_spec = pl.BlockSpec((k, tn), lambda j, i: (0, j))
    in_specs = [pl.BlockSpec((tm, k), lambda j, i: (i, 0)), w_spec]
    args = [a, w]
    if bias is not None:
        in_specs.append(pl.BlockSpec((1, tn), lambda j, i: (0, j)))
        args.append(bias.reshape(1, n).astype(F32))
    if res is not None:
        blocks_per_gate = rows_per_gate // tm
        in_specs.append(pl.BlockSpec((tm, tn), lambda j, i: (i, j)))
        in_specs.append(pl.BlockSpec((None, 1, tn), lambda j, i: (i // blocks_per_gate, 0, j)))
        args += [res, gate.reshape(gate.shape[0], 1, n)]
    kern = functools.partial(_mm_kernel, silu_in=silu_in, has_bias=bias is not None,
                             has_res=res is not None)
    return pl.pallas_call(
        kern,
        grid=(n // tn, m // tm),
        in_specs=in_specs,
        out_specs=pl.BlockSpec((tm, tn), lambda j, i: (i, j)),
        out_shape=jax.ShapeDtypeStruct((m, n), out_dtype),
        compiler_params=_params(("parallel", "parallel")),
        name="matmul",
    )(*args)


def _conv_kernel(prev_ref, cur_ref, next_ref, w_ref, o_ref, *, tt, nheads_blk):
    i = pl.program_id(1)
    grp = pl.program_id(2)
    last = pl.num_programs(1) - 1
    prev = prev_ref[...] * (i > 0).astype(F32)
    nxt = next_ref[...] * (i < last).astype(F32)
    xx = jnp.concatenate([prev, cur_ref[...], nxt], axis=0)
    rows = tt + 16
    pad = (DN_CONV - 1) // 2
    y = None
    for k in range(DN_CONV):
        shift = (pad - k) % rows
        z = xx if shift == 0 else pltpu.roll(xx, shift, 0)
        term = z[8:8 + tt] * w_ref[k:k + 1, :]
        y = term if y is None else y + term
    y = y * _sigmoid(y)
    qscale = jnp.where(grp == 0, HEAD_DIM ** -0.5, 1.0).astype(F32)
    for h in range(nheads_blk):
        yh = y[:, h * HEAD_DIM:(h + 1) * HEAD_DIM]
        ss = jnp.sum(yh * yh, axis=-1, keepdims=True)
        fac = jnp.where(grp == 2, 1.0, lax.rsqrt(ss + NORM_EPS) * qscale)
        o_ref[:, h * HEAD_DIM:(h + 1) * HEAD_DIM] = yh * fac


def dn_conv(p, conv_w, tt=256):
    b, t, _ = p.shape
    tt = min(tt, t)
    cb = DN_HEADS * HEAD_DIM
    n8 = t // 8
    kern = functools.partial(_conv_kernel, tt=tt, nheads_blk=DN_HEADS)
    return pl.pallas_call(
        kern,
        grid=(b, t // tt, 3),
        in_specs=[pl.BlockSpec((None, 8, cb), lambda bi, i, g: (bi, jnp.maximum(i * (tt // 8) - 1, 0), g)),
                  pl.BlockSpec((None, tt, cb), lambda bi, i, g: (bi, i, g)),
                  pl.BlockSpec((None, 8, cb), lambda bi, i, g: (bi, jnp.minimum((i + 1) * (tt // 8), n8 - 1), g)),
                  pl.BlockSpec((DN_CONV, cb), lambda bi, i, g: (0, g))],
        out_specs=pl.BlockSpec((None, tt, cb), lambda bi, i, g: (bi, i, g)),
        out_shape=jax.ShapeDtypeStruct((b, t, 3 * cb), F32),
        compiler_params=_params(("parallel", "parallel", "parallel")),
        name="dn_conv",
    )(p, p, p, conv_w)


def _split3(x):
    hi = x.astype(BF16)
    r1 = x - hi.astype(F32)
    mid = r1.astype(BF16)
    lo = (r1 - mid.astype(F32)).astype(BF16)
    return hi, mid, lo


def _dn_chains(direction, q_ref, k_ref, v_ref, p_ref, par_ref, o_ref, s_scr):
    c = DN_CHUNK
    ii = lax.broadcasted_iota(jnp.int32, (c, c), 0)
    jj = lax.broadcasted_iota(jnp.int32, (c, c), 1)
    incl = (ii >= jj) if direction == 0 else (ii <= jj)
    strict = (ii > jj) if direction == 0 else (ii < jj)

    raw = p_ref[...]
    beta_all = _sigmoid(raw)
    xa = raw + par_ref[0:1, :]
    softplus = jnp.maximum(xa, 0.0) + jnp.log1p(jnp.exp(-jnp.abs(xa)))
    g_all = -jnp.exp(par_ref[1:2, :]) * softplus
    inclb = incl.astype(BF16)
    hi, mid, lo = _split3(g_all)
    gc_all = (jnp.dot(inclb, hi, preferred_element_type=F32)
              + jnp.dot(inclb, mid, preferred_element_type=F32)
              + jnp.dot(inclb, lo, preferred_element_type=F32))
    gtot_all = jnp.sum(g_all, axis=0, keepdims=True)
    gc_t = gc_all.T

    chains = []
    for h in range(DN_HEADS):
        ib = direction * DN_HEADS + h
        ig = 2 * DN_HEADS + ib
        sl = slice(h * HEAD_DIM, (h + 1) * HEAD_DIM)
        gc_c = gc_all[:, ig:ig + 1]
        chains.append(dict(
            incl=incl, strict=strict, beta=beta_all[:, ib:ib + 1], gc_c=gc_c,
            decay=jnp.exp(jnp.where(incl, gc_c - gc_t[ig:ig + 1, :], -jnp.inf)),
            gtot=gtot_all[:, ig:ig + 1], q=q_ref[:, sl], k=k_ref[:, sl], v=v_ref[:, sl],
            o_ref=o_ref, sl=sl, state_idx=(direction, h)))
    return chains


def _dn_step(chains, s_scr):
    c = DN_CHUNK
    ii = lax.broadcasted_iota(jnp.int32, (c, c), 0)
    jj = lax.broadcasted_iota(jnp.int32, (c, c), 1)
    eye = (ii == jj).astype(F32)
    for ch in chains:
        ch["kb"] = ch["k"] * ch["beta"]
        ch["kq"] = _dot_nt(jnp.concatenate([ch["kb"], ch["q"]], axis=0), ch["k"])
    for ch in chains:
        a = jnp.where(ch["strict"], ch["kq"][:c] * ch["decay"], 0.0)
        ch["qk"] = jnp.where(ch["incl"], ch["kq"][c:] * ch["decay"], 0.0)
        ch["inv"] = eye - a
        ch["a_pow"] = a
    for _ in range(int(math.log2(c)) - 1):
        for ch in chains:
            ch["a_pow"] = _dot(ch["a_pow"], ch["a_pow"])
        for ch in chains:
            ch["inv"] = _dot(ch["inv"], eye + ch["a_pow"])
    for ch in chains:
        egc = jnp.exp(ch["gc_c"])
        ch["uw"] = _dot(ch["inv"], jnp.concatenate([ch["v"] * ch["beta"], ch["kb"] * egc], axis=1))
        ch["q_dec"] = ch["q"] * egc
    for ch in chains:
        ch["state"] = s_scr[ch["state_idx"]]
        w = ch["uw"][:, HEAD_DIM:]
        ch["ws"] = _dot(jnp.concatenate([w, ch["q_dec"]], axis=0), ch["state"])
    for ch in chains:
        ch["v_new"] = ch["uw"][:, :HEAD_DIM] - ch["ws"][:c]
        ch["o_ref"][:, ch["sl"]] = ch["ws"][c:] + _dot(ch["qk"], ch["v_new"])
    for ch in chains:
        k_dec = ch["k"] * jnp.exp(ch["gtot"] - ch["gc_c"])
        s_scr[ch["state_idx"]] = ch["state"] * jnp.exp(ch["gtot"]) + _dot(k_dec.T, ch["v_new"])


def _dn_scan_kernel(qf_ref, kf_ref, vf_ref, pf_ref, qb_ref, kb_ref, vb_ref, pb_ref, par_ref, s0_ref,
                    of_ref, ob_ref, sfin_ref, s_scr):
    n = pl.program_id(1)

    @pl.when(n == 0)
    def _():
        s_scr[...] = s0_ref[...]

    chains = (_dn_chains(0, qf_ref, kf_ref, vf_ref, pf_ref, par_ref, of_ref, s_scr)
              + _dn_chains(1, qb_ref, kb_ref, vb_ref, pb_ref, par_ref, ob_ref, s_scr))
    _dn_step(chains, s_scr)

    @pl.when(n == pl.num_programs(1) - 1)
    def _():
        sfin_ref[...] = s_scr[...]


def dn_scan(qkv, p, par, s0):
    b, t, _ = qkv.shape
    c = DN_CHUNK
    nc = t // c
    w = DN_HEADS * HEAD_DIM
    fwd = lambda col: (lambda bi, n: (bi, n, col))
    bwd = lambda col: (lambda bi, n: (bi, nc - 1 - n, col))
    blk = lambda f: pl.BlockSpec((None, c, w), f)
    small = lambda f: pl.BlockSpec((None, c, LANES), f)
    state_spec = pl.BlockSpec((None, 2, DN_HEADS, HEAD_DIM, HEAD_DIM), lambda bi, n: (bi, 0, 0, 0, 0))
    o_shape = jax.ShapeDtypeStruct((b, t, w), F32)
    return pl.pallas_call(
        _dn_scan_kernel,
        grid=(b, nc),
        in_specs=[blk(fwd(0)), blk(fwd(1)), blk(fwd(2)), small(fwd(32)),
                  blk(bwd(0)), blk(bwd(1)), blk(bwd(2)), small(bwd(32)),
                  pl.BlockSpec((8, LANES), lambda bi, n: (0, 0)), state_spec],
        out_specs=[blk(fwd(0)), blk(bwd(0)), state_spec],
        out_shape=[o_shape, o_shape, jax.ShapeDtypeStruct(s0.shape, F32)],
        scratch_shapes=[pltpu.VMEM((2, DN_HEADS, HEAD_DIM, HEAD_DIM), F32)],
        compiler_params=_params(("parallel", "arbitrary")),
        name="dn_scan",
    )(qkv, qkv, qkv, p, qkv, qkv, qkv, p, par, s0)


def _dn_out_kernel(of_ref, ob_ref, z_ref, w_ref, y_ref):
    o = of_ref[...] + ob_ref[...]
    z = z_ref[...]
    for h in range(DN_HEADS):
        sl = slice(h * HEAD_DIM, (h + 1) * HEAD_DIM)
        oh = o[:, sl]
        yh = oh * lax.rsqrt(jnp.mean(oh * oh, axis=-1, keepdims=True) + NORM_EPS) * w_ref[...]
        zh = z[:, sl]
        y_ref[:, sl] = (yh * (zh * _sigmoid(zh))).astype(y_ref.dtype)


def dn_out(o_f, o_b, p, norm_w, tr=256):
    b, t, w = o_f.shape
    tr = min(tr, t)
    return pl.pallas_call(
        _dn_out_kernel,
        grid=(b, t // tr),
        in_specs=[pl.BlockSpec((None, tr, w), lambda bi, i: (bi, i, 0)),
                  pl.BlockSpec((None, tr, w), lambda bi, i: (bi, i, 0)),
                  pl.BlockSpec((None, tr, w), lambda bi, i: (bi, i, 3)),
                  pl.BlockSpec((1, HEAD_DIM), lambda bi, i: (0, 0))],
        out_specs=pl.BlockSpec((None, tr, w), lambda bi, i: (bi, i, 0)),
        out_shape=jax.ShapeDtypeStruct((b, t, w), BF16),
        compiler_params=_params(("parallel", "parallel")),
        name="dn_out",
    )(o_f, o_b, p, norm_w.reshape(1, HEAD_DIM))


def _rope_kernel(x_ref, cos_ref, sin_ref, *o_refs, half, scale, nheads, split):
    lane = lax.broadcasted_iota(jnp.int32, (1, LANES), 1)
    first = (lane % (2 * half)) < half
    cos = cos_ref[...]
    sin = sin_ref[...]
    for h in range(nheads):
        sl = slice(h * LANES, (h + 1) * LANES)
        x = x_ref[:, sl]
        partner = jnp.where(first, pltpu.roll(x, LANES - half, 1), pltpu.roll(x, half, 1))
        y = (x * cos + partner * sin) * scale
        if split:
            o_refs[0][:, sl] = jnp.where(lane < LANES // 2, y, 0.0).astype(BF16)
            o_refs[1][:, sl] = jnp.where(lane >= LANES // 2, y, 0.0).astype(BF16)
        else:
            o_refs[0][:, sl] = y.astype(BF16)


def rope(x, col_block, width, cos, sin, *, half, scale=1.0, split=False, tr=512):
    b, t, _ = x.shape
    tr = min(tr, t)
    nheads = width // LANES
    nout = 2 if split else 1
    kern = functools.partial(_rope_kernel, half=half, scale=scale, nheads=nheads, split=split)
    out = pl.pallas_call(
        kern,
        grid=(b, t // tr),
        in_specs=[pl.BlockSpec((None, tr, width), lambda bi, i: (bi, i, col_block)),
                  pl.BlockSpec((tr, LANES), lambda bi, i: (i, 0)),
                  pl.BlockSpec((tr, LANES), lambda bi, i: (i, 0))],
        out_specs=[pl.BlockSpec((None, tr, width), lambda bi, i: (bi, i, 0))] * nout,
        out_shape=[jax.ShapeDtypeStruct((b, t, width), BF16)] * nout,
        compiler_params=_params(("parallel", "parallel")),
        name="rope",
    )(x, cos, sin)
    return out if split else out[0]


def rope_tables(t, block, half):
    pos = jnp.arange(t, dtype=jnp.int32)
    rows = (pos // GRID_W).astype(F32)
    cols = (pos % GRID_W).astype(F32)
    lane = np.arange(LANES)
    j = lane % block
    use_col = (j // (2 * half)) == 1
    i = j % (2 * half)
    inv = (ROPE_THETA ** (-jnp.arange(half, dtype=F32) / half))[i % half]
    p = jnp.where(jnp.asarray(use_col)[None, :], cols[:, None], rows[:, None])
    ang = p * inv[None, :]
    sign = jnp.asarray(np.where(i < half, -1.0, 1.0).astype(np.float32))[None, :]
    return jnp.cos(ang), jnp.sin(ang) * sign


def _diff_kernel(q1_ref, q2_ref, k_ref, vt_ref, lam_ref, nw_ref, o_ref, acc1, acc2, s_scr,
                 *, lam_init, tk, nkv):
    tq = q1_ref.shape[0]
    acc1[...] = jnp.zeros_like(acc1)
    acc2[...] = jnp.zeros_like(acc2)

    def scores(c, slot):
        kc = k_ref[pl.ds(pl.multiple_of(c * tk, tk), tk), :]
        s_scr[slot, 0] = _dot_nt(kc, q1_ref[...])
        s_scr[slot, 1] = _dot_nt(kc, q2_ref[...])

    def update(s, vtc, m, l, acc):
        m_new = jnp.maximum(m, jnp.max(s, axis=0, keepdims=True))
        alpha = jnp.exp2(m - m_new)
        p = jnp.exp2(s - m_new)
        l_new = l * alpha + jnp.sum(p, axis=0, keepdims=True)
        acc[...] = acc[...] * alpha + _dot(vtc, p)
        return m_new, l_new

    def softmax(c, slot, carry):
        m1, l1, m2, l2 = carry
        vtc = vt_ref[c]
        m1, l1 = update(s_scr[slot, 0], vtc, m1, l1, acc1)
        m2, l2 = update(s_scr[slot, 1], vtc, m2, l2, acc2)
        return m1, l1, m2, l2

    def body(i, carry):
        c = 2 * i
        scores(c + 1, 1)
        carry = softmax(c, 0, carry)
        scores(c + 2, 0)
        return softmax(c + 1, 1, carry)

    neg = jnp.full((1, tq), -jnp.inf, F32)
    zero = jnp.zeros((1, tq), F32)
    scores(0, 0)
    pairs = (nkv - 1) // 2
    carry = lax.fori_loop(0, pairs, body, (neg, zero, neg, zero))
    if nkv % 2 == 0:
        scores(nkv - 1, 1)
        carry = softmax(nkv - 2, 0, carry)
        carry = softmax(nkv - 1, 1, carry)
    else:
        carry = softmax(nkv - 1, 0, carry)
    m1, l1, m2, l2 = carry

    lv = lam_ref[...]
    s01 = jnp.sum(lv[0:1] * lv[1:2], axis=-1, keepdims=True)
    s23 = jnp.sum(lv[2:3] * lv[3:4], axis=-1, keepdims=True)
    lam = jnp.exp(s01) - jnp.exp(s23) + lam_init
    o = acc1[...] / l1 - lam * (acc2[...] / l2)
    y = o * lax.rsqrt(jnp.mean(o * o, axis=0, keepdims=True) + NORM_EPS)
    o_ref[...] = (y.T * nw_ref[...] * (1.0 - lam_init)).astype(o_ref.dtype)


def diff_attention(q1, q2, k, v, lam_vec, norm_w, lam_init, tq=1024, tk=768):
    b, tqa, w = q1.shape
    tka = k.shape[1]
    tq = min(tq, tqa)
    tk = min(tk, tka)
    nkv = tka // tk
    nh = w // HEAD_DIM
    vt = v.reshape(b, nkv, tk, nh, HEAD_DIM).transpose(0, 3, 1, 4, 2)
    kern = functools.partial(_diff_kernel, lam_init=lam_init, tk=tk, nkv=nkv)
    return pl.pallas_call(
        kern,
        grid=(b, nh, tqa // tq),
        in_specs=[pl.BlockSpec((None, tq, HEAD_DIM), lambda bi, h, i: (bi, i, h)),
                  pl.BlockSpec((None, tq, HEAD_DIM), lambda bi, h, i: (bi, i, h)),
                  pl.BlockSpec((None, tka, HEAD_DIM), lambda bi, h, i: (bi, 0, h)),
                  pl.BlockSpec((None, None, nkv, HEAD_DIM, tk), lambda bi, h, i: (bi, h, 0, 0, 0)),
                  pl.BlockSpec((4, DIFF_DQK), lambda bi, h, i: (0, 0)),
                  pl.BlockSpec((1, HEAD_DIM), lambda bi, h, i: (0, 0))],
        out_specs=pl.BlockSpec((None, tq, HEAD_DIM), lambda bi, h, i: (bi, i, h)),
        out_shape=jax.ShapeDtypeStruct((b, tqa, w), BF16),
        scratch_shapes=[pltpu.VMEM((HEAD_DIM, tq), F32)] * 2 + [pltpu.VMEM((2, 2, tk, tq), F32)],
        compiler_params=_params(("parallel", "parallel", "parallel")),
        name="diff_attention",
    )(q1, q2, k, vt, lam_vec, norm_w.reshape(1, HEAD_DIM))


def _win_kernel(sink_ref, q_ref, kp_ref, kc_ref, kn_ref, vp_ref, vc_ref, vn_ref, kx_ref, vx_ref, o_ref,
                *, t_total):
    kvh = pl.program_id(1)
    n = pl.program_id(2)
    wb = WINDOW
    tq = q_ref.shape[0]
    keys = jnp.concatenate([kp_ref[...], kc_ref[...], kn_ref[...], kx_ref[...]], axis=0)
    vals = jnp.concatenate([vp_ref[...], vc_ref[...], vn_ref[...], vx_ref[...]], axis=0)
    nk = keys.shape[0]
    nlocal = tq + 2 * wb
    i = lax.broadcasted_iota(jnp.int32, (tq, nk), 0)
    j = lax.broadcasted_iota(jnp.int32, (tq, nk), 1)
    kpos = n * tq - wb + j
    valid = (jnp.abs(j - wb - i) <= WINDOW) & (kpos >= 0) & (kpos < t_total)
    valid = valid | (j >= nlocal)
    for g in range(GQA_GROUP):
        sl = slice(g * HEAD_DIM, (g + 1) * HEAD_DIM)
        sink = sink_ref[kvh * GQA_GROUP + g]
        s = jnp.where(valid, _dot_nt(q_ref[:, sl], keys), -jnp.inf)
        m = jnp.maximum(jnp.max(s, axis=-1, keepdims=True), sink)
        p = jnp.exp(s - m)
        denom = jnp.sum(p, axis=-1, keepdims=True) + jnp.exp(sink - m)
        o_ref[:, sl] = (_dot(p, vals) / denom).astype(o_ref.dtype)


def window_attention(q, k, v, kvx, sink, tq=512):
    b, t, _ = q.shape
    lctx = kvx.shape[1]
    wb = WINDOW
    tq = min(tq, t)
    per = tq // wb
    nb = t // wb
    qw = GQA_GROUP * HEAD_DIM
    kern = functools.partial(_win_kernel, t_total=t)
    prev = lambda bi, h, n: (bi, jnp.maximum(n * per - 1, 0), h)
    cur = lambda bi, h, n: (bi, n, h)
    nxt = lambda bi, h, n: (bi, jnp.minimum((n + 1) * per, nb - 1), h)
    edge_spec = lambda f: pl.BlockSpec((None, wb, HEAD_DIM), f)
    main_spec = pl.BlockSpec((None, tq, HEAD_DIM), cur)
    return pl.pallas_call(
        kern,
        grid=(b, GQA_KV_HEADS, t // tq),
        in_specs=[pl.BlockSpec(memory_space=pltpu.SMEM),
                  pl.BlockSpec((None, tq, qw), cur),
                  edge_spec(prev), main_spec, edge_spec(nxt),
                  edge_spec(prev), main_spec, edge_spec(nxt),
                  pl.BlockSpec((None, lctx, HEAD_DIM), lambda bi, h, n: (bi, 0, h)),
                  pl.BlockSpec((None, lctx, HEAD_DIM), lambda bi, h, n: (bi, 0, GQA_KV_HEADS + h))],
        out_specs=pl.BlockSpec((None, tq, qw), cur),
        out_shape=jax.ShapeDtypeStruct(q.shape, BF16),
        compiler_params=_params(("parallel", "parallel", "parallel")),
        name="window_attention",
    )(sink, q, k, k, k, v, v, v, kvx, kvx)


RANK_NONE = 255.0
RANK_CODE_BASE = 1e30
RANK_CODE_STEP = 1e28


def _top_values(s, count, want_rank=False):
    vals = []
    work = s
    for r in range(count):
        m = jnp.max(work, axis=0, keepdims=True)
        vals.append(m)
        work = jnp.where(work == m, -(RANK_CODE_BASE + r * RANK_CODE_STEP), work)
    if not want_rank:
        return vals
    rank = jnp.where(work < -0.5 * RANK_CODE_BASE,
                     jnp.round((-work - RANK_CODE_BASE) * (1.0 / RANK_CODE_STEP)), RANK_NONE)
    return vals, rank


def _route_kernel(q_ref, sub_ref, cut_ref, e1_ref, r2_ref, e2_ref):
    kk = PEER_TOPK
    tb = q_ref.shape[0]
    row = lax.broadcasted_iota(jnp.int32, (kk, tb), 0)
    for h in range(PEER_HEADS):
        scores, tops, ranks = [], [], []
        for p in range(2):
            c0 = (h * 2 + p) * PEER_NKEYS
            s = _dot_nt(sub_ref[h, p], q_ref[:, c0:c0 + PEER_NKEYS])
            vals, rank = _top_values(s, kk, want_rank=True)
            scores.append(s)
            tops.append(vals)
            ranks.append(rank)
        v1 = jnp.zeros((kk, tb), F32)
        v2 = jnp.zeros((kk, tb), F32)
        for i in range(kk):
            v1 = jnp.where(row == i, tops[0][i], v1)
            v2 = jnp.where(row == i, tops[1][i], v2)
        mid = jnp.where(row[:8] >= 4, v1[:8], -jnp.inf)
        cand = jnp.concatenate([tops[0][0] + v2] + [tops[0][i] + v2[:8] for i in range(1, 4)]
                               + [mid + tops[1][j] for j in range(3)] + [v1[8:] + tops[1][0]], axis=0)
        thr = _top_values(cand, kk)[kk - 1]
        top = tops[0][0] + tops[1][0]
        z = jnp.sum(jnp.where(cand >= thr, jnp.exp(cand - top), 0.0), axis=0, keepdims=True)
        gdt = r2_ref.dtype
        rank1 = ranks[0].astype(gdt)
        cut = jnp.zeros(rank1.shape, gdt)
        for i in range(kk):
            count = jnp.sum(((tops[0][i] + v2) >= thr).astype(F32), axis=0, keepdims=True)
            cut = jnp.where(rank1 == i, count.astype(gdt), cut)
        in1 = ranks[0] < float(kk)
        in2 = ranks[1] < float(kk)
        cut_ref[h] = cut.astype(F32)
        e1_ref[h] = jnp.where(in1, jnp.exp(scores[0] - tops[0][0]), 0.0)
        r2_ref[h] = ranks[1].astype(r2_ref.dtype)
        e2_ref[h] = jnp.where(in2, jnp.exp(scores[1] - tops[1][0]) / z, 0.0).astype(e2_ref.dtype)


def peer_route(q, subkeys, tb=128):
    n = q.shape[0]
    tb = min(tb, n)
    hh = PEER_HEADS
    tab32 = jax.ShapeDtypeStruct((hh, PEER_NKEYS, n), F32)
    tab16 = jax.ShapeDtypeStruct((hh, PEER_NKEYS, n), BF16)
    tab_spec = pl.BlockSpec((hh, PEER_NKEYS, tb), lambda i: (0, 0, i))
    return pl.pallas_call(
        _route_kernel,
        grid=(n // tb,),
        in_specs=[pl.BlockSpec((tb, q.shape[1]), lambda i: (i, 0)),
                  pl.BlockSpec(subkeys.shape, lambda i: (0, 0, 0, 0))],
        out_specs=[tab_spec] * 4,
        out_shape=[tab32, tab32, tab16, tab16],
        compiler_params=_params(("parallel",)),
        name="peer_route",
    )(q, subkeys)


def _gelu(x):
    return 0.5 * x * (1.0 + lax.erf(x * (2.0 ** -0.5)))


def _peer_kernel(xt_ref, u_ref, vt_ref, cut_ref, e1_ref, r2_ref, e2_ref, o_ref, acc, g_scr, h_scr, a_scr,
                 *, rows, rows_per_part):
    et = pl.program_id(1)

    @pl.when(et == 0)
    def _():
        acc[...] = jnp.zeros_like(acc)

    nk = PEER_NKEYS
    gdt = r2_ref.dtype
    nparts = rows // rows_per_part
    pk = rows_per_part * nk

    def gate_row(r):
        gate = None
        for h in range(PEER_HEADS):
            cut = cut_ref[h, r:r + 1, :].astype(gdt)
            e1 = e1_ref[h, r:r + 1, :].astype(gdt)
            g = jnp.where(r2_ref[h] < cut, e1 * e2_ref[h], jnp.zeros((), gdt))
            gate = g if gate is None else gate + g
        return gate

    def hidden(part):
        return jnp.dot(u_ref[part * pk:(part + 1) * pk, :], xt_ref[...], preferred_element_type=F32)

    def activate(ht, gate):
        return (_gelu(ht) * gate.astype(F32)).astype(BF16)

    def project(part, a):
        return jnp.dot(vt_ref[:, part * pk:(part + 1) * pk], a, preferred_element_type=F32)

    assert nparts == 2
    for r in range(rows_per_part):
        g_scr[r * nk:(r + 1) * nk, :] = gate_row(r)
    h_scr[...] = hidden(0)
    h1 = hidden(1)
    acc[...] += project(0, activate(h_scr[...], g_scr[...]))
    for r in range(rows_per_part):
        a_scr[r * nk:(r + 1) * nk, :] = activate(h1[r * nk:(r + 1) * nk, :], gate_row(rows_per_part + r))
    acc[...] += project(1, a_scr[...])

    @pl.when(et == pl.num_programs(1) - 1)
    def _():
        o_ref[...] = acc[...].T


def peer_experts(xt, u, vt, cut, e1, r2, e2, tb=512, rows=8, rows_per_part=4):
    d, n = xt.shape
    e = u.shape[0]
    tb = min(tb, n)
    te = rows * PEER_NKEYS
    hh = PEER_HEADS
    kern = functools.partial(_peer_kernel, rows=rows, rows_per_part=rows_per_part)
    row_spec = pl.BlockSpec((hh, rows, tb), lambda i, j: (0, j, i))
    full_spec = pl.BlockSpec((hh, PEER_NKEYS, tb), lambda i, j: (0, 0, i))
    return pl.pallas_call(
        kern,
        grid=(n // tb, e // te),
        in_specs=[pl.BlockSpec((d, tb), lambda i, j: (0, i)),
                  pl.BlockSpec((te, d), lambda i, j: (j, 0)),
                  pl.BlockSpec((d, te), lambda i, j: (0, j)),
                  row_spec, row_spec, full_spec, full_spec],
        out_specs=pl.BlockSpec((tb, d), lambda i, j: (i, 0)),
        out_shape=jax.ShapeDtypeStruct((n, d), F32),
        scratch_shapes=[pltpu.VMEM((d, tb), F32), pltpu.VMEM((te // 2, tb), r2.dtype),
                        pltpu.VMEM((te // 2, tb), F32), pltpu.VMEM((te // 2, tb), BF16)],
        compiler_params=_params(("parallel", "arbitrary")),
        name="peer_experts",
    )(xt, u, vt, cut, e1, r2, e2)


def _resid_kernel(x_ref, y_ref, g_ref, w_ref, o_ref, *, final):
    x = x_ref[...] + g_ref[...] * y_ref[...]
    if final:
        x = x * lax.rsqrt(jnp.mean(x * x, axis=-1, keepdims=True) + NORM_EPS) * w_ref[...]
    o_ref[...] = x


def resid(x, y, gate, final_w=None, rows=512):
    b, t, d = x.shape
    tr = min(rows, t)
    bm = gate.shape[0]
    mod_map = (lambda i, j: (i, 0, 0)) if bm == b else (lambda i, j: (0, 0, 0))
    w = jnp.ones((d,), F32) if final_w is None else final_w
    kern = functools.partial(_resid_kernel, final=final_w is not None)
    blk = pl.BlockSpec((None, tr, d), lambda i, j: (i, j, 0))
    return pl.pallas_call(
        kern,
        grid=(b, t // tr),
        in_specs=[blk, blk, pl.BlockSpec((None, 1, d), mod_map), pl.BlockSpec((1, d), lambda i, j: (0, 0))],
        out_specs=blk,
        out_shape=jax.ShapeDtypeStruct((b, t, d), F32),
        compiler_params=_params(("parallel", "parallel")),
        name="resid",
    )(x, y, gate.reshape(bm, 1, d), w.reshape(1, d))


def _peer_ffn(x, fx, gate, wq, bq, subkeys, u, vt, final_w=None):
    b, t, d = x.shape
    n = b * t
    q = matmul(fx.reshape(n, d), wq, out_dtype=BF16, bias=bq)
    cut, e1, r2, e2 = peer_route(q, subkeys)
    y = peer_experts(fx.reshape(n, d).T, u, vt, cut, e1, r2, e2)
    return resid(x, y.reshape(b, t, d), gate, final_w=final_w)


def _out_proj(a, w_out, x, gate):
    b, t, d = x.shape
    g = gate if gate.shape[0] == b else jnp.broadcast_to(gate, (b, d))
    y = matmul(a.reshape(b * t, a.shape[-1]), w_out, out_dtype=F32, res=x.reshape(b * t, d), gate=g,
               rows_per_gate=t, tm=min(512, t))
    return y.reshape(b, t, d)


def kernel(x, c, ctx, c_ctx, ada_w, ada_b, norm_mix_w, norm_ffn_w, ab_w_in, ab_w_out, dn_conv_w, dn_a_log,
           dn_dt_bias, dn_norm_w, diff_lambda, diff_norm_w, gqa_w_in, gqa_w_out, gqa_sink, peer_wq, peer_bq,
           peer_subkeys, peer_u, peer_v, final_norm_w):
    b, t, d = x.shape
    lctx = ctx.shape[1]
    depth = ada_w.shape[0]
    n = b * t
    nctx = b * lctx

    cond = jnp.zeros((16, d), F32).at[:b].set(c).at[b].set(c_ctx)
    mods = []
    for l in range(depth):
        m = matmul(cond, ada_w, out_dtype=F32, bias=ada_b[l], silu_in=True, tn_cap=1024, layer=l)
        mods.append(m.reshape(16, 6, d))
    mod_x = lambda l, i: mods[l][:b, i]
    mod_c = lambda l, i: mods[l][b:b + 1, i]

    w_in = ab_w_in[0]
    qkv_w = DN_HEADS * HEAD_DIM * 3
    z_w = DN_HEADS * HEAD_DIM
    small = 4 * DN_HEADS
    off = qkv_w + z_w
    w_dn = jnp.concatenate([w_in[:, :off], w_in[:, off:off + small],
                            jnp.zeros((d, LANES - small), F32)], axis=1).astype(BF16)
    off += small
    dq_w = DIFF_HEADS * HEAD_DIM
    w_dqk = w_in[:, off:off + 2 * dq_w].astype(BF16)
    w_dv = w_in[:, off + 2 * dq_w:].astype(BF16)
    par = jnp.zeros((8, LANES), F32)
    par = par.at[0, 2 * DN_HEADS:4 * DN_HEADS].set(dn_dt_bias[0].reshape(-1))
    par = par.at[1, 2 * DN_HEADS:4 * DN_HEADS].set(dn_a_log[0].reshape(-1))
    lam_init = 0.8 - 0.6 * math.exp(-0.3 * 0)

    hx = normmod(x, norm_mix_w[0], mod_x(0, 0), mod_x(0, 1))
    hc = normmod(ctx, norm_mix_w[0], mod_c(0, 0), mod_c(0, 1))
    hx2 = hx.reshape(n, d)
    hc2 = hc.reshape(nctx, d)

    p_x = matmul(hx2, w_dn, out_dtype=F32).reshape(b, t, -1)
    p_c = matmul(hc2, w_dn, out_dtype=F32).reshape(b, lctx, -1)
    qkv_c = dn_conv(p_c, dn_conv_w[0])
    qkv_x = dn_conv(p_x, dn_conv_w[0])
    s0 = jnp.zeros((b, 2, DN_HEADS, HEAD_DIM, HEAD_DIM), F32)
    of_c, ob_c, s_c = dn_scan(qkv_c, p_c, par, s0)
    of_x, ob_x, _ = dn_scan(qkv_x, p_x, par, s_c)
    dn_x = dn_out(of_x, ob_x, p_x, dn_norm_w[0])
    dn_c = dn_out(of_c, ob_c, p_c, dn_norm_w[0])

    qk_x = matmul(hx2, w_dqk, out_dtype=F32).reshape(b, t, -1)
    qk_c = matmul(hc2, w_dqk, out_dtype=F32).reshape(b, lctx, -1)
    v_x = matmul(hx2, w_dv, out_dtype=BF16).reshape(b, t, -1)
    v_c = matmul(hc2, w_dv, out_dtype=BF16).reshape(b, lctx, -1)
    cos_d, sin_d = rope_tables(t, DIFF_DQK, DIFF_DQK // 4)
    one_c = jnp.ones((lctx, LANES), F32)
    zero_c = jnp.zeros((lctx, LANES), F32)
    dscale = DIFF_DQK ** -0.5 * math.log2(math.e)
    q1_x, q2_x = rope(qk_x, 0, dq_w, cos_d, sin_d, half=DIFF_DQK // 4, scale=dscale, split=True)
    k_x = rope(qk_x, 1, dq_w, cos_d, sin_d, half=DIFF_DQK // 4)
    q1_c, q2_c = rope(qk_c, 0, dq_w, one_c, zero_c, half=DIFF_DQK // 4, scale=dscale, split=True)
    k_c = rope(qk_c, 1, dq_w, one_c, zero_c, half=DIFF_DQK // 4)
    k_all = jnp.concatenate([k_x, k_c], axis=1)
    v_all = jnp.concatenate([v_x, v_c], axis=1)
    d_x = diff_attention(q1_x, q2_x, k_all, v_all, diff_lambda[0], diff_norm_w[0], lam_init)
    d_c = diff_attention(q1_c, q2_c, k_c, v_c, diff_lambda[0], diff_norm_w[0], lam_init)

    w_out = ab_w_out[0].astype(BF16)
    x = _out_proj(jnp.concatenate([dn_x, d_x], axis=-1), w_out, x, mod_x(0, 2))
    ctx = _out_proj(jnp.concatenate([dn_c, d_c], axis=-1), w_out, ctx, mod_c(0, 2))

    wq = peer_wq[0].astype(BF16)
    sub = peer_subkeys[0].astype(BF16)
    u_tab = peer_u[0].astype(BF16)
    vt_tab = peer_v[0].T.astype(BF16)
    fx = normmod(x, norm_ffn_w[0], mod_x(0, 3), mod_x(0, 4))
    x = _peer_ffn(x, fx, mod_x(0, 5), wq, peer_bq[0], sub, u_tab, vt_tab)
    fc = normmod(ctx, norm_ffn_w[0], mod_c(0, 3), mod_c(0, 4))
    ctx = _peer_ffn(ctx, fc, mod_c(0, 5), wq, peer_bq[0], sub, u_tab, vt_tab)

    gq_w = GQA_Q_HEADS * HEAD_DIM
    gkv_w = GQA_KV_HEADS * HEAD_DIM
    w_g = gqa_w_in[0]
    w_gqk = w_g[:, :gq_w + gkv_w].astype(BF16)
    w_gv = w_g[:, gq_w + gkv_w:].astype(BF16)
    w_gkv = w_g[:, gq_w:].astype(BF16)
    hx = normmod(x, norm_mix_w[1], mod_x(1, 0), mod_x(1, 1))
    hc = normmod(ctx, norm_mix_w[1], mod_c(1, 0), mod_c(1, 1))
    hx2 = hx.reshape(n, d)
    qk = matmul(hx2, w_gqk, out_dtype=F32, tn_cap=1280).reshape(b, t, -1)
    gv = matmul(hx2, w_gv, out_dtype=BF16).reshape(b, t, -1)
    kvx = matmul(hc.reshape(nctx, d), w_gkv, out_dtype=BF16).reshape(b, lctx, -1)
    cos_g, sin_g = rope_tables(t, HEAD_DIM, HEAD_DIM // 4)
    gq = rope(qk, 0, gq_w, cos_g, sin_g, half=HEAD_DIM // 4, scale=HEAD_DIM ** -0.5)
    gk = rope(qk, gq_w // gkv_w, gkv_w, cos_g, sin_g, half=HEAD_DIM // 4)
    att = window_attention(gq, gk, gv, kvx, gqa_sink[0])
    x = _out_proj(att, gqa_w_out[0].astype(BF16), x, mod_x(1, 2))

    wq = peer_wq[1].astype(BF16)
    sub = peer_subkeys[1].astype(BF16)
    u_tab = peer_u[1].astype(BF16)
    vt_tab = peer_v[1].T.astype(BF16)
    fx = normmod(x, norm_ffn_w[1], mod_x(1, 3), mod_x(1, 4))
    return _peer_ffn(x, fx, mod_x(1, 5), wq, peer_bq[1], sub, u_tab, vt_tab, final_w=final_norm_w)
```

```python
import functools
import math

import jax
import jax.numpy as jnp
import numpy as np
from jax import lax
from jax.experimental import pallas as pl
from jax.experimental.pallas import tpu as pltpu

F32 = jnp.float32
BF16 = jnp.bfloat16

LANES = 128
HEAD_DIM = 128
NORM_EPS = 1e-6
ROPE_THETA = 10000.0
GRID_W = 64
DN_HEADS = 8
DN_CHUNK = 64
DN_CONV = 5
DIFF_HEADS = 8
DIFF_DQK = 64
GQA_Q_HEADS = 16
GQA_KV_HEADS = 4
GQA_GROUP = GQA_Q_HEADS // GQA_KV_HEADS
WINDOW = 128
PEER_HEADS = 8
PEER_NKEYS = 128
PEER_TOPK = 16
NEG_BIG = -1e30
VMEM_LIMIT = 56 * 1024 * 1024


def _params(sem):
    return pltpu.CompilerParams(dimension_semantics=sem, vmem_limit_bytes=VMEM_LIMIT)


def _dot(a, b):
    return jnp.dot(a.astype(BF16), b.astype(BF16), preferred_element_type=F32)


def _dot_nt(a, b):
    return lax.dot_general(a.astype(BF16), b.astype(BF16), (((1,), (1,)), ((), ())),
                           preferred_element_type=F32)


def _sigmoid(x):
    return 1.0 / (1.0 + jnp.exp(-x))


def _normmod_kernel(x_ref, w_ref, shift_ref, scale_ref, o_ref):
    x = x_ref[...]
    y = x * lax.rsqrt(jnp.mean(x * x, axis=-1, keepdims=True) + NORM_EPS)
    y = y * w_ref[...]
    o_ref[...] = (y * (1.0 + scale_ref[...]) + shift_ref[...]).astype(o_ref.dtype)


def normmod(x, w, shift, scale, rows=512):
    b, t, d = x.shape
    tr = min(rows, t)
    bm = shift.shape[0]
    mod_map = (lambda i, j: (i, 0, 0)) if bm == b else (lambda i, j: (0, 0, 0))
    return pl.pallas_call(
        _normmod_kernel,
        grid=(b, t // tr),
        in_specs=[pl.BlockSpec((None, tr, d), lambda i, j: (i, j, 0)),
                  pl.BlockSpec((1, d), lambda i, j: (0, 0)),
                  pl.BlockSpec((None, 1, d), mod_map),
                  pl.BlockSpec((None, 1, d), mod_map)],
        out_specs=pl.BlockSpec((None, tr, d), lambda i, j: (i, j, 0)),
        out_shape=jax.ShapeDtypeStruct((b, t, d), BF16),
        compiler_params=_params(("parallel", "parallel")),
        name="normmod",
    )(x, w.reshape(1, d), shift.reshape(bm, 1, d), scale.reshape(bm, 1, d))


def _normmod_rows(x, w, shift, scale):
    y = x * lax.rsqrt(jnp.mean(x * x, axis=-1, keepdims=True) + NORM_EPS) * w
    return y * (1.0 + scale) + shift


def _mm_kernel(*refs, silu_in, has_bias, has_res, has_norm):
    a_ref, w_ref = refs[0], refs[1]
    pos = 2
    a = a_ref[...]
    if silu_in:
        a = a * _sigmoid(a)
    acc = _dot(a, w_ref[...])
    if has_bias:
        acc = acc + refs[pos][...]
        pos += 1
    if has_res:
        acc = refs[pos][...] + refs[pos + 1][...] * acc
        pos += 2
    if has_norm:
        nw_ref, shift_ref, scale_ref = refs[pos:pos + 3]
        pos += 3
    o_ref = refs[pos]
    o_ref[...] = acc.astype(o_ref.dtype)
    if has_norm:
        h_ref = refs[pos + 1]
        h_ref[...] = _normmod_rows(acc, nw_ref[...], shift_ref[...], scale_ref[...]).astype(h_ref.dtype)


def _pick_tn(n, cap):
    best = LANES
    for cand in range(LANES, min(n, cap) + 1, LANES):
        if n % cand == 0:
            best = cand
    return best


def matmul(a, w, *, out_dtype, bias=None, res=None, gate=None, rows_per_gate=None,
           silu_in=False, tm=512, tn_cap=2048, layer=0, norm=None):
    m, k = a.shape
    n = w.shape[-1]
    tm = min(tm, m)
    tn = _pick_tn(n, tn_cap)
    assert norm is None or (tn == n and res is not None)
    if w.ndim == 3:
        w_spec = pl.BlockSpec((None, k, tn), lambda j, i: (layer, 0, j))
    else:
        w_spec = pl.BlockSpec((k, tn), lambda j, i: (0, j))
    in_specs = [pl.BlockSpec((tm, k), lambda j, i: (i, 0)), w_spec]
    args = [a, w]
    if bias is not None:
        in_specs.append(pl.BlockSpec((1, tn), lambda j, i: (0, j)))
        args.append(bias.reshape(1, n).astype(F32))
    if res is not None:
        blocks_per_gate = rows_per_gate // tm
        in_specs.append(pl.BlockSpec((tm, tn), lambda j, i: (i, j)))
        in_specs.append(pl.BlockSpec((None, 1, tn), lambda j, i: (i // blocks_per_gate, 0, j)))
        args += [res, gate.reshape(gate.shape[0], 1, n)]
    out_spec = pl.BlockSpec((tm, tn), lambda j, i: (i, j))
    out_specs, out_shape = out_spec, jax.ShapeDtypeStruct((m, n), out_dtype)
    if norm is not None:
        mod_spec = pl.BlockSpec((None, 1, tn), lambda j, i: (i // blocks_per_gate, 0, j))
        in_specs += [pl.BlockSpec((1, tn), lambda j, i: (0, j)), mod_spec, mod_spec]
        args += [norm[0].reshape(1, n), norm[1].reshape(-1, 1, n), norm[2].reshape(-1, 1, n)]
        out_specs = [out_spec, out_spec]
        out_shape = [out_shape, jax.ShapeDtypeStruct((m, n), BF16)]
    kern = functools.partial(_mm_kernel, silu_in=silu_in, has_bias=bias is not None,
                             has_res=res is not None, has_norm=norm is not None)
    return pl.pallas_call(
        kern,
        grid=(n // tn, m // tm),
        in_specs=in_specs,
        out_specs=out_specs,
        out_shape=out_shape,
        compiler_params=_params(("parallel", "parallel")),
        name="matmul",
    )(*args)


def _conv_kernel(prev_ref, cur_ref, next_ref, w_ref, o_ref, *, tt, nheads_blk):
    i = pl.program_id(1)
    grp = pl.program_id(2)
    last = pl.num_programs(1) - 1
    prev = prev_ref[...] * (i > 0).astype(F32)
    nxt = next_ref[...] * (i < last).astype(F32)
    xx = jnp.concatenate([prev, cur_ref[...], nxt], axis=0)
    rows = tt + 16
    pad = (DN_CONV - 1) // 2
    y = None
    for k in range(DN_CONV):
        shift = (pad - k) % rows
        z = xx if shift == 0 else pltpu.roll(xx, shift, 0)
        term = z[8:8 + tt] * w_ref[k:k + 1, :]
        y = term if y is None else y + term
    y = y * _sigmoid(y)
    qscale = jnp.where(grp == 0, HEAD_DIM ** -0.5, 1.0).astype(F32)
    for h in range(nheads_blk):
        yh = y[:, h * HEAD_DIM:(h + 1) * HEAD_DIM]
        ss = jnp.sum(yh * yh, axis=-1, keepdims=True)
        fac = jnp.where(grp == 2, 1.0, lax.rsqrt(ss + NORM_EPS) * qscale)
        o_ref[:, h * HEAD_DIM:(h + 1) * HEAD_DIM] = yh * fac


def dn_conv(p, conv_w, tt=256):
    b, t, _ = p.shape
    tt = min(tt, t)
    cb = DN_HEADS * HEAD_DIM
    n8 = t // 8
    kern = functools.partial(_conv_kernel, tt=tt, nheads_blk=DN_HEADS)
    return pl.pallas_call(
        kern,
        grid=(b, t // tt, 3),
        in_specs=[pl.BlockSpec((None, 8, cb), lambda bi, i, g: (bi, jnp.maximum(i * (tt // 8) - 1, 0), g)),
                  pl.BlockSpec((None, tt, cb), lambda bi, i, g: (bi, i, g)),
                  pl.BlockSpec((None, 8, cb), lambda bi, i, g: (bi, jnp.minimum((i + 1) * (tt // 8), n8 - 1), g)),
                  pl.BlockSpec((DN_CONV, cb), lambda bi, i, g: (0, g))],
        out_specs=pl.BlockSpec((None, tt, cb), lambda bi, i, g: (bi, i, g)),
        out_shape=jax.ShapeDtypeStruct((b, t, 3 * cb), F32),
        compiler_params=_params(("parallel", "parallel", "parallel")),
        name="dn_conv",
    )(p, p, p, conv_w)


def _split3(x):
    hi = x.astype(BF16)
    r1 = x - hi.astype(F32)
    mid = r1.astype(BF16)
    lo = (r1 - mid.astype(F32)).astype(BF16)
    return hi, mid, lo


def _dn_chains(direction, q_ref, k_ref, v_ref, p_ref, par_ref, o_ref, s_scr):
    c = DN_CHUNK
    ii = lax.broadcasted_iota(jnp.int32, (c, c), 0)
    jj = lax.broadcasted_iota(jnp.int32, (c, c), 1)
    incl = (ii >= jj) if direction == 0 else (ii <= jj)
    strict = (ii > jj) if direction == 0 else (ii < jj)

    raw = p_ref[...]
    beta_all = _sigmoid(raw)
    xa = raw + par_ref[0:1, :]
    softplus = jnp.maximum(xa, 0.0) + jnp.log1p(jnp.exp(-jnp.abs(xa)))
    g_all = -jnp.exp(par_ref[1:2, :]) * softplus
    inclb = incl.astype(BF16)
    hi, mid, lo = _split3(g_all)
    gc_all = (jnp.dot(inclb, hi, preferred_element_type=F32)
              + jnp.dot(inclb, mid, preferred_element_type=F32)
              + jnp.dot(inclb, lo, preferred_element_type=F32))
    gtot_all = jnp.sum(g_all, axis=0, keepdims=True)
    gc_t = gc_all.T

    chains = []
    for h in range(DN_HEADS):
        ib = direction * DN_HEADS + h
        ig = 2 * DN_HEADS + ib
        sl = slice(h * HEAD_DIM, (h + 1) * HEAD_DIM)
        gc_c = gc_all[:, ig:ig + 1]
        chains.append(dict(
            incl=incl, strict=strict, beta=beta_all[:, ib:ib + 1], gc_c=gc_c,
            decay=jnp.exp(jnp.where(incl, gc_c - gc_t[ig:ig + 1, :], -jnp.inf)),
            gtot=gtot_all[:, ig:ig + 1], q=q_ref[:, sl], k=k_ref[:, sl], v=v_ref[:, sl],
            o_ref=o_ref, sl=sl, state_idx=(direction, h)))
    return chains


def _dn_step(chains, s_scr):
    c = DN_CHUNK
    ii = lax.broadcasted_iota(jnp.int32, (c, c), 0)
    jj = lax.broadcasted_iota(jnp.int32, (c, c), 1)
    eye = (ii == jj).astype(F32)
    for ch in chains:
        ch["kb"] = ch["k"] * ch["beta"]
        ch["kq"] = _dot_nt(jnp.concatenate([ch["kb"], ch["q"]], axis=0), ch["k"])
    for ch in chains:
        a = jnp.where(ch["strict"], ch["kq"][:c] * ch["decay"], 0.0)
        ch["qk"] = jnp.where(ch["incl"], ch["kq"][c:] * ch["decay"], 0.0)
        ch["inv"] = eye - a
        ch["a_pow"] = a
    for _ in range(int(math.log2(c)) - 1):
        for ch in chains:
            ch["a_pow"] = _dot(ch["a_pow"], ch["a_pow"])
        for ch in chains:
            ch["inv"] = _dot(ch["inv"], eye + ch["a_pow"])
    for ch in chains:
        egc = jnp.exp(ch["gc_c"])
        ch["uw"] = _dot(ch["inv"], jnp.concatenate([ch["v"] * ch["beta"], ch["kb"] * egc], axis=1))
        ch["q_dec"] = ch["q"] * egc
    for ch in chains:
        ch["state"] = s_scr[ch["state_idx"]]
        w = ch["uw"][:, HEAD_DIM:]
        ch["ws"] = _dot(jnp.concatenate([w, ch["q_dec"]], axis=0), ch["state"])
    for ch in chains:
        ch["v_new"] = ch["uw"][:, :HEAD_DIM] - ch["ws"][:c]
        ch["o_ref"][:, ch["sl"]] = ch["ws"][c:] + _dot(ch["qk"], ch["v_new"])
    for ch in chains:
        k_dec = ch["k"] * jnp.exp(ch["gtot"] - ch["gc_c"])
        s_scr[ch["state_idx"]] = ch["state"] * jnp.exp(ch["gtot"]) + _dot(k_dec.T, ch["v_new"])


def _dn_scan_kernel(qf_ref, kf_ref, vf_ref, pf_ref, qb_ref, kb_ref, vb_ref, pb_ref, par_ref, s0_ref,
                    of_ref, ob_ref, sfin_ref, s_scr):
    n = pl.program_id(1)

    @pl.when(n == 0)
    def _():
        s_scr[...] = s0_ref[...]

    chains = (_dn_chains(0, qf_ref, kf_ref, vf_ref, pf_ref, par_ref, of_ref, s_scr)
              + _dn_chains(1, qb_ref, kb_ref, vb_ref, pb_ref, par_ref, ob_ref, s_scr))
    _dn_step(chains, s_scr)

    @pl.when(n == pl.num_programs(1) - 1)
    def _():
        sfin_ref[...] = s_scr[...]


def dn_scan(qkv, p, par, s0):
    b, t, _ = qkv.shape
    c = DN_CHUNK
    nc = t // c
    w = DN_HEADS * HEAD_DIM
    fwd = lambda col: (lambda bi, n: (bi, n, col))
    bwd = lambda col: (lambda bi, n: (bi, nc - 1 - n, col))
    blk = lambda f: pl.BlockSpec((None, c, w), f)
    small = lambda f: pl.BlockSpec((None, c, LANES), f)
    state_spec = pl.BlockSpec((None, 2, DN_HEADS, HEAD_DIM, HEAD_DIM), lambda bi, n: (bi, 0, 0, 0, 0))
    o_shape = jax.ShapeDtypeStruct((b, t, w), F32)
    return pl.pallas_call(
        _dn_scan_kernel,
        grid=(b, nc),
        in_specs=[blk(fwd(0)), blk(fwd(1)), blk(fwd(2)), small(fwd(32)),
                  blk(bwd(0)), blk(bwd(1)), blk(bwd(2)), small(bwd(32)),
                  pl.BlockSpec((8, LANES), lambda bi, n: (0, 0)), state_spec],
        out_specs=[blk(fwd(0)), blk(bwd(0)), state_spec],
        out_shape=[o_shape, o_shape, jax.ShapeDtypeStruct(s0.shape, F32)],
        scratch_shapes=[pltpu.VMEM((2, DN_HEADS, HEAD_DIM, HEAD_DIM), F32)],
        compiler_params=_params(("parallel", "arbitrary")),
        name="dn_scan",
    )(qkv, qkv, qkv, p, qkv, qkv, qkv, p, par, s0)


def _dn_out_kernel(of_ref, ob_ref, z_ref, w_ref, y_ref):
    o = of_ref[...] + ob_ref[...]
    z = z_ref[...]
    for h in range(DN_HEADS):
        sl = slice(h * HEAD_DIM, (h + 1) * HEAD_DIM)
        oh = o[:, sl]
        yh = oh * lax.rsqrt(jnp.mean(oh * oh, axis=-1, keepdims=True) + NORM_EPS) * w_ref[...]
        zh = z[:, sl]
        y_ref[:, sl] = (yh * (zh * _sigmoid(zh))).astype(y_ref.dtype)


def dn_out(o_f, o_b, p, norm_w, tr=256):
    b, t, w = o_f.shape
    tr = min(tr, t)
    return pl.pallas_call(
        _dn_out_kernel,
        grid=(b, t // tr),
        in_specs=[pl.BlockSpec((None, tr, w), lambda bi, i: (bi, i, 0)),
                  pl.BlockSpec((None, tr, w), lambda bi, i: (bi, i, 0)),
                  pl.BlockSpec((None, tr, w), lambda bi, i: (bi, i, 3)),
                  pl.BlockSpec((1, HEAD_DIM), lambda bi, i: (0, 0))],
        out_specs=pl.BlockSpec((None, tr, w), lambda bi, i: (bi, i, 0)),
        out_shape=jax.ShapeDtypeStruct((b, t, w), BF16),
        compiler_params=_params(("parallel", "parallel")),
        name="dn_out",
    )(o_f, o_b, p, norm_w.reshape(1, HEAD_DIM))


def _rope_kernel(x_ref, cos_ref, sin_ref, *o_refs, half, scale, nheads, split):
    lane = lax.broadcasted_iota(jnp.int32, (1, LANES), 1)
    first = (lane % (2 * half)) < half
    cos = cos_ref[...]
    sin = sin_ref[...]
    for h in range(nheads):
        sl = slice(h * LANES, (h + 1) * LANES)
        x = x_ref[:, sl]
        partner = jnp.where(first, pltpu.roll(x, LANES - half, 1), pltpu.roll(x, half, 1))
        y = (x * cos + partner * sin) * scale
        if split:
            o_refs[0][:, sl] = jnp.where(lane < LANES // 2, y, 0.0).astype(BF16)
            o_refs[1][:, sl] = jnp.where(lane >= LANES // 2, y, 0.0).astype(BF16)
        else:
            o_refs[0][:, sl] = y.astype(BF16)


def rope(x, col_block, width, cos, sin, *, half, scale=1.0, split=False, tr=512):
    b, t, _ = x.shape
    tr = min(tr, t)
    nheads = width // LANES
    nout = 2 if split else 1
    kern = functools.partial(_rope_kernel, half=half, scale=scale, nheads=nheads, split=split)
    out = pl.pallas_call(
        kern,
        grid=(b, t // tr),
        in_specs=[pl.BlockSpec((None, tr, width), lambda bi, i: (bi, i, col_block)),
                  pl.BlockSpec((tr, LANES), lambda bi, i: (i, 0)),
                  pl.BlockSpec((tr, LANES), lambda bi, i: (i, 0))],
        out_specs=[pl.BlockSpec((None, tr, width), lambda bi, i: (bi, i, 0))] * nout,
        out_shape=[jax.ShapeDtypeStruct((b, t, width), BF16)] * nout,
        compiler_params=_params(("parallel", "parallel")),
        name="rope",
    )(x, cos, sin)
    return out if split else out[0]


def rope_tables(t, block, half):
    pos = jnp.arange(t, dtype=jnp.int32)
    rows = (pos // GRID_W).astype(F32)
    cols = (pos % GRID_W).astype(F32)
    lane = np.arange(LANES)
    j = lane % block
    use_col = (j // (2 * half)) == 1
    i = j % (2 * half)
    inv = (ROPE_THETA ** (-jnp.arange(half, dtype=F32) / half))[i % half]
    p = jnp.where(jnp.asarray(use_col)[None, :], cols[:, None], rows[:, None])
    ang = p * inv[None, :]
    sign = jnp.asarray(np.where(i < half, -1.0, 1.0).astype(np.float32))[None, :]
    return jnp.cos(ang), jnp.sin(ang) * sign


def _diff_kernel(q1_ref, q2_ref, k_ref, vt_ref, lam_ref, nw_ref, o_ref, acc1, acc2, s_scr,
                 *, lam_init, tk, nkv):
    tq = q1_ref.shape[0]
    acc1[...] = jnp.zeros_like(acc1)
    acc2[...] = jnp.zeros_like(acc2)

    def scores(c, slot):
        kc = k_ref[pl.ds(pl.multiple_of(c * tk, tk), tk), :]
        s_scr[slot, 0] = _dot_nt(kc, q1_ref[...])
        s_scr[slot, 1] = _dot_nt(kc, q2_ref[...])

    def update(s, vtc, m, l, acc):
        m_new = jnp.maximum(m, jnp.max(s, axis=0, keepdims=True))
        alpha = jnp.exp2(m - m_new)
        p = jnp.exp2(s - m_new)
        l_new = l * alpha + jnp.sum(p, axis=0, keepdims=True)
        acc[...] = acc[...] * alpha + _dot(vtc, p)
        return m_new, l_new

    def softmax(c, slot, carry):
        m1, l1, m2, l2 = carry
        vtc = vt_ref[c]
        m1, l1 = update(s_scr[slot, 0], vtc, m1, l1, acc1)
        m2, l2 = update(s_scr[slot, 1], vtc, m2, l2, acc2)
        return m1, l1, m2, l2

    def body(i, carry):
        c = 2 * i
        scores(c + 1, 1)
        carry = softmax(c, 0, carry)
        scores(c + 2, 0)
        return softmax(c + 1, 1, carry)

    neg = jnp.full((1, tq), -jnp.inf, F32)
    zero = jnp.zeros((1, tq), F32)
    scores(0, 0)
    pairs = (nkv - 1) // 2
    carry = lax.fori_loop(0, pairs, body, (neg, zero, neg, zero))
    if nkv % 2 == 0:
        scores(nkv - 1, 1)
        carry = softmax(nkv - 2, 0, carry)
        carry = softmax(nkv - 1, 1, carry)
    else:
        carry = softmax(nkv - 1, 0, carry)
    m1, l1, m2, l2 = carry

    lv = lam_ref[...]
    s01 = jnp.sum(lv[0:1] * lv[1:2], axis=-1, keepdims=True)
    s23 = jnp.sum(lv[2:3] * lv[3:4], axis=-1, keepdims=True)
    lam = jnp.exp(s01) - jnp.exp(s23) + lam_init
    o = acc1[...] / l1 - lam * (acc2[...] / l2)
    y = o * lax.rsqrt(jnp.mean(o * o, axis=0, keepdims=True) + NORM_EPS)
    o_ref[...] = (y.T * nw_ref[...] * (1.0 - lam_init)).astype(o_ref.dtype)


def diff_attention(q1, q2, k, v, lam_vec, norm_w, lam_init, tq=1024, tk=768):
    b, tqa, w = q1.shape
    tka = k.shape[1]
    tq = min(tq, tqa)
    tk = min(tk, tka)
    nkv = tka // tk
    nh = w // HEAD_DIM
    vt = v.reshape(b, nkv, tk, nh, HEAD_DIM).transpose(0, 3, 1, 4, 2)
    kern = functools.partial(_diff_kernel, lam_init=lam_init, tk=tk, nkv=nkv)
    return pl.pallas_call(
        kern,
        grid=(b, nh, tqa // tq),
        in_specs=[pl.BlockSpec((None, tq, HEAD_DIM), lambda bi, h, i: (bi, i, h)),
                  pl.BlockSpec((None, tq, HEAD_DIM), lambda bi, h, i: (bi, i, h)),
                  pl.BlockSpec((None, tka, HEAD_DIM), lambda bi, h, i: (bi, 0, h)),
                  pl.BlockSpec((None, None, nkv, HEAD_DIM, tk), lambda bi, h, i: (bi, h, 0, 0, 0)),
                  pl.BlockSpec((4, DIFF_DQK), lambda bi, h, i: (0, 0)),
                  pl.BlockSpec((1, HEAD_DIM), lambda bi, h, i: (0, 0))],
        out_specs=pl.BlockSpec((None, tq, HEAD_DIM), lambda bi, h, i: (bi, i, h)),
        out_shape=jax.ShapeDtypeStruct((b, tqa, w), BF16),
        scratch_shapes=[pltpu.VMEM((HEAD_DIM, tq), F32)] * 2 + [pltpu.VMEM((2, 2, tk, tq), F32)],
        compiler_params=_params(("parallel", "parallel", "parallel")),
        name="diff_attention",
    )(q1, q2, k, vt, lam_vec, norm_w.reshape(1, HEAD_DIM))


def _win_kernel(sink_ref, q_ref, kp_ref, kc_ref, kn_ref, vp_ref, vc_ref, vn_ref, kx_ref, vx_ref, o_ref,
                *, t_total):
    kvh = pl.program_id(1)
    n = pl.program_id(2)
    wb = WINDOW
    tq = q_ref.shape[0]
    keys = jnp.concatenate([kp_ref[...], kc_ref[...], kn_ref[...], kx_ref[...]], axis=0)
    vals = jnp.concatenate([vp_ref[...], vc_ref[...], vn_ref[...], vx_ref[...]], axis=0)
    nk = keys.shape[0]
    nlocal = tq + 2 * wb
    i = lax.broadcasted_iota(jnp.int32, (tq, nk), 0)
    j = lax.broadcasted_iota(jnp.int32, (tq, nk), 1)
    kpos = n * tq - wb + j
    valid = (jnp.abs(j - wb - i) <= WINDOW) & (kpos >= 0) & (kpos < t_total)
    valid = valid | (j >= nlocal)
    for g in range(GQA_GROUP):
        sl = slice(g * HEAD_DIM, (g + 1) * HEAD_DIM)
        sink = sink_ref[kvh * GQA_GROUP + g]
        s = jnp.where(valid, _dot_nt(q_ref[:, sl], keys), -jnp.inf)
        m = jnp.maximum(jnp.max(s, axis=-1, keepdims=True), sink)
        p = jnp.exp(s - m)
        denom = jnp.sum(p, axis=-1, keepdims=True) + jnp.exp(sink - m)
        o_ref[:, sl] = (_dot(p, vals) / denom).astype(o_ref.dtype)


def window_attention(q, k, v, kvx, sink, tq=512):
    b, t, _ = q.shape
    lctx = kvx.shape[1]
    wb = WINDOW
    tq = min(tq, t)
    per = tq // wb
    nb = t // wb
    qw = GQA_GROUP * HEAD_DIM
    kern = functools.partial(_win_kernel, t_total=t)
    prev = lambda bi, h, n: (bi, jnp.maximum(n * per - 1, 0), h)
    cur = lambda bi, h, n: (bi, n, h)
    nxt = lambda bi, h, n: (bi, jnp.minimum((n + 1) * per, nb - 1), h)
    edge_spec = lambda f: pl.BlockSpec((None, wb, HEAD_DIM), f)
    main_spec = pl.BlockSpec((None, tq, HEAD_DIM), cur)
    return pl.pallas_call(
        kern,
        grid=(b, GQA_KV_HEADS, t // tq),
        in_specs=[pl.BlockSpec(memory_space=pltpu.SMEM),
                  pl.BlockSpec((None, tq, qw), cur),
                  edge_spec(prev), main_spec, edge_spec(nxt),
                  edge_spec(prev), main_spec, edge_spec(nxt),
                  pl.BlockSpec((None, lctx, HEAD_DIM), lambda bi, h, n: (bi, 0, h)),
                  pl.BlockSpec((None, lctx, HEAD_DIM), lambda bi, h, n: (bi, 0, GQA_KV_HEADS + h))],
        out_specs=pl.BlockSpec((None, tq, qw), cur),
        out_shape=jax.ShapeDtypeStruct(q.shape, BF16),
        compiler_params=_params(("parallel", "parallel", "parallel")),
        name="window_attention",
    )(sink, q, k, k, k, v, v, v, kvx, kvx)


RANK_NONE = 255.0
RANK_CODE_BASE = 1e30
RANK_CODE_STEP = 1e28


def _top_values(s, count, want_rank=False):
    vals = []
    work = s
    for r in range(count):
        m = jnp.max(work, axis=0, keepdims=True)
        vals.append(m)
        work = jnp.where(work == m, -(RANK_CODE_BASE + r * RANK_CODE_STEP), work)
    if not want_rank:
        return vals
    rank = jnp.where(work < -0.5 * RANK_CODE_BASE,
                     jnp.round((-work - RANK_CODE_BASE) * (1.0 / RANK_CODE_STEP)), RANK_NONE)
    return vals, rank


def _route_kernel(q_ref, sub_ref, cut_ref, e1_ref, r2_ref, e2_ref):
    kk = PEER_TOPK
    tb = q_ref.shape[0]
    row = lax.broadcasted_iota(jnp.int32, (kk, tb), 0)
    for h in range(PEER_HEADS):
        scores, tops, ranks = [], [], []
        for p in range(2):
            c0 = (h * 2 + p) * PEER_NKEYS
            s = _dot_nt(sub_ref[h, p], q_ref[:, c0:c0 + PEER_NKEYS])
            vals, rank = _top_values(s, kk, want_rank=True)
            scores.append(s)
            tops.append(vals)
            ranks.append(rank)
        v1 = jnp.zeros((kk, tb), F32)
        v2 = jnp.zeros((kk, tb), F32)
        for i in range(kk):
            v1 = jnp.where(row == i, tops[0][i], v1)
            v2 = jnp.where(row == i, tops[1][i], v2)
        mid = jnp.where(row[:8] >= 4, v1[:8], -jnp.inf)
        cand = jnp.concatenate([tops[0][0] + v2] + [tops[0][i] + v2[:8] for i in range(1, 4)]
                               + [mid + tops[1][j] for j in range(3)] + [v1[8:] + tops[1][0]], axis=0)
        thr = _top_values(cand, kk)[kk - 1]
        top = tops[0][0] + tops[1][0]
        z = jnp.sum(jnp.where(cand >= thr, jnp.exp(cand - top), 0.0), axis=0, keepdims=True)
        gdt = r2_ref.dtype
        rank1 = ranks[0].astype(gdt)
        cut = jnp.zeros(rank1.shape, gdt)
        for i in range(kk):
            count = jnp.sum(((tops[0][i] + v2) >= thr).astype(F32), axis=0, keepdims=True)
            cut = jnp.where(rank1 == i, count.astype(gdt), cut)
        in1 = ranks[0] < float(kk)
        in2 = ranks[1] < float(kk)
        cut_ref[h] = cut.astype(F32)
        e1_ref[h] = jnp.where(in1, jnp.exp(scores[0] - tops[0][0]), 0.0)
        r2_ref[h] = ranks[1].astype(r2_ref.dtype)
        e2_ref[h] = jnp.where(in2, jnp.exp(scores[1] - tops[1][0]) / z, 0.0).astype(e2_ref.dtype)


def peer_route(q, subkeys, tb=128):
    n = q.shape[0]
    tb = min(tb, n)
    hh = PEER_HEADS
    tab32 = jax.ShapeDtypeStruct((hh, PEER_NKEYS, n), F32)
    tab16 = jax.ShapeDtypeStruct((hh, PEER_NKEYS, n), BF16)
    tab_spec = pl.BlockSpec((hh, PEER_NKEYS, tb), lambda i: (0, 0, i))
    return pl.pallas_call(
        _route_kernel,
        grid=(n // tb,),
        in_specs=[pl.BlockSpec((tb, q.shape[1]), lambda i: (i, 0)),
                  pl.BlockSpec(subkeys.shape, lambda i: (0, 0, 0, 0))],
        out_specs=[tab_spec] * 4,
        out_shape=[tab32, tab32, tab16, tab16],
        compiler_params=_params(("parallel",)),
        name="peer_route",
    )(q, subkeys)


def _gelu(x):
    return 0.5 * x * (1.0 + lax.erf(x * (2.0 ** -0.5)))


def _peer_kernel(xt_ref, u_ref, vt_ref, cut_ref, e1_ref, r2_ref, e2_ref, o_ref, acc, g_scr, h_scr, a_scr,
                 *, rows, rows_per_part):
    et = pl.program_id(1)

    @pl.when(et == 0)
    def _():
        acc[...] = jnp.zeros_like(acc)

    nk = PEER_NKEYS
    gdt = r2_ref.dtype
    nparts = rows // rows_per_part
    pk = rows_per_part * nk

    def gate_row(r):
        gate = None
        for h in range(PEER_HEADS):
            cut = cut_ref[h, r:r + 1, :].astype(gdt)
            e1 = e1_ref[h, r:r + 1, :].astype(gdt)
            g = jnp.where(r2_ref[h] < cut, e1 * e2_ref[h], jnp.zeros((), gdt))
            gate = g if gate is None else gate + g
        return gate

    def hidden(part):
        return jnp.dot(u_ref[part * pk:(part + 1) * pk, :], xt_ref[...], preferred_element_type=F32)

    def activate(ht, gate):
        return (_gelu(ht) * gate.astype(F32)).astype(BF16)

    def project(part, a):
        return jnp.dot(vt_ref[:, part * pk:(part + 1) * pk], a, preferred_element_type=F32)

    assert nparts == 2
    for r in range(rows_per_part):
        g_scr[r * nk:(r + 1) * nk, :] = gate_row(r)
    h_scr[...] = hidden(0)
    h1 = hidden(1)
    acc[...] += project(0, activate(h_scr[...], g_scr[...]))
    for r in range(rows_per_part):
        a_scr[r * nk:(r + 1) * nk, :] = activate(h1[r * nk:(r + 1) * nk, :], gate_row(rows_per_part + r))
    acc[...] += project(1, a_scr[...])

    @pl.when(et == pl.num_programs(1) - 1)
    def _():
        o_ref[...] = acc[...].T


def peer_experts(xt, u, vt, cut, e1, r2, e2, tb=512, rows=8, rows_per_part=4):
    d, n = xt.shape
    e = u.shape[0]
    tb = min(tb, n)
    te = rows * PEER_NKEYS
    hh = PEER_HEADS
    kern = functools.partial(_peer_kernel, rows=rows, rows_per_part=rows_per_part)
    row_spec = pl.BlockSpec((hh, rows, tb), lambda i, j: (0, j, i))
    full_spec = pl.BlockSpec((hh, PEER_NKEYS, tb), lambda i, j: (0, 0, i))
    return pl.pallas_call(
        kern,
        grid=(n // tb, e // te),
        in_specs=[pl.BlockSpec((d, tb), lambda i, j: (0, i)),
                  pl.BlockSpec((te, d), lambda i, j: (j, 0)),
                  pl.BlockSpec((d, te), lambda i, j: (0, j)),
                  row_spec, row_spec, full_spec, full_spec],
        out_specs=pl.BlockSpec((tb, d), lambda i, j: (i, 0)),
        out_shape=jax.ShapeDtypeStruct((n, d), F32),
        scratch_shapes=[pltpu.VMEM((d, tb), F32), pltpu.VMEM((te // 2, tb), r2.dtype),
                        pltpu.VMEM((te // 2, tb), F32), pltpu.VMEM((te // 2, tb), BF16)],
        compiler_params=_params(("parallel", "arbitrary")),
        name="peer_experts",
    )(xt, u, vt, cut, e1, r2, e2)


def _resid_kernel(*refs, final, has_norm):
    x_ref, y_ref, g_ref, w_ref = refs[:4]
    x = x_ref[...] + g_ref[...] * y_ref[...]
    if final:
        x = x * lax.rsqrt(jnp.mean(x * x, axis=-1, keepdims=True) + NORM_EPS) * w_ref[...]
    if has_norm:
        shift_ref, scale_ref, o_ref, h_ref = refs[4:]
        h_ref[...] = _normmod_rows(x, w_ref[...], shift_ref[...], scale_ref[...]).astype(h_ref.dtype)
    else:
        o_ref = refs[4]
    o_ref[...] = x


def resid(x, y, gate, final_w=None, norm=None, rows=512):
    b, t, d = x.shape
    tr = min(rows, t)
    bm = gate.shape[0]
    mod_map = (lambda i, j: (i, 0, 0)) if bm == b else (lambda i, j: (0, 0, 0))
    mod_spec = pl.BlockSpec((None, 1, d), mod_map)
    w = final_w if final_w is not None else (norm[0] if norm is not None else jnp.ones((d,), F32))
    kern = functools.partial(_resid_kernel, final=final_w is not None, has_norm=norm is not None)
    blk = pl.BlockSpec((None, tr, d), lambda i, j: (i, j, 0))
    in_specs = [blk, blk, mod_spec, pl.BlockSpec((1, d), lambda i, j: (0, 0))]
    args = [x, y, gate.reshape(bm, 1, d), w.reshape(1, d)]
    out_specs, out_shape = blk, jax.ShapeDtypeStruct((b, t, d), F32)
    if norm is not None:
        in_specs += [mod_spec, mod_spec]
        args += [norm[1].reshape(bm, 1, d), norm[2].reshape(bm, 1, d)]
        out_specs = [blk, blk]
        out_shape = [out_shape, jax.ShapeDtypeStruct((b, t, d), BF16)]
    return pl.pallas_call(
        kern,
        grid=(b, t // tr),
        in_specs=in_specs,
        out_specs=out_specs,
        out_shape=out_shape,
        compiler_params=_params(("parallel", "parallel")),
        name="resid",
    )(*args)


def _peer_ffn(x, fx, gate, wq, bq, subkeys, u, vt, final_w=None, norm=None):
    b, t, d = x.shape
    n = b * t
    q = matmul(fx.reshape(n, d), wq, out_dtype=BF16, bias=bq)
    cut, e1, r2, e2 = peer_route(q, subkeys)
    y = peer_experts(fx.reshape(n, d).T, u, vt, cut, e1, r2, e2)
    return resid(x, y.reshape(b, t, d), gate, final_w=final_w, norm=norm)


def _out_proj(a, w_out, x, gate, norm):
    b, t, d = x.shape
    per_batch = lambda v: v if v.shape[0] == b else jnp.broadcast_to(v, (b, d))
    y, h = matmul(a.reshape(b * t, a.shape[-1]), w_out, out_dtype=F32, res=x.reshape(b * t, d),
                  gate=per_batch(gate), rows_per_gate=t, tm=min(512, t),
                  norm=(norm[0], per_batch(norm[1]), per_batch(norm[2])))
    return y.reshape(b, t, d), h.reshape(b, t, d)


def kernel(x, c, ctx, c_ctx, ada_w, ada_b, norm_mix_w, norm_ffn_w, ab_w_in, ab_w_out, dn_conv_w, dn_a_log,
           dn_dt_bias, dn_norm_w, diff_lambda, diff_norm_w, gqa_w_in, gqa_w_out, gqa_sink, peer_wq, peer_bq,
           peer_subkeys, peer_u, peer_v, final_norm_w):
    b, t, d = x.shape
    lctx = ctx.shape[1]
    depth = ada_w.shape[0]
    n = b * t
    nctx = b * lctx

    cond = jnp.zeros((16, d), F32).at[:b].set(c).at[b].set(c_ctx)
    mods = []
    for l in range(depth):
        m = matmul(cond, ada_w, out_dtype=F32, bias=ada_b[l], silu_in=True, tn_cap=1024, layer=l)
        mods.append(m.reshape(16, 6, d))
    mod_x = lambda l, i: mods[l][:b, i]
    mod_c = lambda l, i: mods[l][b:b + 1, i]

    w_in = ab_w_in[0]
    qkv_w = DN_HEADS * HEAD_DIM * 3
    z_w = DN_HEADS * HEAD_DIM
    small = 4 * DN_HEADS
    off = qkv_w + z_w
    w_dn = jnp.concatenate([w_in[:, :off], w_in[:, off:off + small],
                            jnp.zeros((d, LANES - small), F32)], axis=1).astype(BF16)
    off += small
    dq_w = DIFF_HEADS * HEAD_DIM
    w_dqk = w_in[:, off:off + 2 * dq_w].astype(BF16)
    w_dv = w_in[:, off + 2 * dq_w:].astype(BF16)
    par = jnp.zeros((8, LANES), F32)
    par = par.at[0, 2 * DN_HEADS:4 * DN_HEADS].set(dn_dt_bias[0].reshape(-1))
    par = par.at[1, 2 * DN_HEADS:4 * DN_HEADS].set(dn_a_log[0].reshape(-1))
    lam_init = 0.8 - 0.6 * math.exp(-0.3 * 0)

    hx = normmod(x, norm_mix_w[0], mod_x(0, 0), mod_x(0, 1))
    hc = normmod(ctx, norm_mix_w[0], mod_c(0, 0), mod_c(0, 1))
    hx2 = hx.reshape(n, d)
    hc2 = hc.reshape(nctx, d)

    p_x = matmul(hx2, w_dn, out_dtype=F32).reshape(b, t, -1)
    p_c = matmul(hc2, w_dn, out_dtype=F32).reshape(b, lctx, -1)
    qkv_c = dn_conv(p_c, dn_conv_w[0])
    qkv_x = dn_conv(p_x, dn_conv_w[0])
    s0 = jnp.zeros((b, 2, DN_HEADS, HEAD_DIM, HEAD_DIM), F32)
    of_c, ob_c, s_c = dn_scan(qkv_c, p_c, par, s0)
    of_x, ob_x, _ = dn_scan(qkv_x, p_x, par, s_c)
    dn_x = dn_out(of_x, ob_x, p_x, dn_norm_w[0])
    dn_c = dn_out(of_c, ob_c, p_c, dn_norm_w[0])

    qk_x = matmul(hx2, w_dqk, out_dtype=F32).reshape(b, t, -1)
    qk_c = matmul(hc2, w_dqk, out_dtype=F32).reshape(b, lctx, -1)
    v_x = matmul(hx2, w_dv, out_dtype=BF16).reshape(b, t, -1)
    v_c = matmul(hc2, w_dv, out_dtype=BF16).reshape(b, lctx, -1)
    cos_d, sin_d = rope_tables(t, DIFF_DQK, DIFF_DQK // 4)
    one_c = jnp.ones((lctx, LANES), F32)
    zero_c = jnp.zeros((lctx, LANES), F32)
    dscale = DIFF_DQK ** -0.5 * math.log2(math.e)
    q1_x, q2_x = rope(qk_x, 0, dq_w, cos_d, sin_d, half=DIFF_DQK // 4, scale=dscale, split=True)
    k_x = rope(qk_x, 1, dq_w, cos_d, sin_d, half=DIFF_DQK // 4)
    q1_c, q2_c = rope(qk_c, 0, dq_w, one_c, zero_c, half=DIFF_DQK // 4, scale=dscale, split=True)
    k_c = rope(qk_c, 1, dq_w, one_c, zero_c, half=DIFF_DQK // 4)
    k_all = jnp.concatenate([k_x, k_c], axis=1)
    v_all = jnp.concatenate([v_x, v_c], axis=1)
    d_x = diff_attention(q1_x, q2_x, k_all, v_all, diff_lambda[0], diff_norm_w[0], lam_init)
    d_c = diff_attention(q1_c, q2_c, k_c, v_c, diff_lambda[0], diff_norm_w[0], lam_init)

    w_out = ab_w_out[0].astype(BF16)
    x, fx = _out_proj(jnp.concatenate([dn_x, d_x], axis=-1), w_out, x, mod_x(0, 2),
                      (norm_ffn_w[0], mod_x(0, 3), mod_x(0, 4)))
    ctx, fc = _out_proj(jnp.concatenate([dn_c, d_c], axis=-1), w_out, ctx, mod_c(0, 2),
                        (norm_ffn_w[0], mod_c(0, 3), mod_c(0, 4)))

    wq = peer_wq[0].astype(BF16)
    sub = peer_subkeys[0].astype(BF16)
    u_tab = peer_u[0].astype(BF16)
    vt_tab = peer_v[0].T.astype(BF16)
    x, hx = _peer_ffn(x, fx, mod_x(0, 5), wq, peer_bq[0], sub, u_tab, vt_tab,
                      norm=(norm_mix_w[1], mod_x(1, 0), mod_x(1, 1)))
    ctx, hc = _peer_ffn(ctx, fc, mod_c(0, 5), wq, peer_bq[0], sub, u_tab, vt_tab,
                        norm=(norm_mix_w[1], mod_c(1, 0), mod_c(1, 1)))

    gq_w = GQA_Q_HEADS * HEAD_DIM
    gkv_w = GQA_KV_HEADS * HEAD_DIM
    w_g = gqa_w_in[0]
    w_gqk = w_g[:, :gq_w + gkv_w].astype(BF16)
    w_gv = w_g[:, gq_w + gkv_w:].astype(BF16)
    w_gkv = w_g[:, gq_w:].astype(BF16)
    hx2 = hx.reshape(n, d)
    qk = matmul(hx2, w_gqk, out_dtype=F32, tn_cap=1280).reshape(b, t, -1)
    gv = matmul(hx2, w_gv, out_dtype=BF16).reshape(b, t, -1)
    kvx = matmul(hc.reshape(nctx, d), w_gkv, out_dtype=BF16).reshape(b, lctx, -1)
    cos_g, sin_g = rope_tables(t, HEAD_DIM, HEAD_DIM // 4)
    gq = rope(qk, 0, gq_w, cos_g, sin_g, half=HEAD_DIM // 4, scale=HEAD_DIM ** -0.5)
    gk = rope(qk, gq_w // gkv_w, gkv_w, cos_g, sin_g, half=HEAD_DIM // 4)
    att = window_attention(gq, gk, gv, kvx, gqa_sink[0])
    x, fx = _out_proj(att, gqa_w_out[0].astype(BF16), x, mod_x(1, 2),
                      (norm_ffn_w[1], mod_x(1, 3), mod_x(1, 4)))

    wq = peer_wq[1].astype(BF16)
    sub = peer_subkeys[1].astype(BF16)
    u_tab = peer_u[1].astype(BF16)
    vt_tab = peer_v[1].T.astype(BF16)
    return _peer_ffn(x, fx, mod_x(1, 5), wq, peer_bq[1], sub, u_tab, vt_tab, final_w=final_norm_w)
```

```python
import functools
import math

import jax
import jax.numpy as jnp
import numpy as np
from jax import lax
from jax.experimental import pallas as pl
from jax.experimental.pallas import tpu as pltpu

F32 = jnp.float32
BF16 = jnp.bfloat16

LANES = 128
HEAD_DIM = 128
NORM_EPS = 1e-6
ROPE_THETA = 10000.0
GRID_W = 64
DN_HEADS = 8
DN_CHUNK = 64
DN_CONV = 5
DIFF_HEADS = 8
DIFF_DQK = 64
GQA_Q_HEADS = 16
GQA_KV_HEADS = 4
GQA_GROUP = GQA_Q_HEADS // GQA_KV_HEADS
WINDOW = 128
PEER_HEADS = 8
PEER_NKEYS = 128
PEER_TOPK = 16
NEG_BIG = -1e30
VMEM_LIMIT = 56 * 1024 * 1024


def _params(sem):
    return pltpu.CompilerParams(dimension_semantics=sem, vmem_limit_bytes=VMEM_LIMIT)


def _dot(a, b):
    return jnp.dot(a.astype(BF16), b.astype(BF16), preferred_element_type=F32)


def _dot_nt(a, b):
    return lax.dot_general(a.astype(BF16), b.astype(BF16), (((1,), (1,)), ((), ())),
                           preferred_element_type=F32)


def _sigmoid(x):
    return 1.0 / (1.0 + jnp.exp(-x))


def _normmod_kernel(x_ref, w_ref, shift_ref, scale_ref, o_ref):
    x = x_ref[...]
    y = x * lax.rsqrt(jnp.mean(x * x, axis=-1, keepdims=True) + NORM_EPS)
    y = y * w_ref[...]
    o_ref[...] = (y * (1.0 + scale_ref[...]) + shift_ref[...]).astype(o_ref.dtype)


def normmod(x, w, shift, scale, rows=512):
    b, t, d = x.shape
    tr = min(rows, t)
    bm = shift.shape[0]
    mod_map = (lambda i, j: (i, 0, 0)) if bm == b else (lambda i, j: (0, 0, 0))
    return pl.pallas_call(
        _normmod_kernel,
        grid=(b, t // tr),
        in_specs=[pl.BlockSpec((None, tr, d), lambda i, j: (i, j, 0)),
                  pl.BlockSpec((1, d), lambda i, j: (0, 0)),
                  pl.BlockSpec((None, 1, d), mod_map),
                  pl.BlockSpec((None, 1, d), mod_map)],
        out_specs=pl.BlockSpec((None, tr, d), lambda i, j: (i, j, 0)),
        out_shape=jax.ShapeDtypeStruct((b, t, d), BF16),
        compiler_params=_params(("parallel", "parallel")),
        name="normmod",
    )(x, w.reshape(1, d), shift.reshape(bm, 1, d), scale.reshape(bm, 1, d))


def _normmod_rows(x, w, shift, scale):
    y = x * lax.rsqrt(jnp.mean(x * x, axis=-1, keepdims=True) + NORM_EPS) * w
    return y * (1.0 + scale) + shift


def _mm_kernel(*refs, silu_in, has_bias, has_res, has_norm):
    a_ref, w_ref = refs[0], refs[1]
    pos = 2
    a = a_ref[...]
    if silu_in:
        a = a * _sigmoid(a)
    acc = _dot(a, w_ref[...])
    if has_bias:
        acc = acc + refs[pos][...]
        pos += 1
    if has_res:
        acc = refs[pos][...] + refs[pos + 1][...] * acc
        pos += 2
    if has_norm:
        nw_ref, shift_ref, scale_ref = refs[pos:pos + 3]
        pos += 3
    o_ref = refs[pos]
    o_ref[...] = acc.astype(o_ref.dtype)
    if has_norm:
        h_ref = refs[pos + 1]
        h_ref[...] = _normmod_rows(acc, nw_ref[...], shift_ref[...], scale_ref[...]).astype(h_ref.dtype)


def _pick_tn(n, cap):
    best = LANES
    for cand in range(LANES, min(n, cap) + 1, LANES):
        if n % cand == 0:
            best = cand
    return best


def matmul(a, w, *, out_dtype, bias=None, res=None, gate=None, rows_per_gate=None,
           silu_in=False, tm=512, tn_cap=2048, layer=0, norm=None):
    m, k = a.shape
    n = w.shape[-1]
    tm = min(tm, m)
    tn = _pick_tn(n, tn_cap)
    assert norm is None or (tn == n and res is not None)
    if w.ndim == 3:
        w_spec = pl.BlockSpec((None, k, tn), lambda j, i: (layer, 0, j))
    else:
        w_spec = pl.BlockSpec((k, tn), lambda j, i: (0, j))
    in_specs = [pl.BlockSpec((tm, k), lambda j, i: (i, 0)), w_spec]
    args = [a, w]
    if bias is not None:
        in_specs.append(pl.BlockSpec((1, tn), lambda j, i: (0, j)))
        args.append(bias.reshape(1, n).astype(F32))
    if res is not None:
        blocks_per_gate = rows_per_gate // tm
        in_specs.append(pl.BlockSpec((tm, tn), lambda j, i: (i, j)))
        in_specs.append(pl.BlockSpec((None, 1, tn), lambda j, i: (i // blocks_per_gate, 0, j)))
        args += [res, gate.reshape(gate.shape[0], 1, n)]
    out_spec = pl.BlockSpec((tm, tn), lambda j, i: (i, j))
    out_specs, out_shape = out_spec, jax.ShapeDtypeStruct((m, n), out_dtype)
    if norm is not None:
        mod_spec = pl.BlockSpec((None, 1, tn), lambda j, i: (i // blocks_per_gate, 0, j))
        in_specs += [pl.BlockSpec((1, tn), lambda j, i: (0, j)), mod_spec, mod_spec]
        args += [norm[0].reshape(1, n), norm[1].reshape(-1, 1, n), norm[2].reshape(-1, 1, n)]
        out_specs = [out_spec, out_spec]
        out_shape = [out_shape, jax.ShapeDtypeStruct((m, n), BF16)]
    kern = functools.partial(_mm_kernel, silu_in=silu_in, has_bias=bias is not None,
                             has_res=res is not None, has_norm=norm is not None)
    return pl.pallas_call(
        kern,
        grid=(n // tn, m // tm),
        in_specs=in_specs,
        out_specs=out_specs,
        out_shape=out_shape,
        compiler_params=_params(("parallel", "parallel")),
        name="matmul",
    )(*args)


def _conv_kernel(prev_ref, cur_ref, next_ref, w_ref, o_ref, *, tt, nheads_blk):
    i = pl.program_id(1)
    grp = pl.program_id(2)
    last = pl.num_programs(1) - 1
    prev = prev_ref[...] * (i > 0).astype(F32)
    nxt = next_ref[...] * (i < last).astype(F32)
    xx = jnp.concatenate([prev, cur_ref[...], nxt], axis=0)
    rows = tt + 16
    pad = (DN_CONV - 1) // 2
    y = None
    for k in range(DN_CONV):
        shift = (pad - k) % rows
        z = xx if shift == 0 else pltpu.roll(xx, shift, 0)
        term = z[8:8 + tt] * w_ref[k:k + 1, :]
        y = term if y is None else y + term
    y = y * _sigmoid(y)
    qscale = jnp.where(grp == 0, HEAD_DIM ** -0.5, 1.0).astype(F32)
    for h in range(nheads_blk):
        yh = y[:, h * HEAD_DIM:(h + 1) * HEAD_DIM]
        ss = jnp.sum(yh * yh, axis=-1, keepdims=True)
        fac = jnp.where(grp == 2, 1.0, lax.rsqrt(ss + NORM_EPS) * qscale)
        o_ref[:, h * HEAD_DIM:(h + 1) * HEAD_DIM] = yh * fac


def dn_conv(p, conv_w, tt=256):
    b, t, _ = p.shape
    tt = min(tt, t)
    cb = DN_HEADS * HEAD_DIM
    n8 = t // 8
    kern = functools.partial(_conv_kernel, tt=tt, nheads_blk=DN_HEADS)
    return pl.pallas_call(
        kern,
        grid=(b, t // tt, 3),
        in_specs=[pl.BlockSpec((None, 8, cb), lambda bi, i, g: (bi, jnp.maximum(i * (tt // 8) - 1, 0), g)),
                  pl.BlockSpec((None, tt, cb), lambda bi, i, g: (bi, i, g)),
                  pl.BlockSpec((None, 8, cb), lambda bi, i, g: (bi, jnp.minimum((i + 1) * (tt // 8), n8 - 1), g)),
                  pl.BlockSpec((DN_CONV, cb), lambda bi, i, g: (0, g))],
        out_specs=pl.BlockSpec((None, tt, cb), lambda bi, i, g: (bi, i, g)),
        out_shape=jax.ShapeDtypeStruct((b, t, 3 * cb), F32),
        compiler_params=_params(("parallel", "parallel", "parallel")),
        name="dn_conv",
    )(p, p, p, conv_w)


def _split3(x):
    hi = x.astype(BF16)
    r1 = x - hi.astype(F32)
    mid = r1.astype(BF16)
    lo = (r1 - mid.astype(F32)).astype(BF16)
    return hi, mid, lo


def _dn_chains(direction, q_ref, k_ref, v_ref, p_ref, par_ref, o_ref, s_scr):
    c = DN_CHUNK
    ii = lax.broadcasted_iota(jnp.int32, (c, c), 0)
    jj = lax.broadcasted_iota(jnp.int32, (c, c), 1)
    incl = (ii >= jj) if direction == 0 else (ii <= jj)
    strict = (ii > jj) if direction == 0 else (ii < jj)

    raw = p_ref[...]
    beta_all = _sigmoid(raw)
    xa = raw + par_ref[0:1, :]
    softplus = jnp.maximum(xa, 0.0) + jnp.log1p(jnp.exp(-jnp.abs(xa)))
    g_all = -jnp.exp(par_ref[1:2, :]) * softplus
    inclb = incl.astype(BF16)
    hi, mid, lo = _split3(g_all)
    gc_all = (jnp.dot(inclb, hi, preferred_element_type=F32)
              + jnp.dot(inclb, mid, preferred_element_type=F32)
              + jnp.dot(inclb, lo, preferred_element_type=F32))
    gtot_all = jnp.sum(g_all, axis=0, keepdims=True)
    gc_t = gc_all.T

    chains = []
    for h in range(DN_HEADS):
        ib = direction * DN_HEADS + h
        ig = 2 * DN_HEADS + ib
        sl = slice(h * HEAD_DIM, (h + 1) * HEAD_DIM)
        gc_c = gc_all[:, ig:ig + 1]
        chains.append(dict(
            incl=incl, strict=strict, beta=beta_all[:, ib:ib + 1], gc_c=gc_c,
            decay=jnp.exp(jnp.where(incl, gc_c - gc_t[ig:ig + 1, :], -jnp.inf)),
            gtot=gtot_all[:, ig:ig + 1], q=q_ref[:, sl], k=k_ref[:, sl], v=v_ref[:, sl],
            o_ref=o_ref, sl=sl, state_idx=(direction, h)))
    return chains


def _dn_step(chains, s_scr):
    c = DN_CHUNK
    ii = lax.broadcasted_iota(jnp.int32, (c, c), 0)
    jj = lax.broadcasted_iota(jnp.int32, (c, c), 1)
    eye = (ii == jj).astype(F32)
    for ch in chains:
        ch["kb"] = ch["k"] * ch["beta"]
        ch["kq"] = _dot_nt(jnp.concatenate([ch["kb"], ch["q"]], axis=0), ch["k"])
    for ch in chains:
        a = jnp.where(ch["strict"], ch["kq"][:c] * ch["decay"], 0.0)
        ch["qk"] = jnp.where(ch["incl"], ch["kq"][c:] * ch["decay"], 0.0)
        ch["inv"] = eye - a
        ch["a_pow"] = a
    for _ in range(int(math.log2(c)) - 1):
        for ch in chains:
            ch["a_pow"] = _dot(ch["a_pow"], ch["a_pow"])
        for ch in chains:
            ch["inv"] = _dot(ch["inv"], eye + ch["a_pow"])
    for ch in chains:
        egc = jnp.exp(ch["gc_c"])
        ch["uw"] = _dot(ch["inv"], jnp.concatenate([ch["v"] * ch["beta"], ch["kb"] * egc], axis=1))
        ch["q_dec"] = ch["q"] * egc
    for ch in chains:
        ch["state"] = s_scr[ch["state_idx"]]
        w = ch["uw"][:, HEAD_DIM:]
        ch["ws"] = _dot(jnp.concatenate([w, ch["q_dec"]], axis=0), ch["state"])
    for ch in chains:
        ch["v_new"] = ch["uw"][:, :HEAD_DIM] - ch["ws"][:c]
        ch["o_ref"][:, ch["sl"]] = ch["ws"][c:] + _dot(ch["qk"], ch["v_new"])
    for ch in chains:
        k_dec = ch["k"] * jnp.exp(ch["gtot"] - ch["gc_c"])
        s_scr[ch["state_idx"]] = ch["state"] * jnp.exp(ch["gtot"]) + _dot(k_dec.T, ch["v_new"])


def _dn_scan_kernel(qf_ref, kf_ref, vf_ref, pf_ref, qb_ref, kb_ref, vb_ref, pb_ref, par_ref, s0_ref,
                    of_ref, ob_ref, sfin_ref, s_scr):
    n = pl.program_id(1)

    @pl.when(n == 0)
    def _():
        s_scr[...] = s0_ref[...]

    chains = (_dn_chains(0, qf_ref, kf_ref, vf_ref, pf_ref, par_ref, of_ref, s_scr)
              + _dn_chains(1, qb_ref, kb_ref, vb_ref, pb_ref, par_ref, ob_ref, s_scr))
    _dn_step(chains, s_scr)

    @pl.when(n == pl.num_programs(1) - 1)
    def _():
        sfin_ref[...] = s_scr[...]


def dn_scan(qkv, p, par, s0):
    b, t, _ = qkv.shape
    c = DN_CHUNK
    nc = t // c
    w = DN_HEADS * HEAD_DIM
    fwd = lambda col: (lambda bi, n: (bi, n, col))
    bwd = lambda col: (lambda bi, n: (bi, nc - 1 - n, col))
    blk = lambda f: pl.BlockSpec((None, c, w), f)
    small = lambda f: pl.BlockSpec((None, c, LANES), f)
    state_spec = pl.BlockSpec((None, 2, DN_HEADS, HEAD_DIM, HEAD_DIM), lambda bi, n: (bi, 0, 0, 0, 0))
    o_shape = jax.ShapeDtypeStruct((b, t, w), F32)
    return pl.pallas_call(
        _dn_scan_kernel,
        grid=(b, nc),
        in_specs=[blk(fwd(0)), blk(fwd(1)), blk(fwd(2)), small(fwd(32)),
                  blk(bwd(0)), blk(bwd(1)), blk(bwd(2)), small(bwd(32)),
                  pl.BlockSpec((8, LANES), lambda bi, n: (0, 0)), state_spec],
        out_specs=[blk(fwd(0)), blk(bwd(0)), state_spec],
        out_shape=[o_shape, o_shape, jax.ShapeDtypeStruct(s0.shape, F32)],
        scratch_shapes=[pltpu.VMEM((2, DN_HEADS, HEAD_DIM, HEAD_DIM), F32)],
        compiler_params=_params(("parallel", "arbitrary")),
        name="dn_scan",
    )(qkv, qkv, qkv, p, qkv, qkv, qkv, p, par, s0)


def _dn_out_kernel(of_ref, ob_ref, z_ref, w_ref, y_ref):
    o = of_ref[...] + ob_ref[...]
    z = z_ref[...]
    for h in range(DN_HEADS):
        sl = slice(h * HEAD_DIM, (h + 1) * HEAD_DIM)
        oh = o[:, sl]
        yh = oh * lax.rsqrt(jnp.mean(oh * oh, axis=-1, keepdims=True) + NORM_EPS) * w_ref[...]
        zh = z[:, sl]
        y_ref[:, sl] = (yh * (zh * _sigmoid(zh))).astype(y_ref.dtype)


def dn_out(o_f, o_b, p, norm_w, tr=256):
    b, t, w = o_f.shape
    tr = min(tr, t)
    return pl.pallas_call(
        _dn_out_kernel,
        grid=(b, t // tr),
        in_specs=[pl.BlockSpec((None, tr, w), lambda bi, i: (bi, i, 0)),
                  pl.BlockSpec((None, tr, w), lambda bi, i: (bi, i, 0)),
                  pl.BlockSpec((None, tr, w), lambda bi, i: (bi, i, 3)),
                  pl.BlockSpec((1, HEAD_DIM), lambda bi, i: (0, 0))],
        out_specs=pl.BlockSpec((None, tr, w), lambda bi, i: (bi, i, 0)),
        out_shape=jax.ShapeDtypeStruct((b, t, w), BF16),
        compiler_params=_params(("parallel", "parallel")),
        name="dn_out",
    )(o_f, o_b, p, norm_w.reshape(1, HEAD_DIM))


def _rope_kernel(x_ref, cos_ref, sin_ref, *o_refs, half, scale, nheads, split):
    lane = lax.broadcasted_iota(jnp.int32, (1, LANES), 1)
    first = (lane % (2 * half)) < half
    cos = cos_ref[...]
    sin = sin_ref[...]
    for h in range(nheads):
        sl = slice(h * LANES, (h + 1) * LANES)
        x = x_ref[:, sl]
        partner = jnp.where(first, pltpu.roll(x, LANES - half, 1), pltpu.roll(x, half, 1))
        y = (x * cos + partner * sin) * scale
        if split:
            o_refs[0][:, sl] = jnp.where(lane < LANES // 2, y, 0.0).astype(BF16)
            o_refs[1][:, sl] = jnp.where(lane >= LANES // 2, y, 0.0).astype(BF16)
        else:
            o_refs[0][:, sl] = y.astype(BF16)


def rope(x, col_block, width, cos, sin, *, half, scale=1.0, split=False, tr=512):
    b, t, _ = x.shape
    tr = min(tr, t)
    nheads = width // LANES
    nout = 2 if split else 1
    kern = functools.partial(_rope_kernel, half=half, scale=scale, nheads=nheads, split=split)
    out = pl.pallas_call(
        kern,
        grid=(b, t // tr),
        in_specs=[pl.BlockSpec((None, tr, width), lambda bi, i: (bi, i, col_block)),
                  pl.BlockSpec((tr, LANES), lambda bi, i: (i, 0)),
                  pl.BlockSpec((tr, LANES), lambda bi, i: (i, 0))],
        out_specs=[pl.BlockSpec((None, tr, width), lambda bi, i: (bi, i, 0))] * nout,
        out_shape=[jax.ShapeDtypeStruct((b, t, width), BF16)] * nout,
        compiler_params=_params(("parallel", "parallel")),
        name="rope",
    )(x, cos, sin)
    return out if split else out[0]


def rope_tables(t, block, half):
    pos = jnp.arange(t, dtype=jnp.int32)
    rows = (pos // GRID_W).astype(F32)
    cols = (pos % GRID_W).astype(F32)
    lane = np.arange(LANES)
    j = lane % block
    use_col = (j // (2 * half)) == 1
    i = j % (2 * half)
    inv = (ROPE_THETA ** (-jnp.arange(half, dtype=F32) / half))[i % half]
    p = jnp.where(jnp.asarray(use_col)[None, :], cols[:, None], rows[:, None])
    ang = p * inv[None, :]
    sign = jnp.asarray(np.where(i < half, -1.0, 1.0).astype(np.float32))[None, :]
    return jnp.cos(ang), jnp.sin(ang) * sign


def _diff_kernel(q1_ref, q2_ref, k_ref, vt_ref, lam_ref, nw_ref, o_ref, acc1, acc2, s_scr,
                 *, lam_init, tk, nkv):
    tq = q1_ref.shape[0]
    acc1[...] = jnp.zeros_like(acc1)
    acc2[...] = jnp.zeros_like(acc2)

    def scores(c, slot):
        kc = k_ref[pl.ds(pl.multiple_of(c * tk, tk), tk), :]
        s_scr[slot, 0] = _dot_nt(kc, q1_ref[...])
        s_scr[slot, 1] = _dot_nt(kc, q2_ref[...])

    def update(s, vtc, m, l, acc):
        m_new = jnp.maximum(m, jnp.max(s, axis=0, keepdims=True))
        alpha = jnp.exp2(m - m_new)
        p = jnp.exp2(s - m_new)
        l_new = l * alpha + jnp.sum(p, axis=0, keepdims=True)
        acc[...] = acc[...] * alpha + _dot(vtc, p)
        return m_new, l_new

    def softmax(c, slot, carry):
        m1, l1, m2, l2 = carry
        vtc = vt_ref[c]
        m1, l1 = update(s_scr[slot, 0], vtc, m1, l1, acc1)
        m2, l2 = update(s_scr[slot, 1], vtc, m2, l2, acc2)
        return m1, l1, m2, l2

    def body(i, carry):
        c = 2 * i
        scores(c + 1, 1)
        carry = softmax(c, 0, carry)
        scores(c + 2, 0)
        return softmax(c + 1, 1, carry)

    neg = jnp.full((1, tq), -jnp.inf, F32)
    zero = jnp.zeros((1, tq), F32)
    scores(0, 0)
    pairs = (nkv - 1) // 2
    carry = lax.fori_loop(0, pairs, body, (neg, zero, neg, zero))
    if nkv % 2 == 0:
        scores(nkv - 1, 1)
        carry = softmax(nkv - 2, 0, carry)
        carry = softmax(nkv - 1, 1, carry)
    else:
        carry = softmax(nkv - 1, 0, carry)
    m1, l1, m2, l2 = carry

    lv = lam_ref[...]
    s01 = jnp.sum(lv[0:1] * lv[1:2], axis=-1, keepdims=True)
    s23 = jnp.sum(lv[2:3] * lv[3:4], axis=-1, keepdims=True)
    lam = jnp.exp(s01) - jnp.exp(s23) + lam_init
    o = acc1[...] / l1 - lam * (acc2[...] / l2)
    y = o * lax.rsqrt(jnp.mean(o * o, axis=0, keepdims=True) + NORM_EPS)
    o_ref[...] = (y.T * nw_ref[...] * (1.0 - lam_init)).astype(o_ref.dtype)


def diff_attention(q1, q2, k, v, lam_vec, norm_w, lam_init, tq=1024, tk=1408):
    b, tqa, w = q1.shape
    tka = k.shape[1]
    tq = min(tq, tqa)
    tk = min(tk, tka)
    nkv = tka // tk
    nh = w // HEAD_DIM
    vt = v.reshape(b, nkv, tk, nh, HEAD_DIM).transpose(0, 3, 1, 4, 2)
    kern = functools.partial(_diff_kernel, lam_init=lam_init, tk=tk, nkv=nkv)
    return pl.pallas_call(
        kern,
        grid=(b, nh, tqa // tq),
        in_specs=[pl.BlockSpec((None, tq, HEAD_DIM), lambda bi, h, i: (bi, i, h)),
                  pl.BlockSpec((None, tq, HEAD_DIM), lambda bi, h, i: (bi, i, h)),
                  pl.BlockSpec((None, tka, HEAD_DIM), lambda bi, h, i: (bi, 0, h)),
                  pl.BlockSpec((None, None, nkv, HEAD_DIM, tk), lambda bi, h, i: (bi, h, 0, 0, 0)),
                  pl.BlockSpec((4, DIFF_DQK), lambda bi, h, i: (0, 0)),
                  pl.BlockSpec((1, HEAD_DIM), lambda bi, h, i: (0, 0))],
        out_specs=pl.BlockSpec((None, tq, HEAD_DIM), lambda bi, h, i: (bi, i, h)),
        out_shape=jax.ShapeDtypeStruct((b, tqa, w), BF16),
        scratch_shapes=[pltpu.VMEM((HEAD_DIM, tq), F32)] * 2 + [pltpu.VMEM((2, 2, tk, tq), F32)],
        compiler_params=_params(("parallel", "parallel", "parallel")),
        name="diff_attention",
    )(q1, q2, k, vt, lam_vec, norm_w.reshape(1, HEAD_DIM))


def _win_kernel(sink_ref, q_ref, kp_ref, kc_ref, kn_ref, vp_ref, vc_ref, vn_ref, kx_ref, vx_ref, o_ref,
                *, t_total):
    kvh = pl.program_id(1)
    n = pl.program_id(2)
    wb = WINDOW
    tq = q_ref.shape[0]
    keys = jnp.concatenate([kp_ref[...], kc_ref[...], kn_ref[...], kx_ref[...]], axis=0)
    vals = jnp.concatenate([vp_ref[...], vc_ref[...], vn_ref[...], vx_ref[...]], axis=0)
    nk = keys.shape[0]
    nlocal = tq + 2 * wb
    i = lax.broadcasted_iota(jnp.int32, (tq, nk), 0)
    j = lax.broadcasted_iota(jnp.int32, (tq, nk), 1)
    kpos = n * tq - wb + j
    valid = (jnp.abs(j - wb - i) <= WINDOW) & (kpos >= 0) & (kpos < t_total)
    valid = valid | (j >= nlocal)
    for g in range(GQA_GROUP):
        sl = slice(g * HEAD_DIM, (g + 1) * HEAD_DIM)
        sink = sink_ref[kvh * GQA_GROUP + g]
        s = jnp.where(valid, _dot_nt(q_ref[:, sl], keys), -jnp.inf)
        m = jnp.maximum(jnp.max(s, axis=-1, keepdims=True), sink)
        p = jnp.exp(s - m)
        denom = jnp.sum(p, axis=-1, keepdims=True) + jnp.exp(sink - m)
        o_ref[:, sl] = (_dot(p, vals) / denom).astype(o_ref.dtype)


def window_attention(q, k, v, kvx, sink, tq=512):
    b, t, _ = q.shape
    lctx = kvx.shape[1]
    wb = WINDOW
    tq = min(tq, t)
    per = tq // wb
    nb = t // wb
    qw = GQA_GROUP * HEAD_DIM
    kern = functools.partial(_win_kernel, t_total=t)
    prev = lambda bi, h, n: (bi, jnp.maximum(n * per - 1, 0), h)
    cur = lambda bi, h, n: (bi, n, h)
    nxt = lambda bi, h, n: (bi, jnp.minimum((n + 1) * per, nb - 1), h)
    edge_spec = lambda f: pl.BlockSpec((None, wb, HEAD_DIM), f)
    main_spec = pl.BlockSpec((None, tq, HEAD_DIM), cur)
    return pl.pallas_call(
        kern,
        grid=(b, GQA_KV_HEADS, t // tq),
        in_specs=[pl.BlockSpec(memory_space=pltpu.SMEM),
                  pl.BlockSpec((None, tq, qw), cur),
                  edge_spec(prev), main_spec, edge_spec(nxt),
                  edge_spec(prev), main_spec, edge_spec(nxt),
                  pl.BlockSpec((None, lctx, HEAD_DIM), lambda bi, h, n: (bi, 0, h)),
                  pl.BlockSpec((None, lctx, HEAD_DIM), lambda bi, h, n: (bi, 0, GQA_KV_HEADS + h))],
        out_specs=pl.BlockSpec((None, tq, qw), cur),
        out_shape=jax.ShapeDtypeStruct(q.shape, BF16),
        compiler_params=_params(("parallel", "parallel", "parallel")),
        name="window_attention",
    )(sink, q, k, k, k, v, v, v, kvx, kvx)


RANK_NONE = 255.0
RANK_CODE_BASE = 1e30
RANK_CODE_STEP = 1e28


def _top_values(s, count, want_rank=False):
    vals = []
    work = s
    for r in range(count):
        m = jnp.max(work, axis=0, keepdims=True)
        vals.append(m)
        work = jnp.where(work == m, -(RANK_CODE_BASE + r * RANK_CODE_STEP), work)
    if not want_rank:
        return vals
    rank = jnp.where(work < -0.5 * RANK_CODE_BASE,
                     jnp.round((-work - RANK_CODE_BASE) * (1.0 / RANK_CODE_STEP)), RANK_NONE)
    return vals, rank


def _route_kernel(q_ref, sub_ref, cut_ref, e1_ref, r2_ref, e2_ref):
    kk = PEER_TOPK
    tb = q_ref.shape[0]
    row = lax.broadcasted_iota(jnp.int32, (kk, tb), 0)
    for h in range(PEER_HEADS):
        scores, tops, ranks = [], [], []
        for p in range(2):
            c0 = (h * 2 + p) * PEER_NKEYS
            s = _dot_nt(sub_ref[h, p], q_ref[:, c0:c0 + PEER_NKEYS])
            vals, rank = _top_values(s, kk, want_rank=True)
            scores.append(s)
            tops.append(vals)
            ranks.append(rank)
        v1 = jnp.zeros((kk, tb), F32)
        v2 = jnp.zeros((kk, tb), F32)
        for i in range(kk):
            v1 = jnp.where(row == i, tops[0][i], v1)
            v2 = jnp.where(row == i, tops[1][i], v2)
        mid = jnp.where(row[:8] >= 4, v1[:8], -jnp.inf)
        cand = jnp.concatenate([tops[0][0] + v2] + [tops[0][i] + v2[:8] for i in range(1, 4)]
                               + [mid + tops[1][j] for j in range(3)] + [v1[8:] + tops[1][0]], axis=0)
        thr = _top_values(cand, kk)[kk - 1]
        top = tops[0][0] + tops[1][0]
        z = jnp.sum(jnp.where(cand >= thr, jnp.exp(cand - top), 0.0), axis=0, keepdims=True)
        gdt = r2_ref.dtype
        rank1 = ranks[0].astype(gdt)
        cut = jnp.zeros(rank1.shape, gdt)
        for i in range(kk):
            count = jnp.sum(((tops[0][i] + v2) >= thr).astype(F32), axis=0, keepdims=True)
            cut = jnp.where(rank1 == i, count.astype(gdt), cut)
        in1 = ranks[0] < float(kk)
        in2 = ranks[1] < float(kk)
        cut_ref[h] = cut.astype(F32)
        e1_ref[h] = jnp.where(in1, jnp.exp(scores[0] - tops[0][0]), 0.0)
        r2_ref[h] = ranks[1].astype(r2_ref.dtype)
        e2_ref[h] = jnp.where(in2, jnp.exp(scores[1] - tops[1][0]) / z, 0.0).astype(e2_ref.dtype)


def peer_route(q, subkeys, tb=128):
    n = q.shape[0]
    tb = min(tb, n)
    hh = PEER_HEADS
    tab32 = jax.ShapeDtypeStruct((hh, PEER_NKEYS, n), F32)
    tab16 = jax.ShapeDtypeStruct((hh, PEER_NKEYS, n), BF16)
    tab_spec = pl.BlockSpec((hh, PEER_NKEYS, tb), lambda i: (0, 0, i))
    return pl.pallas_call(
        _route_kernel,
        grid=(n // tb,),
        in_specs=[pl.BlockSpec((tb, q.shape[1]), lambda i: (i, 0)),
                  pl.BlockSpec(subkeys.shape, lambda i: (0, 0, 0, 0))],
        out_specs=[tab_spec] * 4,
        out_shape=[tab32, tab32, tab16, tab16],
        compiler_params=_params(("parallel",)),
        name="peer_route",
    )(q, subkeys)


def _gelu(x):
    return 0.5 * x * (1.0 + lax.erf(x * (2.0 ** -0.5)))


def _peer_kernel(xt_ref, u_ref, vt_ref, cut_ref, e1_ref, r2_ref, e2_ref, o_ref, acc, g_scr, h_scr, a_scr,
                 *, rows, nparts):
    et = pl.program_id(1)

    @pl.when(et == 0)
    def _():
        acc[...] = jnp.zeros_like(acc)

    nk = PEER_NKEYS
    gdt = r2_ref.dtype
    half = rows // nparts
    pk = half * nk

    def gate_row(r):
        gate = None
        for h in range(PEER_HEADS):
            cut = cut_ref[h, r:r + 1, :].astype(gdt)
            e1 = e1_ref[h, r:r + 1, :].astype(gdt)
            g = jnp.where(r2_ref[h] < cut, e1 * e2_ref[h], jnp.zeros((), gdt))
            gate = g if gate is None else gate + g
        return gate

    def hidden(part):
        return jnp.dot(u_ref[part * pk:(part + 1) * pk, :], xt_ref[...], preferred_element_type=F32)

    def activate(ht, gate):
        return (_gelu(ht).astype(gdt) * gate).astype(BF16)

    def project(part, a):
        return jnp.dot(vt_ref[:, part * pk:(part + 1) * pk], a, preferred_element_type=F32)

    for r in range(half):
        g_scr[r * nk:(r + 1) * nk, :] = gate_row(r)
    h_scr[...] = hidden(0)
    for part in range(nparts):
        if part + 1 < nparts:
            h_next = hidden(part + 1)
        a = activate(h_scr[...], g_scr[...]) if part == 0 else a_scr[...]
        acc[...] += project(part, a)
        if part + 1 < nparts:
            for r in range(half):
                a_scr[r * nk:(r + 1) * nk, :] = activate(h_next[r * nk:(r + 1) * nk, :],
                                                        gate_row((part + 1) * half + r))

    @pl.when(et == pl.num_programs(1) - 1)
    def _():
        o_ref[...] = acc[...].T


def peer_experts(xt, u, vt, cut, e1, r2, e2, tb=512, rows=8, nparts=2):
    d, n = xt.shape
    e = u.shape[0]
    tb = min(tb, n)
    te = rows * PEER_NKEYS
    hh = PEER_HEADS
    kern = functools.partial(_peer_kernel, rows=rows, nparts=nparts)
    row_spec = pl.BlockSpec((hh, rows, tb), lambda i, j: (0, j, i))
    full_spec = pl.BlockSpec((hh, PEER_NKEYS, tb), lambda i, j: (0, 0, i))
    return pl.pallas_call(
        kern,
        grid=(n // tb, e // te),
        in_specs=[pl.BlockSpec((d, tb), lambda i, j: (0, i)),
                  pl.BlockSpec((te, d), lambda i, j: (j, 0)),
                  pl.BlockSpec((d, te), lambda i, j: (0, j)),
                  row_spec, row_spec, full_spec, full_spec],
        out_specs=pl.BlockSpec((tb, d), lambda i, j: (i, 0)),
        out_shape=jax.ShapeDtypeStruct((n, d), F32),
        scratch_shapes=[pltpu.VMEM((d, tb), F32), pltpu.VMEM((te // nparts, tb), r2.dtype),
                        pltpu.VMEM((te // nparts, tb), F32), pltpu.VMEM((te // nparts, tb), BF16)],
        compiler_params=_params(("parallel", "arbitrary")),
        name="peer_experts",
    )(xt, u, vt, cut, e1, r2, e2)


def _resid_kernel(*refs, final, has_norm):
    x_ref, y_ref, g_ref, w_ref = refs[:4]
    x = x_ref[...] + g_ref[...] * y_ref[...]
    if final:
        x = x * lax.rsqrt(jnp.mean(x * x, axis=-1, keepdims=True) + NORM_EPS) * w_ref[...]
    if has_norm:
        shift_ref, scale_ref, o_ref, h_ref = refs[4:]
        h_ref[...] = _normmod_rows(x, w_ref[...], shift_ref[...], scale_ref[...]).astype(h_ref.dtype)
    else:
        o_ref = refs[4]
    o_ref[...] = x


def resid(x, y, gate, final_w=None, norm=None, rows=512):
    b, t, d = x.shape
    tr = min(rows, t)
    bm = gate.shape[0]
    mod_map = (lambda i, j: (i, 0, 0)) if bm == b else (lambda i, j: (0, 0, 0))
    mod_spec = pl.BlockSpec((None, 1, d), mod_map)
    w = final_w if final_w is not None else (norm[0] if norm is not None else jnp.ones((d,), F32))
    kern = functools.partial(_resid_kernel, final=final_w is not None, has_norm=norm is not None)
    blk = pl.BlockSpec((None, tr, d), lambda i, j: (i, j, 0))
    in_specs = [blk, blk, mod_spec, pl.BlockSpec((1, d), lambda i, j: (0, 0))]
    args = [x, y, gate.reshape(bm, 1, d), w.reshape(1, d)]
    out_specs, out_shape = blk, jax.ShapeDtypeStruct((b, t, d), F32)
    if norm is not None:
        in_specs += [mod_spec, mod_spec]
        args += [norm[1].reshape(bm, 1, d), norm[2].reshape(bm, 1, d)]
        out_specs = [blk, blk]
        out_shape = [out_shape, jax.ShapeDtypeStruct((b, t, d), BF16)]
    return pl.pallas_call(
        kern,
        grid=(b, t // tr),
        in_specs=in_specs,
        out_specs=out_specs,
        out_shape=out_shape,
        compiler_params=_params(("parallel", "parallel")),
        name="resid",
    )(*args)


def _peer_ffn(x, fx, gate, wq, bq, subkeys, u, vt, final_w=None, norm=None):
    b, t, d = x.shape
    n = b * t
    q = matmul(fx.reshape(n, d), wq, out_dtype=BF16, bias=bq)
    cut, e1, r2, e2 = peer_route(q, subkeys)
    y = peer_experts(fx.reshape(n, d).T, u, vt, cut, e1, r2, e2)
    return resid(x, y.reshape(b, t, d), gate, final_w=final_w, norm=norm)


def _out_proj(a, w_out, x, gate, norm):
    b, t, d = x.shape
    per_batch = lambda v: v if v.shape[0] == b else jnp.broadcast_to(v, (b, d))
    y, h = matmul(a.reshape(b * t, a.shape[-1]), w_out, out_dtype=F32, res=x.reshape(b * t, d),
                  gate=per_batch(gate), rows_per_gate=t, tm=min(512, t),
                  norm=(norm[0], per_batch(norm[1]), per_batch(norm[2])))
    return y.reshape(b, t, d), h.reshape(b, t, d)


def kernel(x, c, ctx, c_ctx, ada_w, ada_b, norm_mix_w, norm_ffn_w, ab_w_in, ab_w_out, dn_conv_w, dn_a_log,
           dn_dt_bias, dn_norm_w, diff_lambda, diff_norm_w, gqa_w_in, gqa_w_out, gqa_sink, peer_wq, peer_bq,
           peer_subkeys, peer_u, peer_v, final_norm_w):
    b, t, d = x.shape
    lctx = ctx.shape[1]
    depth = ada_w.shape[0]
    n = b * t
    nctx = b * lctx

    cond = jnp.zeros((16, d), F32).at[:b].set(c).at[b].set(c_ctx)
    mods = []
    for l in range(depth):
        m = matmul(cond, ada_w, out_dtype=F32, bias=ada_b[l], silu_in=True, tn_cap=1024, layer=l)
        mods.append(m.reshape(16, 6, d))
    mod_x = lambda l, i: mods[l][:b, i]
    mod_c = lambda l, i: mods[l][b:b + 1, i]

    w_in = ab_w_in[0]
    qkv_w = DN_HEADS * HEAD_DIM * 3
    z_w = DN_HEADS * HEAD_DIM
    small = 4 * DN_HEADS
    off = qkv_w + z_w
    w_dn = jnp.concatenate([w_in[:, :off], w_in[:, off:off + small],
                            jnp.zeros((d, LANES - small), F32)], axis=1).astype(BF16)
    off += small
    dq_w = DIFF_HEADS * HEAD_DIM
    w_dqk = w_in[:, off:off + 2 * dq_w].astype(BF16)
    w_dv = w_in[:, off + 2 * dq_w:].astype(BF16)
    par = jnp.zeros((8, LANES), F32)
    par = par.at[0, 2 * DN_HEADS:4 * DN_HEADS].set(dn_dt_bias[0].reshape(-1))
    par = par.at[1, 2 * DN_HEADS:4 * DN_HEADS].set(dn_a_log[0].reshape(-1))
    lam_init = 0.8 - 0.6 * math.exp(-0.3 * 0)

    hx = normmod(x, norm_mix_w[0], mod_x(0, 0), mod_x(0, 1))
    hc = normmod(ctx, norm_mix_w[0], mod_c(0, 0), mod_c(0, 1))
    hx2 = hx.reshape(n, d)
    hc2 = hc.reshape(nctx, d)

    p_x = matmul(hx2, w_dn, out_dtype=F32).reshape(b, t, -1)
    p_c = matmul(hc2, w_dn, out_dtype=F32).reshape(b, lctx, -1)
    qkv_c = dn_conv(p_c, dn_conv_w[0])
    qkv_x = dn_conv(p_x, dn_conv_w[0])
    s0 = jnp.zeros((b, 2, DN_HEADS, HEAD_DIM, HEAD_DIM), F32)
    of_c, ob_c, s_c = dn_scan(qkv_c, p_c, par, s0)
    of_x, ob_x, _ = dn_scan(qkv_x, p_x, par, s_c)
    dn_x = dn_out(of_x, ob_x, p_x, dn_norm_w[0])
    dn_c = dn_out(of_c, ob_c, p_c, dn_norm_w[0])

    qk_x = matmul(hx2, w_dqk, out_dtype=F32).reshape(b, t, -1)
    qk_c = matmul(hc2, w_dqk, out_dtype=F32).reshape(b, lctx, -1)
    v_x = matmul(hx2, w_dv, out_dtype=BF16).reshape(b, t, -1)
    v_c = matmul(hc2, w_dv, out_dtype=BF16).reshape(b, lctx, -1)
    cos_d, sin_d = rope_tables(t, DIFF_DQK, DIFF_DQK // 4)
    one_c = jnp.ones((lctx, LANES), F32)
    zero_c = jnp.zeros((lctx, LANES), F32)
    dscale = DIFF_DQK ** -0.5 * math.log2(math.e)
    q1_x, q2_x = rope(qk_x, 0, dq_w, cos_d, sin_d, half=DIFF_DQK // 4, scale=dscale, split=True)
    k_x = rope(qk_x, 1, dq_w, cos_d, sin_d, half=DIFF_DQK // 4)
    q1_c, q2_c = rope(qk_c, 0, dq_w, one_c, zero_c, half=DIFF_DQK // 4, scale=dscale, split=True)
    k_c = rope(qk_c, 1, dq_w, one_c, zero_c, half=DIFF_DQK // 4)
    k_all = jnp.concatenate([k_x, k_c], axis=1)
    v_all = jnp.concatenate([v_x, v_c], axis=1)
    d_x = diff_attention(q1_x, q2_x, k_all, v_all, diff_lambda[0], diff_norm_w[0], lam_init)
    d_c = diff_attention(q1_c, q2_c, k_c, v_c, diff_lambda[0], diff_norm_w[0], lam_init)

    w_out = ab_w_out[0].astype(BF16)
    x, fx = _out_proj(jnp.concatenate([dn_x, d_x], axis=-1), w_out, x, mod_x(0, 2),
                      (norm_ffn_w[0], mod_x(0, 3), mod_x(0, 4)))
    ctx, fc = _out_proj(jnp.concatenate([dn_c, d_c], axis=-1), w_out, ctx, mod_c(0, 2),
                        (norm_ffn_w[0], mod_c(0, 3), mod_c(0, 4)))

    wq = peer_wq[0].astype(BF16)
    sub = peer_subkeys[0].astype(BF16)
    u_tab = peer_u[0].astype(BF16)
    vt_tab = peer_v[0].T.astype(BF16)
    x, hx = _peer_ffn(x, fx, mod_x(0, 5), wq, peer_bq[0], sub, u_tab, vt_tab,
                      norm=(norm_mix_w[1], mod_x(1, 0), mod_x(1, 1)))
    ctx, hc = _peer_ffn(ctx, fc, mod_c(0, 5), wq, peer_bq[0], sub, u_tab, vt_tab,
                        norm=(norm_mix_w[1], mod_c(1, 0), mod_c(1, 1)))

    gq_w = GQA_Q_HEADS * HEAD_DIM
    gkv_w = GQA_KV_HEADS * HEAD_DIM
    w_g = gqa_w_in[0]
    w_gqk = w_g[:, :gq_w + gkv_w].astype(BF16)
    w_gv = w_g[:, gq_w + gkv_w:].astype(BF16)
    w_gkv = w_g[:, gq_w:].astype(BF16)
    hx2 = hx.reshape(n, d)
    qk = matmul(hx2, w_gqk, out_dtype=F32, tn_cap=1280).reshape(b, t, -1)
    gv = matmul(hx2, w_gv, out_dtype=BF16).reshape(b, t, -1)
    kvx = matmul(hc.reshape(nctx, d), w_gkv, out_dtype=BF16).reshape(b, lctx, -1)
    cos_g, sin_g = rope_tables(t, HEAD_DIM, HEAD_DIM // 4)
    gq = rope(qk, 0, gq_w, cos_g, sin_g, half=HEAD_DIM // 4, scale=HEAD_DIM ** -0.5)
    gk = rope(qk, gq_w // gkv_w, gkv_w, cos_g, sin_g, half=HEAD_DIM // 4)
    att = window_attention(gq, gk, gv, kvx, gqa_sink[0])
    x, fx = _out_proj(att, gqa_w_out[0].astype(BF16), x, mod_x(1, 2),
                      (norm_ffn_w[1], mod_x(1, 3), mod_x(1, 4)))

    wq = peer_wq[1].astype(BF16)
    sub = peer_subkeys[1].astype(BF16)
    u_tab = peer_u[1].astype(BF16)
    vt_tab = peer_v[1].T.astype(BF16)
    return _peer_ffn(x, fx, mod_x(1, 5), wq, peer_bq[1], sub, u_tab, vt_tab, final_w=final_norm_w)
```

```python
import functools
import math

import jax
import jax.numpy as jnp
import numpy as np
from jax import lax
from jax.experimental import pallas as pl
from jax.experimental.pallas import tpu as pltpu

F32 = jnp.float32
BF16 = jnp.bfloat16

LANES = 128
HEAD_DIM = 128
NORM_EPS = 1e-6
ROPE_THETA = 10000.0
GRID_W = 64
DN_HEADS = 8
DN_CHUNK = 64
DN_CONV = 5
DIFF_HEADS = 8
DIFF_DQK = 64
GQA_Q_HEADS = 16
GQA_KV_HEADS = 4
GQA_GROUP = GQA_Q_HEADS // GQA_KV_HEADS
WINDOW = 128
PEER_HEADS = 8
PEER_NKEYS = 128
PEER_TOPK = 16
NEG_BIG = -1e30
VMEM_LIMIT = 56 * 1024 * 1024


def _params(sem):
    return pltpu.CompilerParams(dimension_semantics=sem, vmem_limit_bytes=VMEM_LIMIT)


def _dot(a, b):
    return jnp.dot(a.astype(BF16), b.astype(BF16), preferred_element_type=F32)


def _dot_nt(a, b):
    return lax.dot_general(a.astype(BF16), b.astype(BF16), (((1,), (1,)), ((), ())),
                           preferred_element_type=F32)


def _sigmoid(x):
    return 1.0 / (1.0 + jnp.exp(-x))


def _normmod_kernel(x_ref, w_ref, shift_ref, scale_ref, o_ref):
    x = x_ref[...]
    y = x * lax.rsqrt(jnp.mean(x * x, axis=-1, keepdims=True) + NORM_EPS)
    y = y * w_ref[...]
    o_ref[...] = (y * (1.0 + scale_ref[...]) + shift_ref[...]).astype(o_ref.dtype)


def normmod(x, w, shift, scale, rows=512):
    b, t, d = x.shape
    tr = min(rows, t)
    bm = shift.shape[0]
    mod_map = (lambda i, j: (i, 0, 0)) if bm == b else (lambda i, j: (0, 0, 0))
    return pl.pallas_call(
        _normmod_kernel,
        grid=(b, t // tr),
        in_specs=[pl.BlockSpec((None, tr, d), lambda i, j: (i, j, 0)),
                  pl.BlockSpec((1, d), lambda i, j: (0, 0)),
                  pl.BlockSpec((None, 1, d), mod_map),
                  pl.BlockSpec((None, 1, d), mod_map)],
        out_specs=pl.BlockSpec((None, tr, d), lambda i, j: (i, j, 0)),
        out_shape=jax.ShapeDtypeStruct((b, t, d), BF16),
        compiler_params=_params(("parallel", "parallel")),
        name="normmod",
    )(x, w.reshape(1, d), shift.reshape(bm, 1, d), scale.reshape(bm, 1, d))


def _normmod_rows(x, w, shift, scale):
    y = x * lax.rsqrt(jnp.mean(x * x, axis=-1, keepdims=True) + NORM_EPS) * w
    return y * (1.0 + scale) + shift


def _mm_kernel(*refs, silu_in, has_bias, has_res, has_norm):
    a_ref, w_ref = refs[0], refs[1]
    pos = 2
    a = a_ref[...]
    if silu_in:
        a = a * _sigmoid(a)
    acc = _dot(a, w_ref[...])
    if has_bias:
        acc = acc + refs[pos][...]
        pos += 1
    if has_res:
        acc = refs[pos][...] + refs[pos + 1][...] * acc
        pos += 2
    if has_norm:
        nw_ref, shift_ref, scale_ref = refs[pos:pos + 3]
        pos += 3
    o_ref = refs[pos]
    o_ref[...] = acc.astype(o_ref.dtype)
    if has_norm:
        h_ref = refs[pos + 1]
        h_ref[...] = _normmod_rows(acc, nw_ref[...], shift_ref[...], scale_ref[...]).astype(h_ref.dtype)


def _pick_tn(n, cap):
    best = LANES
    for cand in range(LANES, min(n, cap) + 1, LANES):
        if n % cand == 0:
            best = cand
    return best


def matmul(a, w, *, out_dtype, bias=None, res=None, gate=None, rows_per_gate=None,
           silu_in=False, tm=512, tn_cap=2048, layer=0, norm=None):
    m, k = a.shape
    n = w.shape[-1]
    tm = min(tm, m)
    tn = _pick_tn(n, tn_cap)
    assert norm is None or (tn == n and res is not None)
    if w.ndim == 3:
        w_spec = pl.BlockSpec((None, k, tn), lambda j, i: (layer, 0, j))
    else:
        w_spec = pl.BlockSpec((k, tn), lambda j, i: (0, j))
    in_specs = [pl.BlockSpec((tm, k), lambda j, i: (i, 0)), w_spec]
    args = [a, w]
    if bias is not None:
        in_specs.append(pl.BlockSpec((1, tn), lambda j, i: (0, j)))
        args.append(bias.reshape(1, n).astype(F32))
    if res is not None:
        blocks_per_gate = rows_per_gate // tm
        in_specs.append(pl.BlockSpec((tm, tn), lambda j, i: (i, j)))
        in_specs.append(pl.BlockSpec((None, 1, tn), lambda j, i: (i // blocks_per_gate, 0, j)))
        args += [res, gate.reshape(gate.shape[0], 1, n)]
    out_spec = pl.BlockSpec((tm, tn), lambda j, i: (i, j))
    out_specs, out_shape = out_spec, jax.ShapeDtypeStruct((m, n), out_dtype)
    if norm is not None:
        mod_spec = pl.BlockSpec((None, 1, tn), lambda j, i: (i // blocks_per_gate, 0, j))
        in_specs += [pl.BlockSpec((1, tn), lambda j, i: (0, j)), mod_spec, mod_spec]
        args += [norm[0].reshape(1, n), norm[1].reshape(-1, 1, n), norm[2].reshape(-1, 1, n)]
        out_specs = [out_spec, out_spec]
        out_shape = [out_shape, jax.ShapeDtypeStruct((m, n), BF16)]
    kern = functools.partial(_mm_kernel, silu_in=silu_in, has_bias=bias is not None,
                             has_res=res is not None, has_norm=norm is not None)
    return pl.pallas_call(
        kern,
        grid=(n // tn, m // tm),
        in_specs=in_specs,
        out_specs=out_specs,
        out_shape=out_shape,
        compiler_params=_params(("parallel", "parallel")),
        name="matmul",
    )(*args)


def _conv_kernel(prev_ref, cur_ref, next_ref, w_ref, o_ref, *, tt, nheads_blk):
    i = pl.program_id(1)
    grp = pl.program_id(2)
    last = pl.num_programs(1) - 1
    prev = prev_ref[...] * (i > 0).astype(F32)
    nxt = next_ref[...] * (i < last).astype(F32)
    xx = jnp.concatenate([prev, cur_ref[...], nxt], axis=0)
    rows = tt + 16
    pad = (DN_CONV - 1) // 2
    y = None
    for k in range(DN_CONV):
        shift = (pad - k) % rows
        z = xx if shift == 0 else pltpu.roll(xx, shift, 0)
        term = z[8:8 + tt] * w_ref[k:k + 1, :]
        y = term if y is None else y + term
    y = y * _sigmoid(y)
    qscale = jnp.where(grp == 0, HEAD_DIM ** -0.5, 1.0).astype(F32)
    for h in range(nheads_blk):
        yh = y[:, h * HEAD_DIM:(h + 1) * HEAD_DIM]
        ss = jnp.sum(yh * yh, axis=-1, keepdims=True)
        fac = jnp.where(grp == 2, 1.0, lax.rsqrt(ss + NORM_EPS) * qscale)
        o_ref[:, h * HEAD_DIM:(h + 1) * HEAD_DIM] = yh * fac


def dn_conv(p, conv_w, tt=256):
    b, t, _ = p.shape
    tt = min(tt, t)
    cb = DN_HEADS * HEAD_DIM
    n8 = t // 8
    kern = functools.partial(_conv_kernel, tt=tt, nheads_blk=DN_HEADS)
    return pl.pallas_call(
        kern,
        grid=(b, t // tt, 3),
        in_specs=[pl.BlockSpec((None, 8, cb), lambda bi, i, g: (bi, jnp.maximum(i * (tt // 8) - 1, 0), g)),
                  pl.BlockSpec((None, tt, cb), lambda bi, i, g: (bi, i, g)),
                  pl.BlockSpec((None, 8, cb), lambda bi, i, g: (bi, jnp.minimum((i + 1) * (tt // 8), n8 - 1), g)),
                  pl.BlockSpec((DN_CONV, cb), lambda bi, i, g: (0, g))],
        out_specs=pl.BlockSpec((None, tt, cb), lambda bi, i, g: (bi, i, g)),
        out_shape=jax.ShapeDtypeStruct((b, t, 3 * cb), F32),
        compiler_params=_params(("parallel", "parallel", "parallel")),
        name="dn_conv",
    )(p, p, p, conv_w)


def _split3(x):
    hi = x.astype(BF16)
    r1 = x - hi.astype(F32)
    mid = r1.astype(BF16)
    lo = (r1 - mid.astype(F32)).astype(BF16)
    return hi, mid, lo


def _dn_chains(bi, direction, q_ref, k_ref, v_ref, p_ref, par_ref, o_ref):
    c = DN_CHUNK
    ii = lax.broadcasted_iota(jnp.int32, (c, c), 0)
    jj = lax.broadcasted_iota(jnp.int32, (c, c), 1)
    incl = (ii >= jj) if direction == 0 else (ii <= jj)
    strict = (ii > jj) if direction == 0 else (ii < jj)

    raw = p_ref[bi]
    beta_all = _sigmoid(raw)
    xa = raw + par_ref[0:1, :]
    softplus = jnp.maximum(xa, 0.0) + jnp.log1p(jnp.exp(-jnp.abs(xa)))
    g_all = -jnp.exp(par_ref[1:2, :]) * softplus
    inclb = incl.astype(BF16)
    hi, mid, lo = _split3(g_all)
    gc_all = (jnp.dot(inclb, hi, preferred_element_type=F32)
              + jnp.dot(inclb, mid, preferred_element_type=F32)
              + jnp.dot(inclb, lo, preferred_element_type=F32))
    gtot_all = jnp.sum(g_all, axis=0, keepdims=True)
    gc_t = gc_all.T

    chains = []
    for h in range(DN_HEADS):
        ib = direction * DN_HEADS + h
        ig = 2 * DN_HEADS + ib
        sl = slice(h * HEAD_DIM, (h + 1) * HEAD_DIM)
        gc_c = gc_all[:, ig:ig + 1]
        chains.append(dict(
            incl=incl, strict=strict, beta=beta_all[:, ib:ib + 1], gc_c=gc_c,
            decay=jnp.exp(jnp.where(incl, gc_c - gc_t[ig:ig + 1, :], -jnp.inf)),
            gtot=gtot_all[:, ig:ig + 1], q=q_ref[bi, :, sl], k=k_ref[bi, :, sl], v=v_ref[bi, :, sl],
            o_ref=o_ref, o_idx=(bi, slice(None), sl), state_idx=(bi, direction, h)))
    return chains


def _dn_step(chains, s_scr):
    c = DN_CHUNK
    ii = lax.broadcasted_iota(jnp.int32, (c, c), 0)
    jj = lax.broadcasted_iota(jnp.int32, (c, c), 1)
    eye = (ii == jj).astype(F32)
    for ch in chains:
        ch["kb"] = ch["k"] * ch["beta"]
        ch["kq"] = _dot_nt(jnp.concatenate([ch["kb"], ch["q"]], axis=0), ch["k"])
    for ch in chains:
        a = jnp.where(ch["strict"], ch["kq"][:c] * ch["decay"], 0.0)
        ch["qk"] = jnp.where(ch["incl"], ch["kq"][c:] * ch["decay"], 0.0)
        ch["inv"] = eye - a
        ch["a_pow"] = a
    for _ in range(int(math.log2(c)) - 1):
        for ch in chains:
            ch["a_pow"] = _dot(ch["a_pow"], ch["a_pow"])
        for ch in chains:
            ch["inv"] = _dot(ch["inv"], eye + ch["a_pow"])
    for ch in chains:
        egc = jnp.exp(ch["gc_c"])
        ch["uw"] = _dot(ch["inv"], jnp.concatenate([ch["v"] * ch["beta"], ch["kb"] * egc], axis=1))
        ch["q_dec"] = ch["q"] * egc
    for ch in chains:
        ch["state"] = s_scr[ch["state_idx"]]
        w = ch["uw"][:, HEAD_DIM:]
        ch["ws"] = _dot(jnp.concatenate([w, ch["q_dec"]], axis=0), ch["state"])
    for ch in chains:
        ch["v_new"] = ch["uw"][:, :HEAD_DIM] - ch["ws"][:c]
        ch["o_ref"][ch["o_idx"]] = ch["ws"][c:] + _dot(ch["qk"], ch["v_new"])
    for ch in chains:
        k_dec = ch["k"] * jnp.exp(ch["gtot"] - ch["gc_c"])
        s_scr[ch["state_idx"]] = ch["state"] * jnp.exp(ch["gtot"]) + _dot(k_dec.T, ch["v_new"])


def _dn_scan_kernel(qf_ref, kf_ref, vf_ref, pf_ref, qb_ref, kb_ref, vb_ref, pb_ref, par_ref, s0_ref,
                    of_ref, ob_ref, sfin_ref, s_scr):
    n = pl.program_id(0)

    @pl.when(n == 0)
    def _():
        s_scr[...] = s0_ref[...]

    chains = []
    for bi in range(s_scr.shape[0]):
        chains += _dn_chains(bi, 0, qf_ref, kf_ref, vf_ref, pf_ref, par_ref, of_ref)
        chains += _dn_chains(bi, 1, qb_ref, kb_ref, vb_ref, pb_ref, par_ref, ob_ref)
    _dn_step(chains, s_scr)

    @pl.when(n == pl.num_programs(0) - 1)
    def _():
        sfin_ref[...] = s_scr[...]


def dn_scan(qkv, p, par, s0):
    b, t, _ = qkv.shape
    c = DN_CHUNK
    nc = t // c
    w = DN_HEADS * HEAD_DIM
    fwd = lambda col: (lambda n: (0, n, col))
    bwd = lambda col: (lambda n: (0, nc - 1 - n, col))
    blk = lambda f: pl.BlockSpec((b, c, w), f)
    small = lambda f: pl.BlockSpec((b, c, LANES), f)
    state_spec = pl.BlockSpec((b, 2, DN_HEADS, HEAD_DIM, HEAD_DIM), lambda n: (0, 0, 0, 0, 0))
    o_shape = jax.ShapeDtypeStruct((b, t, w), F32)
    return pl.pallas_call(
        _dn_scan_kernel,
        grid=(nc,),
        in_specs=[blk(fwd(0)), blk(fwd(1)), blk(fwd(2)), small(fwd(32)),
                  blk(bwd(0)), blk(bwd(1)), blk(bwd(2)), small(bwd(32)),
                  pl.BlockSpec((8, LANES), lambda n: (0, 0)), state_spec],
        out_specs=[blk(fwd(0)), blk(bwd(0)), state_spec],
        out_shape=[o_shape, o_shape, jax.ShapeDtypeStruct(s0.shape, F32)],
        scratch_shapes=[pltpu.VMEM((b, 2, DN_HEADS, HEAD_DIM, HEAD_DIM), F32)],
        compiler_params=_params(("arbitrary",)),
        name="dn_scan",
    )(qkv, qkv, qkv, p, qkv, qkv, qkv, p, par, s0)


def _dn_out_kernel(of_ref, ob_ref, z_ref, w_ref, y_ref):
    o = of_ref[...] + ob_ref[...]
    z = z_ref[...]
    for h in range(DN_HEADS):
        sl = slice(h * HEAD_DIM, (h + 1) * HEAD_DIM)
        oh = o[:, sl]
        yh = oh * lax.rsqrt(jnp.mean(oh * oh, axis=-1, keepdims=True) + NORM_EPS) * w_ref[...]
        zh = z[:, sl]
        y_ref[:, sl] = (yh * (zh * _sigmoid(zh))).astype(y_ref.dtype)


def dn_out(o_f, o_b, p, norm_w, tr=256):
    b, t, w = o_f.shape
    tr = min(tr, t)
    return pl.pallas_call(
        _dn_out_kernel,
        grid=(b, t // tr),
        in_specs=[pl.BlockSpec((None, tr, w), lambda bi, i: (bi, i, 0)),
                  pl.BlockSpec((None, tr, w), lambda bi, i: (bi, i, 0)),
                  pl.BlockSpec((None, tr, w), lambda bi, i: (bi, i, 3)),
                  pl.BlockSpec((1, HEAD_DIM), lambda bi, i: (0, 0))],
        out_specs=pl.BlockSpec((None, tr, w), lambda bi, i: (bi, i, 0)),
        out_shape=jax.ShapeDtypeStruct((b, t, w), BF16),
        compiler_params=_params(("parallel", "parallel")),
        name="dn_out",
    )(o_f, o_b, p, norm_w.reshape(1, HEAD_DIM))


def _rope_kernel(x_ref, cos_ref, sin_ref, *o_refs, half, scale, nheads, split):
    lane = lax.broadcasted_iota(jnp.int32, (1, LANES), 1)
    first = (lane % (2 * half)) < half
    cos = cos_ref[...]
    sin = sin_ref[...]
    for h in range(nheads):
        sl = slice(h * LANES, (h + 1) * LANES)
        x = x_ref[:, sl]
        partner = jnp.where(first, pltpu.roll(x, LANES - half, 1), pltpu.roll(x, half, 1))
        y = (x * cos + partner * sin) * scale
        if split:
            o_refs[0][:, sl] = jnp.where(lane < LANES // 2, y, 0.0).astype(BF16)
            o_refs[1][:, sl] = jnp.where(lane >= LANES // 2, y, 0.0).astype(BF16)
        else:
            o_refs[0][:, sl] = y.astype(BF16)


def rope(x, col_block, width, cos, sin, *, half, scale=1.0, split=False, tr=512):
    b, t, _ = x.shape
    tr = min(tr, t)
    nheads = width // LANES
    nout = 2 if split else 1
    kern = functools.partial(_rope_kernel, half=half, scale=scale, nheads=nheads, split=split)
    out = pl.pallas_call(
        kern,
        grid=(b, t // tr),
        in_specs=[pl.BlockSpec((None, tr, width), lambda bi, i: (bi, i, col_block)),
                  pl.BlockSpec((tr, LANES), lambda bi, i: (i, 0)),
                  pl.BlockSpec((tr, LANES), lambda bi, i: (i, 0))],
        out_specs=[pl.BlockSpec((None, tr, width), lambda bi, i: (bi, i, 0))] * nout,
        out_shape=[jax.ShapeDtypeStruct((b, t, width), BF16)] * nout,
        compiler_params=_params(("parallel", "parallel")),
        name="rope",
    )(x, cos, sin)
    return out if split else out[0]


def rope_tables(t, block, half):
    pos = jnp.arange(t, dtype=jnp.int32)
    rows = (pos // GRID_W).astype(F32)
    cols = (pos % GRID_W).astype(F32)
    lane = np.arange(LANES)
    j = lane % block
    use_col = (j // (2 * half)) == 1
    i = j % (2 * half)
    inv = (ROPE_THETA ** (-jnp.arange(half, dtype=F32) / half))[i % half]
    p = jnp.where(jnp.asarray(use_col)[None, :], cols[:, None], rows[:, None])
    ang = p * inv[None, :]
    sign = jnp.asarray(np.where(i < half, -1.0, 1.0).astype(np.float32))[None, :]
    return jnp.cos(ang), jnp.sin(ang) * sign


def _diff_kernel(q1_ref, q2_ref, k_ref, vt_ref, lam_ref, nw_ref, o_ref, acc1, acc2, s_scr,
                 *, lam_init, tk, nkv):
    tq = q1_ref.shape[0]
    acc1[...] = jnp.zeros_like(acc1)
    acc2[...] = jnp.zeros_like(acc2)

    def scores(c, slot):
        kc = k_ref[pl.ds(pl.multiple_of(c * tk, tk), tk), :]
        s_scr[slot, 0] = _dot_nt(kc, q1_ref[...])
        s_scr[slot, 1] = _dot_nt(kc, q2_ref[...])

    def update(s, vtc, m, l, acc):
        m_new = jnp.maximum(m, jnp.max(s, axis=0, keepdims=True))
        alpha = jnp.exp2(m - m_new)
        p = jnp.exp2(s - m_new)
        l_new = l * alpha + jnp.sum(p, axis=0, keepdims=True)
        acc[...] = acc[...] * alpha + _dot(vtc, p)
        return m_new, l_new

    def softmax(c, slot, carry):
        m1, l1, m2, l2 = carry
        vtc = vt_ref[c]
        m1, l1 = update(s_scr[slot, 0], vtc, m1, l1, acc1)
        m2, l2 = update(s_scr[slot, 1], vtc, m2, l2, acc2)
        return m1, l1, m2, l2

    def body(i, carry):
        c = 2 * i
        scores(c + 1, 1)
        carry = softmax(c, 0, carry)
        scores(c + 2, 0)
        return softmax(c + 1, 1, carry)

    neg = jnp.full((1, tq), -jnp.inf, F32)
    zero = jnp.zeros((1, tq), F32)
    scores(0, 0)
    pairs = (nkv - 1) // 2
    carry = lax.fori_loop(0, pairs, body, (neg, zero, neg, zero))
    if nkv % 2 == 0:
        scores(nkv - 1, 1)
        carry = softmax(nkv - 2, 0, carry)
        carry = softmax(nkv - 1, 1, carry)
    else:
        carry = softmax(nkv - 1, 0, carry)
    m1, l1, m2, l2 = carry

    lv = lam_ref[...]
    s01 = jnp.sum(lv[0:1] * lv[1:2], axis=-1, keepdims=True)
    s23 = jnp.sum(lv[2:3] * lv[3:4], axis=-1, keepdims=True)
    lam = jnp.exp(s01) - jnp.exp(s23) + lam_init
    o = acc1[...] / l1 - lam * (acc2[...] / l2)
    y = o * lax.rsqrt(jnp.mean(o * o, axis=0, keepdims=True) + NORM_EPS)
    o_ref[...] = (y.T * nw_ref[...] * (1.0 - lam_init)).astype(o_ref.dtype)


def diff_attention(q1, q2, k, v, lam_vec, norm_w, lam_init, tq=1024, tk=1408):
    b, tqa, w = q1.shape
    tka = k.shape[1]
    tq = min(tq, tqa)
    tk = min(tk, tka)
    nkv = tka // tk
    nh = w // HEAD_DIM
    vt = v.reshape(b, nkv, tk, nh, HEAD_DIM).transpose(0, 3, 1, 4, 2)
    kern = functools.partial(_diff_kernel, lam_init=lam_init, tk=tk, nkv=nkv)
    return pl.pallas_call(
        kern,
        grid=(b, nh, tqa // tq),
        in_specs=[pl.BlockSpec((None, tq, HEAD_DIM), lambda bi, h, i: (bi, i, h)),
                  pl.BlockSpec((None, tq, HEAD_DIM), lambda bi, h, i: (bi, i, h)),
                  pl.BlockSpec((None, tka, HEAD_DIM), lambda bi, h, i: (bi, 0, h)),
                  pl.BlockSpec((None, None, nkv, HEAD_DIM, tk), lambda bi, h, i: (bi, h, 0, 0, 0)),
                  pl.BlockSpec((4, DIFF_DQK), lambda bi, h, i: (0, 0)),
                  pl.BlockSpec((1, HEAD_DIM), lambda bi, h, i: (0, 0))],
        out_specs=pl.BlockSpec((None, tq, HEAD_DIM), lambda bi, h, i: (bi, i, h)),
        out_shape=jax.ShapeDtypeStruct((b, tqa, w), BF16),
        scratch_shapes=[pltpu.VMEM((HEAD_DIM, tq), F32)] * 2 + [pltpu.VMEM((2, 2, tk, tq), F32)],
        compiler_params=_params(("parallel", "parallel", "parallel")),
        name="diff_attention",
    )(q1, q2, k, vt, lam_vec, norm_w.reshape(1, HEAD_DIM))


def _win_kernel(sink_ref, q_ref, kp_ref, kc_ref, kn_ref, vp_ref, vc_ref, vn_ref, kx_ref, vx_ref, o_ref,
                *, t_total):
    kvh = pl.program_id(1)
    n = pl.program_id(2)
    wb = WINDOW
    tq = q_ref.shape[0]
    keys = jnp.concatenate([kp_ref[...], kc_ref[...], kn_ref[...], kx_ref[...]], axis=0)
    vals = jnp.concatenate([vp_ref[...], vc_ref[...], vn_ref[...], vx_ref[...]], axis=0)
    nk = keys.shape[0]
    nlocal = tq + 2 * wb
    i = lax.broadcasted_iota(jnp.int32, (tq, nk), 0)
    j = lax.broadcasted_iota(jnp.int32, (tq, nk), 1)
    kpos = n * tq - wb + j
    valid = (jnp.abs(j - wb - i) <= WINDOW) & (kpos >= 0) & (kpos < t_total)
    valid = valid | (j >= nlocal)
    for g in range(GQA_GROUP):
        sl = slice(g * HEAD_DIM, (g + 1) * HEAD_DIM)
        sink = sink_ref[kvh * GQA_GROUP + g]
        s = jnp.where(valid, _dot_nt(q_ref[:, sl], keys), -jnp.inf)
        m = jnp.maximum(jnp.max(s, axis=-1, keepdims=True), sink)
        p = jnp.exp(s - m)
        denom = jnp.sum(p, axis=-1, keepdims=True) + jnp.exp(sink - m)
        o_ref[:, sl] = (_dot(p, vals) / denom).astype(o_ref.dtype)


def window_attention(q, k, v, kvx, sink, tq=512):
    b, t, _ = q.shape
    lctx = kvx.shape[1]
    wb = WINDOW
    tq = min(tq, t)
    per = tq // wb
    nb = t // wb
    qw = GQA_GROUP * HEAD_DIM
    kern = functools.partial(_win_kernel, t_total=t)
    prev = lambda bi, h, n: (bi, jnp.maximum(n * per - 1, 0), h)
    cur = lambda bi, h, n: (bi, n, h)
    nxt = lambda bi, h, n: (bi, jnp.minimum((n + 1) * per, nb - 1), h)
    edge_spec = lambda f: pl.BlockSpec((None, wb, HEAD_DIM), f)
    main_spec = pl.BlockSpec((None, tq, HEAD_DIM), cur)
    return pl.pallas_call(
        kern,
        grid=(b, GQA_KV_HEADS, t // tq),
        in_specs=[pl.BlockSpec(memory_space=pltpu.SMEM),
                  pl.BlockSpec((None, tq, qw), cur),
                  edge_spec(prev), main_spec, edge_spec(nxt),
                  edge_spec(prev), main_spec, edge_spec(nxt),
                  pl.BlockSpec((None, lctx, HEAD_DIM), lambda bi, h, n: (bi, 0, h)),
                  pl.BlockSpec((None, lctx, HEAD_DIM), lambda bi, h, n: (bi, 0, GQA_KV_HEADS + h))],
        out_specs=pl.BlockSpec((None, tq, qw), cur),
        out_shape=jax.ShapeDtypeStruct(q.shape, BF16),
        compiler_params=_params(("parallel", "parallel", "parallel")),
        name="window_attention",
    )(sink, q, k, k, k, v, v, v, kvx, kvx)


RANK_NONE = 255.0
RANK_CODE_BASE = 1e30
RANK_CODE_STEP = 1e28


def _top_values(s, count, want_rank=False):
    vals = []
    work = s
    for r in range(count):
        m = jnp.max(work, axis=0, keepdims=True)
        vals.append(m)
        work = jnp.where(work == m, -(RANK_CODE_BASE + r * RANK_CODE_STEP), work)
    if not want_rank:
        return vals
    rank = jnp.where(work < -0.5 * RANK_CODE_BASE,
                     jnp.round((-work - RANK_CODE_BASE) * (1.0 / RANK_CODE_STEP)), RANK_NONE)
    return vals, rank


def _route_kernel(q_ref, sub_ref, cut_ref, e1_ref, r2_ref, e2_ref):
    kk = PEER_TOPK
    tb = q_ref.shape[0]
    row = lax.broadcasted_iota(jnp.int32, (kk, tb), 0)
    for h in range(PEER_HEADS):
        scores, tops, ranks = [], [], []
        for p in range(2):
            c0 = (h * 2 + p) * PEER_NKEYS
            s = _dot_nt(sub_ref[h, p], q_ref[:, c0:c0 + PEER_NKEYS])
            vals, rank = _top_values(s, kk, want_rank=True)
            scores.append(s)
            tops.append(vals)
            ranks.append(rank)
        v1 = jnp.zeros((kk, tb), F32)
        v2 = jnp.zeros((kk, tb), F32)
        for i in range(kk):
            v1 = jnp.where(row == i, tops[0][i], v1)
            v2 = jnp.where(row == i, tops[1][i], v2)
        mid = jnp.where(row[:8] >= 4, v1[:8], -jnp.inf)
        cand = jnp.concatenate([tops[0][0] + v2] + [tops[0][i] + v2[:8] for i in range(1, 4)]
                               + [mid + tops[1][j] for j in range(3)] + [v1[8:] + tops[1][0]], axis=0)
        thr = _top_values(cand, kk)[kk - 1]
        top = tops[0][0] + tops[1][0]
        z = jnp.sum(jnp.where(cand >= thr, jnp.exp(cand - top), 0.0), axis=0, keepdims=True)
        gdt = r2_ref.dtype
        rank1 = ranks[0].astype(gdt)
        cut = jnp.zeros(rank1.shape, gdt)
        for i in range(kk):
            count = jnp.sum(((tops[0][i] + v2) >= thr).astype(F32), axis=0, keepdims=True)
            cut = jnp.where(rank1 == i, count.astype(gdt), cut)
        in1 = ranks[0] < float(kk)
        in2 = ranks[1] < float(kk)
        cut_ref[h] = cut.astype(F32)
        e1_ref[h] = jnp.where(in1, jnp.exp(scores[0] - tops[0][0]), 0.0)
        r2_ref[h] = ranks[1].astype(r2_ref.dtype)
        e2_ref[h] = jnp.where(in2, jnp.exp(scores[1] - tops[1][0]) / z, 0.0).astype(e2_ref.dtype)


def peer_route(q, subkeys, tb=128):
    n = q.shape[0]
    tb = min(tb, n)
    hh = PEER_HEADS
    tab32 = jax.ShapeDtypeStruct((hh, PEER_NKEYS, n), F32)
    tab16 = jax.ShapeDtypeStruct((hh, PEER_NKEYS, n), BF16)
    tab_spec = pl.BlockSpec((hh, PEER_NKEYS, tb), lambda i: (0, 0, i))
    return pl.pallas_call(
        _route_kernel,
        grid=(n // tb,),
        in_specs=[pl.BlockSpec((tb, q.shape[1]), lambda i: (i, 0)),
                  pl.BlockSpec(subkeys.shape, lambda i: (0, 0, 0, 0))],
        out_specs=[tab_spec] * 4,
        out_shape=[tab32, tab32, tab16, tab16],
        compiler_params=_params(("parallel",)),
        name="peer_route",
    )(q, subkeys)


def _gelu(x):
    return 0.5 * x * (1.0 + lax.erf(x * (2.0 ** -0.5)))


def _peer_kernel(xt_ref, u_ref, vt_ref, cut_ref, e1_ref, r2_ref, e2_ref, o_ref, acc, g_scr, h_scr, a_scr,
                 *, rows, nparts):
    et = pl.program_id(1)

    @pl.when(et == 0)
    def _():
        acc[...] = jnp.zeros_like(acc)

    nk = PEER_NKEYS
    gdt = r2_ref.dtype
    half = rows // nparts
    pk = half * nk

    def gate_row(r):
        gate = None
        for h in range(PEER_HEADS):
            cut = cut_ref[h, r:r + 1, :].astype(gdt)
            e1 = e1_ref[h, r:r + 1, :].astype(gdt)
            g = jnp.where(r2_ref[h] < cut, e1 * e2_ref[h], jnp.zeros((), gdt))
            gate = g if gate is None else gate + g
        return gate

    def hidden(part):
        return jnp.dot(u_ref[part * pk:(part + 1) * pk, :], xt_ref[...], preferred_element_type=F32)

    def activate(ht, gate):
        return (_gelu(ht).astype(gdt) * gate).astype(BF16)

    def project(part, a):
        return jnp.dot(vt_ref[:, part * pk:(part + 1) * pk], a, preferred_element_type=F32)

    for r in range(half):
        g_scr[r * nk:(r + 1) * nk, :] = gate_row(r)
    h_scr[...] = hidden(0)
    for part in range(nparts):
        if part + 1 < nparts:
            h_next = hidden(part + 1)
        a = activate(h_scr[...], g_scr[...]) if part == 0 else a_scr[...]
        acc[...] += project(part, a)
        if part + 1 < nparts:
            for r in range(half):
                a_scr[r * nk:(r + 1) * nk, :] = activate(h_next[r * nk:(r + 1) * nk, :],
                                                        gate_row((part + 1) * half + r))

    @pl.when(et == pl.num_programs(1) - 1)
    def _():
        o_ref[...] = acc[...].T


def peer_experts(xt, u, vt, cut, e1, r2, e2, tb=512, rows=8, nparts=2):
    d, n = xt.shape
    e = u.shape[0]
    tb = min(tb, n)
    te = rows * PEER_NKEYS
    hh = PEER_HEADS
    kern = functools.partial(_peer_kernel, rows=rows, nparts=nparts)
    row_spec = pl.BlockSpec((hh, rows, tb), lambda i, j: (0, j, i))
    full_spec = pl.BlockSpec((hh, PEER_NKEYS, tb), lambda i, j: (0, 0, i))
    return pl.pallas_call(
        kern,
        grid=(n // tb, e // te),
        in_specs=[pl.BlockSpec((d, tb), lambda i, j: (0, i)),
                  pl.BlockSpec((te, d), lambda i, j: (j, 0)),
                  pl.BlockSpec((d, te), lambda i, j: (0, j)),
                  row_spec, row_spec, full_spec, full_spec],
        out_specs=pl.BlockSpec((tb, d), lambda i, j: (i, 0)),
        out_shape=jax.ShapeDtypeStruct((n, d), F32),
        scratch_shapes=[pltpu.VMEM((d, tb), F32), pltpu.VMEM((te // nparts, tb), r2.dtype),
                        pltpu.VMEM((te // nparts, tb), F32), pltpu.VMEM((te // nparts, tb), BF16)],
        compiler_params=_params(("parallel", "arbitrary")),
        name="peer_experts",
    )(xt, u, vt, cut, e1, r2, e2)


def _resid_kernel(*refs, final, has_norm):
    x_ref, y_ref, g_ref, w_ref = refs[:4]
    x = x_ref[...] + g_ref[...] * y_ref[...]
    if final:
        x = x * lax.rsqrt(jnp.mean(x * x, axis=-1, keepdims=True) + NORM_EPS) * w_ref[...]
    if has_norm:
        shift_ref, scale_ref, o_ref, h_ref = refs[4:]
        h_ref[...] = _normmod_rows(x, w_ref[...], shift_ref[...], scale_ref[...]).astype(h_ref.dtype)
    else:
        o_ref = refs[4]
    o_ref[...] = x


def resid(x, y, gate, final_w=None, norm=None, rows=512):
    b, t, d = x.shape
    tr = min(rows, t)
    bm = gate.shape[0]
    mod_map = (lambda i, j: (i, 0, 0)) if bm == b else (lambda i, j: (0, 0, 0))
    mod_spec = pl.BlockSpec((None, 1, d), mod_map)
    w = final_w if final_w is not None else (norm[0] if norm is not None else jnp.ones((d,), F32))
    kern = functools.partial(_resid_kernel, final=final_w is not None, has_norm=norm is not None)
    blk = pl.BlockSpec((None, tr, d), lambda i, j: (i, j, 0))
    in_specs = [blk, blk, mod_spec, pl.BlockSpec((1, d), lambda i, j: (0, 0))]
    args = [x, y, gate.reshape(bm, 1, d), w.reshape(1, d)]
    out_specs, out_shape = blk, jax.ShapeDtypeStruct((b, t, d), F32)
    if norm is not None:
        in_specs += [mod_spec, mod_spec]
        args += [norm[1].reshape(bm, 1, d), norm[2].reshape(bm, 1, d)]
        out_specs = [blk, blk]
        out_shape = [out_shape, jax.ShapeDtypeStruct((b, t, d), BF16)]
    return pl.pallas_call(
        kern,
        grid=(b, t // tr),
        in_specs=in_specs,
        out_specs=out_specs,
        out_shape=out_shape,
        compiler_params=_params(("parallel", "parallel")),
        name="resid",
    )(*args)


def _peer_ffn(x, fx, gate, wq, bq, subkeys, u, vt, final_w=None, norm=None):
    b, t, d = x.shape
    n = b * t
    q = matmul(fx.reshape(n, d), wq, out_dtype=BF16, bias=bq)
    cut, e1, r2, e2 = peer_route(q, subkeys)
    y = peer_experts(fx.reshape(n, d).T, u, vt, cut, e1, r2, e2)
    return resid(x, y.reshape(b, t, d), gate, final_w=final_w, norm=norm)


def _out_proj(a, w_out, x, gate, norm):
    b, t, d = x.shape
    per_batch = lambda v: v if v.shape[0] == b else jnp.broadcast_to(v, (b, d))
    y, h = matmul(a.reshape(b * t, a.shape[-1]), w_out, out_dtype=F32, res=x.reshape(b * t, d),
                  gate=per_batch(gate), rows_per_gate=t, tm=min(512, t),
                  norm=(norm[0], per_batch(norm[1]), per_batch(norm[2])))
    return y.reshape(b, t, d), h.reshape(b, t, d)


def kernel(x, c, ctx, c_ctx, ada_w, ada_b, norm_mix_w, norm_ffn_w, ab_w_in, ab_w_out, dn_conv_w, dn_a_log,
           dn_dt_bias, dn_norm_w, diff_lambda, diff_norm_w, gqa_w_in, gqa_w_out, gqa_sink, peer_wq, peer_bq,
           peer_subkeys, peer_u, peer_v, final_norm_w):
    b, t, d = x.shape
    lctx = ctx.shape[1]
    depth = ada_w.shape[0]
    n = b * t
    nctx = b * lctx

    cond = jnp.zeros((16, d), F32).at[:b].set(c).at[b].set(c_ctx)
    mods = []
    for l in range(depth):
        m = matmul(cond, ada_w, out_dtype=F32, bias=ada_b[l], silu_in=True, tn_cap=1024, layer=l)
        mods.append(m.reshape(16, 6, d))
    mod_x = lambda l, i: mods[l][:b, i]
    mod_c = lambda l, i: mods[l][b:b + 1, i]

    w_in = ab_w_in[0]
    qkv_w = DN_HEADS * HEAD_DIM * 3
    z_w = DN_HEADS * HEAD_DIM
    small = 4 * DN_HEADS
    off = qkv_w + z_w
    w_dn = jnp.concatenate([w_in[:, :off], w_in[:, off:off + small],
                            jnp.zeros((d, LANES - small), F32)], axis=1).astype(BF16)
    off += small
    dq_w = DIFF_HEADS * HEAD_DIM
    w_dqk = w_in[:, off:off + 2 * dq_w].astype(BF16)
    w_dv = w_in[:, off + 2 * dq_w:].astype(BF16)
    par = jnp.zeros((8, LANES), F32)
    par = par.at[0, 2 * DN_HEADS:4 * DN_HEADS].set(dn_dt_bias[0].reshape(-1))
    par = par.at[1, 2 * DN_HEADS:4 * DN_HEADS].set(dn_a_log[0].reshape(-1))
    lam_init = 0.8 - 0.6 * math.exp(-0.3 * 0)

    hx = normmod(x, norm_mix_w[0], mod_x(0, 0), mod_x(0, 1))
    hc = normmod(ctx, norm_mix_w[0], mod_c(0, 0), mod_c(0, 1))
    hx2 = hx.reshape(n, d)
    hc2 = hc.reshape(nctx, d)

    p_x = matmul(hx2, w_dn, out_dtype=F32).reshape(b, t, -1)
    p_c = matmul(hc2, w_dn, out_dtype=F32).reshape(b, lctx, -1)
    qkv_c = dn_conv(p_c, dn_conv_w[0])
    qkv_x = dn_conv(p_x, dn_conv_w[0])
    s0 = jnp.zeros((b, 2, DN_HEADS, HEAD_DIM, HEAD_DIM), F32)
    of_c, ob_c, s_c = dn_scan(qkv_c, p_c, par, s0)
    of_x, ob_x, _ = dn_scan(qkv_x, p_x, par, s_c)
    dn_x = dn_out(of_x, ob_x, p_x, dn_norm_w[0])
    dn_c = dn_out(of_c, ob_c, p_c, dn_norm_w[0])

    qk_x = matmul(hx2, w_dqk, out_dtype=F32).reshape(b, t, -1)
    qk_c = matmul(hc2, w_dqk, out_dtype=F32).reshape(b, lctx, -1)
    v_x = matmul(hx2, w_dv, out_dtype=BF16).reshape(b, t, -1)
    v_c = matmul(hc2, w_dv, out_dtype=BF16).reshape(b, lctx, -1)
    cos_d, sin_d = rope_tables(t, DIFF_DQK, DIFF_DQK // 4)
    one_c = jnp.ones((lctx, LANES), F32)
    zero_c = jnp.zeros((lctx, LANES), F32)
    dscale = DIFF_DQK ** -0.5 * math.log2(math.e)
    q1_x, q2_x = rope(qk_x, 0, dq_w, cos_d, sin_d, half=DIFF_DQK // 4, scale=dscale, split=True)
    k_x = rope(qk_x, 1, dq_w, cos_d, sin_d, half=DIFF_DQK // 4)
    q1_c, q2_c = rope(qk_c, 0, dq_w, one_c, zero_c, half=DIFF_DQK // 4, scale=dscale, split=True)
    k_c = rope(qk_c, 1, dq_w, one_c, zero_c, half=DIFF_DQK // 4)
    k_all = jnp.concatenate([k_x, k_c], axis=1)
    v_all = jnp.concatenate([v_x, v_c], axis=1)
    d_x = diff_attention(q1_x, q2_x, k_all, v_all, diff_lambda[0], diff_norm_w[0], lam_init)
    d_c = diff_attention(q1_c, q2_c, k_c, v_c, diff_lambda[0], diff_norm_w[0], lam_init)

    w_out = ab_w_out[0].astype(BF16)
    x, fx = _out_proj(jnp.concatenate([dn_x, d_x], axis=-1), w_out, x, mod_x(0, 2),
                      (norm_ffn_w[0], mod_x(0, 3), mod_x(0, 4)))
    ctx, fc = _out_proj(jnp.concatenate([dn_c, d_c], axis=-1), w_out, ctx, mod_c(0, 2),
                        (norm_ffn_w[0], mod_c(0, 3), mod_c(0, 4)))

    wq = peer_wq[0].astype(BF16)
    sub = peer_subkeys[0].astype(BF16)
    u_tab = peer_u[0].astype(BF16)
    vt_tab = peer_v[0].T.astype(BF16)
    x, hx = _peer_ffn(x, fx, mod_x(0, 5), wq, peer_bq[0], sub, u_tab, vt_tab,
                      norm=(norm_mix_w[1], mod_x(1, 0), mod_x(1, 1)))
    ctx, hc = _peer_ffn(ctx, fc, mod_c(0, 5), wq, peer_bq[0], sub, u_tab, vt_tab,
                        norm=(norm_mix_w[1], mod_c(1, 0), mod_c(1, 1)))

    gq_w = GQA_Q_HEADS * HEAD_DIM
    gkv_w = GQA_KV_HEADS * HEAD_DIM
    w_g = gqa_w_in[0]
    w_gqk = w_g[:, :gq_w + gkv_w].astype(BF16)
    w_gv = w_g[:, gq_w + gkv_w:].astype(BF16)
    w_gkv = w_g[:, gq_w:].astype(BF16)
    hx2 = hx.reshape(n, d)
    qk = matmul(hx2, w_gqk, out_dtype=F32, tn_cap=1280).reshape(b, t, -1)
    gv = matmul(hx2, w_gv, out_dtype=BF16).reshape(b, t, -1)
    kvx = matmul(hc.reshape(nctx, d), w_gkv, out_dtype=BF16).reshape(b, lctx, -1)
    cos_g, sin_g = rope_tables(t, HEAD_DIM, HEAD_DIM // 4)
    gq = rope(qk, 0, gq_w, cos_g, sin_g, half=HEAD_DIM // 4, scale=HEAD_DIM ** -0.5)
    gk = rope(qk, gq_w // gkv_w, gkv_w, cos_g, sin_g, half=HEAD_DIM // 4)
    att = window_attention(gq, gk, gv, kvx, gqa_sink[0])
    x, fx = _out_proj(att, gqa_w_out[0].astype(BF16), x, mod_x(1, 2),
                      (norm_ffn_w[1], mod_x(1, 3), mod_x(1, 4)))

    wq = peer_wq[1].astype(BF16)
    sub = peer_subkeys[1].astype(BF16)
    u_tab = peer_u[1].astype(BF16)
    vt_tab = peer_v[1].T.astype(BF16)
    return _peer_ffn(x, fx, mod_x(1, 5), wq, peer_bq[1], sub, u_tab, vt_tab, final_w=final_norm_w)
```

```python
import functools
import math

import jax
import jax.numpy as jnp
import numpy as np
from jax import lax
from jax.experimental import pallas as pl
from jax.experimental.pallas import tpu as pltpu

F32 = jnp.float32
BF16 = jnp.bfloat16

LANES = 128
HEAD_DIM = 128
NORM_EPS = 1e-6
ROPE_THETA = 10000.0
GRID_W = 64
DN_HEADS = 8
DN_CHUNK = 64
DN_CONV = 5
DIFF_HEADS = 8
DIFF_DQK = 64
GQA_Q_HEADS = 16
GQA_KV_HEADS = 4
GQA_GROUP = GQA_Q_HEADS // GQA_KV_HEADS
WINDOW = 128
PEER_HEADS = 8
PEER_NKEYS = 128
PEER_TOPK = 16
NEG_BIG = -1e30
VMEM_LIMIT = 56 * 1024 * 1024


def _params(sem):
    return pltpu.CompilerParams(dimension_semantics=sem, vmem_limit_bytes=VMEM_LIMIT)


def _dot(a, b):
    return jnp.dot(a.astype(BF16), b.astype(BF16), preferred_element_type=F32)


def _dot_nt(a, b):
    return lax.dot_general(a.astype(BF16), b.astype(BF16), (((1,), (1,)), ((), ())),
                           preferred_element_type=F32)


def _sigmoid(x):
    return 1.0 / (1.0 + jnp.exp(-x))


def _normmod_kernel(x_ref, w_ref, shift_ref, scale_ref, o_ref):
    x = x_ref[...]
    y = x * lax.rsqrt(jnp.mean(x * x, axis=-1, keepdims=True) + NORM_EPS)
    y = y * w_ref[...]
    o_ref[...] = (y * (1.0 + scale_ref[...]) + shift_ref[...]).astype(o_ref.dtype)


def normmod(x, w, shift, scale, rows=512):
    b, t, d = x.shape
    tr = min(rows, t)
    bm = shift.shape[0]
    mod_map = (lambda i, j: (i, 0, 0)) if bm == b else (lambda i, j: (0, 0, 0))
    return pl.pallas_call(
        _normmod_kernel,
        grid=(b, t // tr),
        in_specs=[pl.BlockSpec((None, tr, d), lambda i, j: (i, j, 0)),
                  pl.BlockSpec((1, d), lambda i, j: (0, 0)),
                  pl.BlockSpec((None, 1, d), mod_map),
                  pl.BlockSpec((None, 1, d), mod_map)],
        out_specs=pl.BlockSpec((None, tr, d), lambda i, j: (i, j, 0)),
        out_shape=jax.ShapeDtypeStruct((b, t, d), BF16),
        compiler_params=_params(("parallel", "parallel")),
        name="normmod",
    )(x, w.reshape(1, d), shift.reshape(bm, 1, d), scale.reshape(bm, 1, d))


def _normmod_rows(x, w, shift, scale):
    y = x * lax.rsqrt(jnp.mean(x * x, axis=-1, keepdims=True) + NORM_EPS) * w
    return y * (1.0 + scale) + shift


def _mm_kernel(*refs, silu_in, has_bias, has_res, has_norm):
    a_ref, w_ref = refs[0], refs[1]
    pos = 2
    a = a_ref[...]
    if silu_in:
        a = a * _sigmoid(a)
    acc = _dot(a, w_ref[...])
    if has_bias:
        acc = acc + refs[pos][...]
        pos += 1
    if has_res:
        acc = refs[pos][...] + refs[pos + 1][...] * acc
        pos += 2
    if has_norm:
        nw_ref, shift_ref, scale_ref = refs[pos:pos + 3]
        pos += 3
    o_ref = refs[pos]
    o_ref[...] = acc.astype(o_ref.dtype)
    if has_norm:
        h_ref = refs[pos + 1]
        h_ref[...] = _normmod_rows(acc, nw_ref[...], shift_ref[...], scale_ref[...]).astype(h_ref.dtype)


def _pick_tn(n, cap):
    best = LANES
    for cand in range(LANES, min(n, cap) + 1, LANES):
        if n % cand == 0:
            best = cand
    return best


def matmul(a, w, *, out_dtype, bias=None, res=None, gate=None, rows_per_gate=None,
           silu_in=False, tm=512, tn_cap=2048, layer=0, norm=None):
    m, k = a.shape
    n = w.shape[-1]
    tm = min(tm, m)
    tn = _pick_tn(n, tn_cap)
    assert norm is None or (tn == n and res is not None)
    if w.ndim == 3:
        w_spec = pl.BlockSpec((None, k, tn), lambda j, i: (layer, 0, j))
    else:
        w_spec = pl.BlockSpec((k, tn), lambda j, i: (0, j))
    in_specs = [pl.BlockSpec((tm, k), lambda j, i: (i, 0)), w_spec]
    args = [a, w]
    if bias is not None:
        in_specs.append(pl.BlockSpec((1, tn), lambda j, i: (0, j)))
        args.append(bias.reshape(1, n).astype(F32))
    if res is not None:
        blocks_per_gate = rows_per_gate // tm
        in_specs.append(pl.BlockSpec((tm, tn), lambda j, i: (i, j)))
        in_specs.append(pl.BlockSpec((None, 1, tn), lambda j, i: (i // blocks_per_gate, 0, j)))
        args += [res, gate.reshape(gate.shape[0], 1, n)]
    out_spec = pl.BlockSpec((tm, tn), lambda j, i: (i, j))
    out_specs, out_shape = out_spec, jax.ShapeDtypeStruct((m, n), out_dtype)
    if norm is not None:
        mod_spec = pl.BlockSpec((None, 1, tn), lambda j, i: (i // blocks_per_gate, 0, j))
        in_specs += [pl.BlockSpec((1, tn), lambda j, i: (0, j)), mod_spec, mod_spec]
        args += [norm[0].reshape(1, n), norm[1].reshape(-1, 1, n), norm[2].reshape(-1, 1, n)]
        out_specs = [out_spec, out_spec]
        out_shape = [out_shape, jax.ShapeDtypeStruct((m, n), BF16)]
    kern = functools.partial(_mm_kernel, silu_in=silu_in, has_bias=bias is not None,
                             has_res=res is not None, has_norm=norm is not None)
    return pl.pallas_call(
        kern,
        grid=(n // tn, m // tm),
        in_specs=in_specs,
        out_specs=out_specs,
        out_shape=out_shape,
        compiler_params=_params(("parallel", "parallel")),
        name="matmul",
    )(*args)


def _conv_kernel(prev_ref, cur_ref, next_ref, w_ref, o_ref, *, tt, nheads_blk):
    i = pl.program_id(1)
    grp = pl.program_id(2)
    last = pl.num_programs(1) - 1
    prev = prev_ref[...] * (i > 0).astype(F32)
    nxt = next_ref[...] * (i < last).astype(F32)
    xx = jnp.concatenate([prev, cur_ref[...], nxt], axis=0)
    rows = tt + 16
    pad = (DN_CONV - 1) // 2
    y = None
    for k in range(DN_CONV):
        shift = (pad - k) % rows
        z = xx if shift == 0 else pltpu.roll(xx, shift, 0)
        term = z[8:8 + tt] * w_ref[k:k + 1, :]
        y = term if y is None else y + term
    y = y * _sigmoid(y)
    qscale = jnp.where(grp == 0, HEAD_DIM ** -0.5, 1.0).astype(F32)
    for h in range(nheads_blk):
        yh = y[:, h * HEAD_DIM:(h + 1) * HEAD_DIM]
        ss = jnp.sum(yh * yh, axis=-1, keepdims=True)
        fac = jnp.where(grp == 2, 1.0, lax.rsqrt(ss + NORM_EPS) * qscale)
        o_ref[:, h * HEAD_DIM:(h + 1) * HEAD_DIM] = yh * fac


def dn_conv(p, conv_w, tt=256):
    b, t, _ = p.shape
    tt = min(tt, t)
    cb = DN_HEADS * HEAD_DIM
    n8 = t // 8
    kern = functools.partial(_conv_kernel, tt=tt, nheads_blk=DN_HEADS)
    return pl.pallas_call(
        kern,
        grid=(b, t // tt, 3),
        in_specs=[pl.BlockSpec((None, 8, cb), lambda bi, i, g: (bi, jnp.maximum(i * (tt // 8) - 1, 0), g)),
                  pl.BlockSpec((None, tt, cb), lambda bi, i, g: (bi, i, g)),
                  pl.BlockSpec((None, 8, cb), lambda bi, i, g: (bi, jnp.minimum((i + 1) * (tt // 8), n8 - 1), g)),
                  pl.BlockSpec((DN_CONV, cb), lambda bi, i, g: (0, g))],
        out_specs=pl.BlockSpec((None, tt, cb), lambda bi, i, g: (bi, i, g)),
        out_shape=jax.ShapeDtypeStruct((b, t, 3 * cb), F32),
        compiler_params=_params(("parallel", "parallel", "parallel")),
        name="dn_conv",
    )(p, p, p, conv_w)


def _split3(x):
    hi = x.astype(BF16)
    r1 = x - hi.astype(F32)
    mid = r1.astype(BF16)
    lo = (r1 - mid.astype(F32)).astype(BF16)
    return hi, mid, lo


def _dn_chains(bi, direction, q_ref, k_ref, v_ref, p_ref, par_ref, o_ref):
    c = DN_CHUNK
    ii = lax.broadcasted_iota(jnp.int32, (c, c), 0)
    jj = lax.broadcasted_iota(jnp.int32, (c, c), 1)
    incl = (ii >= jj) if direction == 0 else (ii <= jj)
    strict = (ii > jj) if direction == 0 else (ii < jj)

    raw = p_ref[bi]
    beta_all = _sigmoid(raw)
    xa = raw + par_ref[0:1, :]
    softplus = jnp.maximum(xa, 0.0) + jnp.log1p(jnp.exp(-jnp.abs(xa)))
    g_all = -jnp.exp(par_ref[1:2, :]) * softplus
    inclb = incl.astype(BF16)
    hi, mid, lo = _split3(g_all)
    gc_all = (jnp.dot(inclb, hi, preferred_element_type=F32)
              + jnp.dot(inclb, mid, preferred_element_type=F32)
              + jnp.dot(inclb, lo, preferred_element_type=F32))
    gtot_all = jnp.sum(g_all, axis=0, keepdims=True)
    gc_t = gc_all.T

    chains = []
    for h in range(DN_HEADS):
        ib = direction * DN_HEADS + h
        ig = 2 * DN_HEADS + ib
        sl = slice(h * HEAD_DIM, (h + 1) * HEAD_DIM)
        gc_c = gc_all[:, ig:ig + 1]
        chains.append(dict(
            incl=incl, strict=strict, beta=beta_all[:, ib:ib + 1], gc_c=gc_c,
            decay=jnp.exp(jnp.where(incl, gc_c - gc_t[ig:ig + 1, :], -jnp.inf)),
            gtot=gtot_all[:, ig:ig + 1], q=q_ref[bi, :, sl], k=k_ref[bi, :, sl], v=v_ref[bi, :, sl],
            o_ref=o_ref, o_idx=(bi, slice(None), sl), state_idx=(bi, direction, h)))
    return chains


def _dn_step(chains, s_scr):
    c = DN_CHUNK
    ii = lax.broadcasted_iota(jnp.int32, (c, c), 0)
    jj = lax.broadcasted_iota(jnp.int32, (c, c), 1)
    eye = (ii == jj).astype(F32)
    for ch in chains:
        ch["kb"] = ch["k"] * ch["beta"]
        ch["kq"] = _dot_nt(jnp.concatenate([ch["kb"], ch["q"]], axis=0), ch["k"])
    for ch in chains:
        a = jnp.where(ch["strict"], ch["kq"][:c] * ch["decay"], 0.0)
        ch["qk"] = jnp.where(ch["incl"], ch["kq"][c:] * ch["decay"], 0.0)
        ch["inv"] = eye - a
        ch["a_pow"] = a
    for _ in range(int(math.log2(c)) - 1):
        for ch in chains:
            ch["a_pow"] = _dot(ch["a_pow"], ch["a_pow"])
        for ch in chains:
            ch["inv"] = _dot(ch["inv"], eye + ch["a_pow"])
    for ch in chains:
        egc = jnp.exp(ch["gc_c"])
        ch["uw"] = _dot(ch["inv"], jnp.concatenate([ch["v"] * ch["beta"], ch["kb"] * egc], axis=1))
        ch["q_dec"] = ch["q"] * egc
    for ch in chains:
        ch["state"] = s_scr[ch["state_idx"]]
        w = ch["uw"][:, HEAD_DIM:]
        ch["ws"] = _dot(jnp.concatenate([w, ch["q_dec"]], axis=0), ch["state"])
    for ch in chains:
        ch["v_new"] = ch["uw"][:, :HEAD_DIM] - ch["ws"][:c]
        ch["o_ref"][ch["o_idx"]] = ch["ws"][c:] + _dot(ch["qk"], ch["v_new"])
    for ch in chains:
        k_dec = ch["k"] * jnp.exp(ch["gtot"] - ch["gc_c"])
        s_scr[ch["state_idx"]] = ch["state"] * jnp.exp(ch["gtot"]) + _dot(k_dec.T, ch["v_new"])


def _dn_scan_kernel(qf_ref, kf_ref, vf_ref, pf_ref, qb_ref, kb_ref, vb_ref, pb_ref, par_ref, s0_ref,
                    of_ref, ob_ref, sfin_ref, s_scr):
    n = pl.program_id(0)

    @pl.when(n == 0)
    def _():
        s_scr[...] = s0_ref[...]

    chains = []
    for bi in range(s_scr.shape[0]):
        chains += _dn_chains(bi, 0, qf_ref, kf_ref, vf_ref, pf_ref, par_ref, of_ref)
        chains += _dn_chains(bi, 1, qb_ref, kb_ref, vb_ref, pb_ref, par_ref, ob_ref)
    _dn_step(chains, s_scr)

    @pl.when(n == pl.num_programs(0) - 1)
    def _():
        sfin_ref[...] = s_scr[...]


def dn_scan(qkv, p, par, s0):
    b, t, _ = qkv.shape
    c = DN_CHUNK
    nc = t // c
    w = DN_HEADS * HEAD_DIM
    fwd = lambda col: (lambda n: (0, n, col))
    bwd = lambda col: (lambda n: (0, nc - 1 - n, col))
    blk = lambda f: pl.BlockSpec((b, c, w), f)
    small = lambda f: pl.BlockSpec((b, c, LANES), f)
    state_spec = pl.BlockSpec((b, 2, DN_HEADS, HEAD_DIM, HEAD_DIM), lambda n: (0, 0, 0, 0, 0))
    o_shape = jax.ShapeDtypeStruct((b, t, w), F32)
    return pl.pallas_call(
        _dn_scan_kernel,
        grid=(nc,),
        in_specs=[blk(fwd(0)), blk(fwd(1)), blk(fwd(2)), small(fwd(32)),
                  blk(bwd(0)), blk(bwd(1)), blk(bwd(2)), small(bwd(32)),
                  pl.BlockSpec((8, LANES), lambda n: (0, 0)), state_spec],
        out_specs=[blk(fwd(0)), blk(bwd(0)), state_spec],
        out_shape=[o_shape, o_shape, jax.ShapeDtypeStruct(s0.shape, F32)],
        scratch_shapes=[pltpu.VMEM((b, 2, DN_HEADS, HEAD_DIM, HEAD_DIM), F32)],
        compiler_params=_params(("arbitrary",)),
        name="dn_scan",
    )(qkv, qkv, qkv, p, qkv, qkv, qkv, p, par, s0)


def _dn_out_kernel(of_ref, ob_ref, z_ref, w_ref, y_ref):
    o = of_ref[...] + ob_ref[...]
    z = z_ref[...]
    for h in range(DN_HEADS):
        sl = slice(h * HEAD_DIM, (h + 1) * HEAD_DIM)
        oh = o[:, sl]
        yh = oh * lax.rsqrt(jnp.mean(oh * oh, axis=-1, keepdims=True) + NORM_EPS) * w_ref[...]
        zh = z[:, sl]
        y_ref[:, sl] = (yh * (zh * _sigmoid(zh))).astype(y_ref.dtype)


def dn_out(o_f, o_b, p, norm_w, tr=256):
    b, t, w = o_f.shape
    tr = min(tr, t)
    return pl.pallas_call(
        _dn_out_kernel,
        grid=(b, t // tr),
        in_specs=[pl.BlockSpec((None, tr, w), lambda bi, i: (bi, i, 0)),
                  pl.BlockSpec((None, tr, w), lambda bi, i: (bi, i, 0)),
                  pl.BlockSpec((None, tr, w), lambda bi, i: (bi, i, 3)),
                  pl.BlockSpec((1, HEAD_DIM), lambda bi, i: (0, 0))],
        out_specs=pl.BlockSpec((None, tr, w), lambda bi, i: (bi, i, 0)),
        out_shape=jax.ShapeDtypeStruct((b, t, w), BF16),
        compiler_params=_params(("parallel", "parallel")),
        name="dn_out",
    )(o_f, o_b, p, norm_w.reshape(1, HEAD_DIM))


def _rope_kernel(x_ref, cos_ref, sin_ref, *o_refs, half, scale, nheads, split):
    lane = lax.broadcasted_iota(jnp.int32, (1, LANES), 1)
    first = (lane % (2 * half)) < half
    cos = cos_ref[...]
    sin = sin_ref[...]
    for h in range(nheads):
        sl = slice(h * LANES, (h + 1) * LANES)
        x = x_ref[:, sl]
        partner = jnp.where(first, pltpu.roll(x, LANES - half, 1), pltpu.roll(x, half, 1))
        y = (x * cos + partner * sin) * scale
        if split:
            o_refs[0][:, sl] = jnp.where(lane < LANES // 2, y, 0.0).astype(BF16)
            o_refs[1][:, sl] = jnp.where(lane >= LANES // 2, y, 0.0).astype(BF16)
        else:
            o_refs[0][:, sl] = y.astype(BF16)


def rope(x, col_block, width, cos, sin, *, half, scale=1.0, split=False, tr=512):
    b, t, _ = x.shape
    tr = min(tr, t)
    nheads = width // LANES
    nout = 2 if split else 1
    kern = functools.partial(_rope_kernel, half=half, scale=scale, nheads=nheads, split=split)
    out = pl.pallas_call(
        kern,
        grid=(b, t // tr),
        in_specs=[pl.BlockSpec((None, tr, width), lambda bi, i: (bi, i, col_block)),
                  pl.BlockSpec((tr, LANES), lambda bi, i: (i, 0)),
                  pl.BlockSpec((tr, LANES), lambda bi, i: (i, 0))],
        out_specs=[pl.BlockSpec((None, tr, width), lambda bi, i: (bi, i, 0))] * nout,
        out_shape=[jax.ShapeDtypeStruct((b, t, width), BF16)] * nout,
        compiler_params=_params(("parallel", "parallel")),
        name="rope",
    )(x, cos, sin)
    return out if split else out[0]


def rope_tables(t, block, half):
    pos = jnp.arange(t, dtype=jnp.int32)
    rows = (pos // GRID_W).astype(F32)
    cols = (pos % GRID_W).astype(F32)
    lane = np.arange(LANES)
    j = lane % block
    use_col = (j // (2 * half)) == 1
    i = j % (2 * half)
    inv = (ROPE_THETA ** (-jnp.arange(half, dtype=F32) / half))[i % half]
    p = jnp.where(jnp.asarray(use_col)[None, :], cols[:, None], rows[:, None])
    ang = p * inv[None, :]
    sign = jnp.asarray(np.where(i < half, -1.0, 1.0).astype(np.float32))[None, :]
    return jnp.cos(ang), jnp.sin(ang) * sign


def _diff_kernel(q1_ref, q2_ref, k_ref, vt_ref, lam_ref, nw_ref, o_ref, acc1, acc2, s_scr,
                 *, lam_init, tk, nkv):
    tq = q1_ref.shape[0]
    acc1[...] = jnp.zeros_like(acc1)
    acc2[...] = jnp.zeros_like(acc2)

    def scores(c, slot):
        kc = k_ref[pl.ds(pl.multiple_of(c * tk, tk), tk), :]
        s_scr[slot, 0] = _dot_nt(kc, q1_ref[...])
        s_scr[slot, 1] = _dot_nt(kc, q2_ref[...])

    def update(s, vtc, m, l, acc):
        m_new = jnp.maximum(m, jnp.max(s, axis=0, keepdims=True))
        alpha = jnp.exp2(m - m_new)
        p = jnp.exp2(s - m_new)
        l_new = l * alpha + jnp.sum(p, axis=0, keepdims=True)
        acc[...] = acc[...] * alpha + _dot(vtc, p)
        return m_new, l_new

    def softmax(c, slot, carry):
        m1, l1, m2, l2 = carry
        vtc = vt_ref[c]
        m1, l1 = update(s_scr[slot, 0], vtc, m1, l1, acc1)
        m2, l2 = update(s_scr[slot, 1], vtc, m2, l2, acc2)
        return m1, l1, m2, l2

    def body(i, carry):
        c = 2 * i
        scores(c + 1, 1)
        carry = softmax(c, 0, carry)
        scores(c + 2, 0)
        return softmax(c + 1, 1, carry)

    neg = jnp.full((1, tq), -jnp.inf, F32)
    zero = jnp.zeros((1, tq), F32)
    scores(0, 0)
    pairs = (nkv - 1) // 2
    carry = lax.fori_loop(0, pairs, body, (neg, zero, neg, zero))
    if nkv % 2 == 0:
        scores(nkv - 1, 1)
        carry = softmax(nkv - 2, 0, carry)
        carry = softmax(nkv - 1, 1, carry)
    else:
        carry = softmax(nkv - 1, 0, carry)
    m1, l1, m2, l2 = carry

    lv = lam_ref[...]
    s01 = jnp.sum(lv[0:1] * lv[1:2], axis=-1, keepdims=True)
    s23 = jnp.sum(lv[2:3] * lv[3:4], axis=-1, keepdims=True)
    lam = jnp.exp(s01) - jnp.exp(s23) + lam_init
    o = acc1[...] / l1 - lam * (acc2[...] / l2)
    y = o * lax.rsqrt(jnp.mean(o * o, axis=0, keepdims=True) + NORM_EPS)
    o_ref[...] = (y.T * nw_ref[...] * (1.0 - lam_init)).astype(o_ref.dtype)


def diff_attention(q1, q2, k, v, lam_vec, norm_w, lam_init, tq=1024, tk=1408):
    b, tqa, w = q1.shape
    tka = k.shape[1]
    tq = min(tq, tqa)
    tk = min(tk, tka)
    nkv = tka // tk
    nh = w // HEAD_DIM
    vt = v.reshape(b, nkv, tk, nh, HEAD_DIM).transpose(0, 3, 1, 4, 2)
    kern = functools.partial(_diff_kernel, lam_init=lam_init, tk=tk, nkv=nkv)
    return pl.pallas_call(
        kern,
        grid=(b, nh, tqa // tq),
        in_specs=[pl.BlockSpec((None, tq, HEAD_DIM), lambda bi, h, i: (bi, i, h)),
                  pl.BlockSpec((None, tq, HEAD_DIM), lambda bi, h, i: (bi, i, h)),
                  pl.BlockSpec((None, tka, HEAD_DIM), lambda bi, h, i: (bi, 0, h)),
                  pl.BlockSpec((None, None, nkv, HEAD_DIM, tk), lambda bi, h, i: (bi, h, 0, 0, 0)),
                  pl.BlockSpec((4, DIFF_DQK), lambda bi, h, i: (0, 0)),
                  pl.BlockSpec((1, HEAD_DIM), lambda bi, h, i: (0, 0))],
        out_specs=pl.BlockSpec((None, tq, HEAD_DIM), lambda bi, h, i: (bi, i, h)),
        out_shape=jax.ShapeDtypeStruct((b, tqa, w), BF16),
        scratch_shapes=[pltpu.VMEM((HEAD_DIM, tq), F32)] * 2 + [pltpu.VMEM((2, 2, tk, tq), F32)],
        compiler_params=_params(("parallel", "parallel", "parallel")),
        name="diff_attention",
    )(q1, q2, k, vt, lam_vec, norm_w.reshape(1, HEAD_DIM))


def _win_kernel(sink_ref, q_ref, kp_ref, kc_ref, kn_ref, vp_ref, vc_ref, vn_ref, kx_ref, vx_ref, o_ref,
                *, t_total):
    kvh = pl.program_id(1)
    n = pl.program_id(2)
    wb = WINDOW
    tq = q_ref.shape[0]
    keys = jnp.concatenate([kp_ref[...], kc_ref[...], kn_ref[...], kx_ref[...]], axis=0)
    vals = jnp.concatenate([vp_ref[...], vc_ref[...], vn_ref[...], vx_ref[...]], axis=0)
    nk = keys.shape[0]
    nlocal = tq + 2 * wb
    i = lax.broadcasted_iota(jnp.int32, (tq, nk), 0)
    j = lax.broadcasted_iota(jnp.int32, (tq, nk), 1)
    kpos = n * tq - wb + j
    valid = (jnp.abs(j - wb - i) <= WINDOW) & (kpos >= 0) & (kpos < t_total)
    valid = valid | (j >= nlocal)
    for g in range(GQA_GROUP):
        sl = slice(g * HEAD_DIM, (g + 1) * HEAD_DIM)
        sink = sink_ref[kvh * GQA_GROUP + g]
        s = jnp.where(valid, _dot_nt(q_ref[:, sl], keys), -jnp.inf)
        m = jnp.maximum(jnp.max(s, axis=-1, keepdims=True), sink)
        p = jnp.exp(s - m)
        denom = jnp.sum(p, axis=-1, keepdims=True) + jnp.exp(sink - m)
        o_ref[:, sl] = (_dot(p, vals) / denom).astype(o_ref.dtype)


def window_attention(q, k, v, kvx, sink, tq=512):
    b, t, _ = q.shape
    lctx = kvx.shape[1]
    wb = WINDOW
    tq = min(tq, t)
    per = tq // wb
    nb = t // wb
    qw = GQA_GROUP * HEAD_DIM
    kern = functools.partial(_win_kernel, t_total=t)
    prev = lambda bi, h, n: (bi, jnp.maximum(n * per - 1, 0), h)
    cur = lambda bi, h, n: (bi, n, h)
    nxt = lambda bi, h, n: (bi, jnp.minimum((n + 1) * per, nb - 1), h)
    edge_spec = lambda f: pl.BlockSpec((None, wb, HEAD_DIM), f)
    main_spec = pl.BlockSpec((None, tq, HEAD_DIM), cur)
    return pl.pallas_call(
        kern,
        grid=(b, GQA_KV_HEADS, t // tq),
        in_specs=[pl.BlockSpec(memory_space=pltpu.SMEM),
                  pl.BlockSpec((None, tq, qw), cur),
                  edge_spec(prev), main_spec, edge_spec(nxt),
                  edge_spec(prev), main_spec, edge_spec(nxt),
                  pl.BlockSpec((None, lctx, HEAD_DIM), lambda bi, h, n: (bi, 0, h)),
                  pl.BlockSpec((None, lctx, HEAD_DIM), lambda bi, h, n: (bi, 0, GQA_KV_HEADS + h))],
        out_specs=pl.BlockSpec((None, tq, qw), cur),
        out_shape=jax.ShapeDtypeStruct(q.shape, BF16),
        compiler_params=_params(("parallel", "parallel", "parallel")),
        name="window_attention",
    )(sink, q, k, k, k, v, v, v, kvx, kvx)


RANK_NONE = 255.0
RANK_CODE_BASE = 1e30
RANK_CODE_STEP = 1e28


def _top_values(s, count, want_rank=False):
    vals = []
    work = s
    for r in range(count):
        m = jnp.max(work, axis=0, keepdims=True)
        vals.append(m)
        work = jnp.where(work == m, -(RANK_CODE_BASE + r * RANK_CODE_STEP), work)
    if not want_rank:
        return vals
    rank = jnp.where(work < -0.5 * RANK_CODE_BASE,
                     jnp.round((-work - RANK_CODE_BASE) * (1.0 / RANK_CODE_STEP)), RANK_NONE)
    return vals, rank


def _route_kernel(q_ref, sub_ref, cut_ref, e1_ref, r2_ref, e2_ref):
    kk = PEER_TOPK
    tb = q_ref.shape[0]
    row = lax.broadcasted_iota(jnp.int32, (kk, tb), 0)
    for h in range(PEER_HEADS):
        scores, tops, ranks = [], [], []
        for p in range(2):
            c0 = (h * 2 + p) * PEER_NKEYS
            s = _dot_nt(sub_ref[h, p], q_ref[:, c0:c0 + PEER_NKEYS])
            vals, rank = _top_values(s, kk, want_rank=True)
            scores.append(s)
            tops.append(vals)
            ranks.append(rank)
        v1 = jnp.zeros((kk, tb), F32)
        v2 = jnp.zeros((kk, tb), F32)
        for i in range(kk):
            v1 = jnp.where(row == i, tops[0][i], v1)
            v2 = jnp.where(row == i, tops[1][i], v2)
        mid = jnp.where(row[:8] >= 4, v1[:8], -jnp.inf)
        cand = jnp.concatenate([tops[0][0] + v2] + [tops[0][i] + v2[:8] for i in range(1, 4)]
                               + [mid + tops[1][j] for j in range(3)] + [v1[8:] + tops[1][0]], axis=0)
        thr = _top_values(cand, kk)[kk - 1]
        top = tops[0][0] + tops[1][0]
        z = jnp.sum(jnp.where(cand >= thr, jnp.exp(cand - top), 0.0), axis=0, keepdims=True)
        gdt = r2_ref.dtype
        rank1 = ranks[0].astype(gdt)
        cut = jnp.zeros(rank1.shape, gdt)
        for i in range(kk):
            count = jnp.sum(((tops[0][i] + v2) >= thr).astype(F32), axis=0, keepdims=True)
            cut = jnp.where(rank1 == i, count.astype(gdt), cut)
        in1 = ranks[0] < float(kk)
        in2 = ranks[1] < float(kk)
        cut_ref[h] = cut.astype(F32)
        e1_ref[h] = jnp.where(in1, jnp.exp(scores[0] - tops[0][0]), 0.0)
        r2_ref[h] = ranks[1].astype(r2_ref.dtype)
        e2_ref[h] = jnp.where(in2, jnp.exp(scores[1] - tops[1][0]) / z, 0.0).astype(e2_ref.dtype)


def peer_route(q, subkeys, tb=128):
    n = q.shape[0]
    tb = min(tb, n)
    hh = PEER_HEADS
    tab32 = jax.ShapeDtypeStruct((hh, PEER_NKEYS, n), F32)
    tab16 = jax.ShapeDtypeStruct((hh, PEER_NKEYS, n), BF16)
    tab_spec = pl.BlockSpec((hh, PEER_NKEYS, tb), lambda i: (0, 0, i))
    return pl.pallas_call(
        _route_kernel,
        grid=(n // tb,),
        in_specs=[pl.BlockSpec((tb, q.shape[1]), lambda i: (i, 0)),
                  pl.BlockSpec(subkeys.shape, lambda i: (0, 0, 0, 0))],
        out_specs=[tab_spec] * 4,
        out_shape=[tab32, tab32, tab16, tab16],
        compiler_params=_params(("parallel",)),
        name="peer_route",
    )(q, subkeys)


def _gelu(x):
    return 0.5 * x * (1.0 + lax.erf(x * (2.0 ** -0.5)))


def _peer_kernel(*refs, rows, nparts, final):
    xt_ref, u_ref, vt_ref, cut_ref, e1_ref, r2_ref, e2_ref = refs[:7]
    if final:
        x_ref, gate_ref, fw_ref = refs[7:10]
    o_ref, acc, g_scr, h_scr, a_scr = refs[-5:]
    et = pl.program_id(1)

    @pl.when(et == 0)
    def _():
        acc[...] = jnp.zeros_like(acc)

    nk = PEER_NKEYS
    gdt = r2_ref.dtype
    half = rows // nparts
    pk = half * nk

    def gate_row(r):
        gate = None
        for h in range(PEER_HEADS):
            cut = cut_ref[h, r:r + 1, :].astype(gdt)
            e1 = e1_ref[h, r:r + 1, :].astype(gdt)
            g = jnp.where(r2_ref[h] < cut, e1 * e2_ref[h], jnp.zeros((), gdt))
            gate = g if gate is None else gate + g
        return gate

    def hidden(part):
        return jnp.dot(u_ref[part * pk:(part + 1) * pk, :], xt_ref[...], preferred_element_type=F32)

    def activate(ht, gate):
        return (_gelu(ht).astype(gdt) * gate).astype(BF16)

    def project(part, a):
        return jnp.dot(vt_ref[:, part * pk:(part + 1) * pk], a, preferred_element_type=F32)

    for r in range(half):
        g_scr[r * nk:(r + 1) * nk, :] = gate_row(r)
    h_scr[...] = hidden(0)
    for part in range(nparts):
        if part + 1 < nparts:
            h_next = hidden(part + 1)
        a = activate(h_scr[...], g_scr[...]) if part == 0 else a_scr[...]
        acc[...] += project(part, a)
        if part + 1 < nparts:
            for r in range(half):
                a_scr[r * nk:(r + 1) * nk, :] = activate(h_next[r * nk:(r + 1) * nk, :],
                                                        gate_row((part + 1) * half + r))

    @pl.when(et == pl.num_programs(1) - 1)
    def _():
        y = acc[...].T
        if final:
            y = x_ref[...] + gate_ref[...] * y
            y = y * lax.rsqrt(jnp.mean(y * y, axis=-1, keepdims=True) + NORM_EPS) * fw_ref[...]
        o_ref[...] = y


def peer_experts(xt, u, vt, cut, e1, r2, e2, tb=512, rows=8, nparts=2, final=None):
    d, n = xt.shape
    e = u.shape[0]
    tb = min(tb, n)
    te = rows * PEER_NKEYS
    hh = PEER_HEADS
    kern = functools.partial(_peer_kernel, rows=rows, nparts=nparts, final=final is not None)
    row_spec = pl.BlockSpec((hh, rows, tb), lambda i, j: (0, j, i))
    full_spec = pl.BlockSpec((hh, PEER_NKEYS, tb), lambda i, j: (0, 0, i))
    in_specs = [pl.BlockSpec((d, tb), lambda i, j: (0, i)),
                pl.BlockSpec((te, d), lambda i, j: (j, 0)),
                pl.BlockSpec((d, te), lambda i, j: (0, j)),
                row_spec, row_spec, full_spec, full_spec]
    args = [xt, u, vt, cut, e1, r2, e2]
    if final is not None:
        x_res, gate, fw, rows_per_gate = final
        blocks_per_gate = rows_per_gate // tb
        in_specs += [pl.BlockSpec((tb, d), lambda i, j: (i, 0)),
                     pl.BlockSpec((None, 1, d), lambda i, j: (i // blocks_per_gate, 0, 0)),
                     pl.BlockSpec((1, d), lambda i, j: (0, 0))]
        args += [x_res, gate.reshape(gate.shape[0], 1, d), fw.reshape(1, d)]
    return pl.pallas_call(
        kern,
        grid=(n // tb, e // te),
        in_specs=in_specs,
        out_specs=pl.BlockSpec((tb, d), lambda i, j: (i, 0)),
        out_shape=jax.ShapeDtypeStruct((n, d), F32),
        scratch_shapes=[pltpu.VMEM((d, tb), F32), pltpu.VMEM((te // nparts, tb), r2.dtype),
                        pltpu.VMEM((te // nparts, tb), F32), pltpu.VMEM((te // nparts, tb), BF16)],
        compiler_params=_params(("parallel", "arbitrary")),
        name="peer_experts",
    )(*args)


def _resid_kernel(*refs, final, has_norm):
    x_ref, y_ref, g_ref, w_ref = refs[:4]
    x = x_ref[...] + g_ref[...] * y_ref[...]
    if final:
        x = x * lax.rsqrt(jnp.mean(x * x, axis=-1, keepdims=True) + NORM_EPS) * w_ref[...]
    if has_norm:
        shift_ref, scale_ref, o_ref, h_ref = refs[4:]
        h_ref[...] = _normmod_rows(x, w_ref[...], shift_ref[...], scale_ref[...]).astype(h_ref.dtype)
    else:
        o_ref = refs[4]
    o_ref[...] = x


def resid(x, y, gate, final_w=None, norm=None, rows=512):
    b, t, d = x.shape
    tr = min(rows, t)
    bm = gate.shape[0]
    mod_map = (lambda i, j: (i, 0, 0)) if bm == b else (lambda i, j: (0, 0, 0))
    mod_spec = pl.BlockSpec((None, 1, d), mod_map)
    w = final_w if final_w is not None else (norm[0] if norm is not None else jnp.ones((d,), F32))
    kern = functools.partial(_resid_kernel, final=final_w is not None, has_norm=norm is not None)
    blk = pl.BlockSpec((None, tr, d), lambda i, j: (i, j, 0))
    in_specs = [blk, blk, mod_spec, pl.BlockSpec((1, d), lambda i, j: (0, 0))]
    args = [x, y, gate.reshape(bm, 1, d), w.reshape(1, d)]
    out_specs, out_shape = blk, jax.ShapeDtypeStruct((b, t, d), F32)
    if norm is not None:
        in_specs += [mod_spec, mod_spec]
        args += [norm[1].reshape(bm, 1, d), norm[2].reshape(bm, 1, d)]
        out_specs = [blk, blk]
        out_shape = [out_shape, jax.ShapeDtypeStruct((b, t, d), BF16)]
    return pl.pallas_call(
        kern,
        grid=(b, t // tr),
        in_specs=in_specs,
        out_specs=out_specs,
        out_shape=out_shape,
        compiler_params=_params(("parallel", "parallel")),
        name="resid",
    )(*args)


def _peer_ffn(x, fx, gate, wq, bq, subkeys, u, vt, final_w=None, norm=None):
    b, t, d = x.shape
    n = b * t
    q = matmul(fx.reshape(n, d), wq, out_dtype=BF16, bias=bq)
    cut, e1, r2, e2 = peer_route(q, subkeys)
    if final_w is not None:
        y = peer_experts(fx.reshape(n, d).T, u, vt, cut, e1, r2, e2,
                         final=(x.reshape(n, d), gate, final_w, t))
        return y.reshape(b, t, d)
    y = peer_experts(fx.reshape(n, d).T, u, vt, cut, e1, r2, e2)
    return resid(x, y.reshape(b, t, d), gate, norm=norm)


def _out_proj(a, w_out, x, gate, norm):
    b, t, d = x.shape
    per_batch = lambda v: v if v.shape[0] == b else jnp.broadcast_to(v, (b, d))
    y, h = matmul(a.reshape(b * t, a.shape[-1]), w_out, out_dtype=F32, res=x.reshape(b * t, d),
                  gate=per_batch(gate), rows_per_gate=t, tm=min(512, t),
                  norm=(norm[0], per_batch(norm[1]), per_batch(norm[2])))
    return y.reshape(b, t, d), h.reshape(b, t, d)


def kernel(x, c, ctx, c_ctx, ada_w, ada_b, norm_mix_w, norm_ffn_w, ab_w_in, ab_w_out, dn_conv_w, dn_a_log,
           dn_dt_bias, dn_norm_w, diff_lambda, diff_norm_w, gqa_w_in, gqa_w_out, gqa_sink, peer_wq, peer_bq,
           peer_subkeys, peer_u, peer_v, final_norm_w):
    b, t, d = x.shape
    lctx = ctx.shape[1]
    depth = ada_w.shape[0]
    n = b * t
    nctx = b * lctx

    cond = jnp.zeros((16, d), F32).at[:b].set(c).at[b].set(c_ctx)
    mods = []
    for l in range(depth):
        m = matmul(cond, ada_w, out_dtype=F32, bias=ada_b[l], silu_in=True, tn_cap=1024, layer=l)
        mods.append(m.reshape(16, 6, d))
    mod_x = lambda l, i: mods[l][:b, i]
    mod_c = lambda l, i: mods[l][b:b + 1, i]

    w_in = ab_w_in[0]
    qkv_w = DN_HEADS * HEAD_DIM * 3
    z_w = DN_HEADS * HEAD_DIM
    small = 4 * DN_HEADS
    off = qkv_w + z_w
    w_dn = jnp.concatenate([w_in[:, :off], w_in[:, off:off + small],
                            jnp.zeros((d, LANES - small), F32)], axis=1).astype(BF16)
    off += small
    dq_w = DIFF_HEADS * HEAD_DIM
    w_dqk = w_in[:, off:off + 2 * dq_w].astype(BF16)
    w_dv = w_in[:, off + 2 * dq_w:].astype(BF16)
    par = jnp.zeros((8, LANES), F32)
    par = par.at[0, 2 * DN_HEADS:4 * DN_HEADS].set(dn_dt_bias[0].reshape(-1))
    par = par.at[1, 2 * DN_HEADS:4 * DN_HEADS].set(dn_a_log[0].reshape(-1))
    lam_init = 0.8 - 0.6 * math.exp(-0.3 * 0)

    hx = normmod(x, norm_mix_w[0], mod_x(0, 0), mod_x(0, 1))
    hc = normmod(ctx, norm_mix_w[0], mod_c(0, 0), mod_c(0, 1))
    hx2 = hx.reshape(n, d)
    hc2 = hc.reshape(nctx, d)

    p_x = matmul(hx2, w_dn, out_dtype=F32).reshape(b, t, -1)
    p_c = matmul(hc2, w_dn, out_dtype=F32).reshape(b, lctx, -1)
    qkv_c = dn_conv(p_c, dn_conv_w[0])
    qkv_x = dn_conv(p_x, dn_conv_w[0])
    s0 = jnp.zeros((b, 2, DN_HEADS, HEAD_DIM, HEAD_DIM), F32)
    of_c, ob_c, s_c = dn_scan(qkv_c, p_c, par, s0)
    of_x, ob_x, _ = dn_scan(qkv_x, p_x, par, s_c)
    dn_x = dn_out(of_x, ob_x, p_x, dn_norm_w[0])
    dn_c = dn_out(of_c, ob_c, p_c, dn_norm_w[0])

    qk_x = matmul(hx2, w_dqk, out_dtype=F32).reshape(b, t, -1)
    qk_c = matmul(hc2, w_dqk, out_dtype=F32).reshape(b, lctx, -1)
    v_x = matmul(hx2, w_dv, out_dtype=BF16).reshape(b, t, -1)
    v_c = matmul(hc2, w_dv, out_dtype=BF16).reshape(b, lctx, -1)
    cos_d, sin_d = rope_tables(t, DIFF_DQK, DIFF_DQK // 4)
    one_c = jnp.ones((lctx, LANES), F32)
    zero_c = jnp.zeros((lctx, LANES), F32)
    dscale = DIFF_DQK ** -0.5 * math.log2(math.e)
    q1_x, q2_x = rope(qk_x, 0, dq_w, cos_d, sin_d, half=DIFF_DQK // 4, scale=dscale, split=True)
    k_x = rope(qk_x, 1, dq_w, cos_d, sin_d, half=DIFF_DQK // 4)
    q1_c, q2_c = rope(qk_c, 0, dq_w, one_c, zero_c, half=DIFF_DQK // 4, scale=dscale, split=True)
    k_c = rope(qk_c, 1, dq_w, one_c, zero_c, half=DIFF_DQK // 4)
    k_all = jnp.concatenate([k_x, k_c], axis=1)
    v_all = jnp.concatenate([v_x, v_c], axis=1)
    d_x = diff_attention(q1_x, q2_x, k_all, v_all, diff_lambda[0], diff_norm_w[0], lam_init)
    d_c = diff_attention(q1_c, q2_c, k_c, v_c, diff_lambda[0], diff_norm_w[0], lam_init)

    w_out = ab_w_out[0].astype(BF16)
    x, fx = _out_proj(jnp.concatenate([dn_x, d_x], axis=-1), w_out, x, mod_x(0, 2),
                      (norm_ffn_w[0], mod_x(0, 3), mod_x(0, 4)))
    ctx, fc = _out_proj(jnp.concatenate([dn_c, d_c], axis=-1), w_out, ctx, mod_c(0, 2),
                        (norm_ffn_w[0], mod_c(0, 3), mod_c(0, 4)))

    wq = peer_wq[0].astype(BF16)
    sub = peer_subkeys[0].astype(BF16)
    u_tab = peer_u[0].astype(BF16)
    vt_tab = peer_v[0].T.astype(BF16)
    x, hx = _peer_ffn(x, fx, mod_x(0, 5), wq, peer_bq[0], sub, u_tab, vt_tab,
                      norm=(norm_mix_w[1], mod_x(1, 0), mod_x(1, 1)))
    ctx, hc = _peer_ffn(ctx, fc, mod_c(0, 5), wq, peer_bq[0], sub, u_tab, vt_tab,
                        norm=(norm_mix_w[1], mod_c(1, 0), mod_c(1, 1)))

    gq_w = GQA_Q_HEADS * HEAD_DIM
    gkv_w = GQA_KV_HEADS * HEAD_DIM
    w_g = gqa_w_in[0]
    w_gqk = w_g[:, :gq_w + gkv_w].astype(BF16)
    w_gv = w_g[:, gq_w + gkv_w:].astype(BF16)
    w_gkv = w_g[:, gq_w:].astype(BF16)
    hx2 = hx.reshape(n, d)
    qk = matmul(hx2, w_gqk, out_dtype=F32, tn_cap=1280).reshape(b, t, -1)
    gv = matmul(hx2, w_gv, out_dtype=BF16).reshape(b, t, -1)
    kvx = matmul(hc.reshape(nctx, d), w_gkv, out_dtype=BF16).reshape(b, lctx, -1)
    cos_g, sin_g = rope_tables(t, HEAD_DIM, HEAD_DIM // 4)
    gq = rope(qk, 0, gq_w, cos_g, sin_g, half=HEAD_DIM // 4, scale=HEAD_DIM ** -0.5)
    gk = rope(qk, gq_w // gkv_w, gkv_w, cos_g, sin_g, half=HEAD_DIM // 4)
    att = window_attention(gq, gk, gv, kvx, gqa_sink[0])
    x, fx = _out_proj(att, gqa_w_out[0].astype(BF16), x, mod_x(1, 2),
                      (norm_ffn_w[1], mod_x(1, 3), mod_x(1, 4)))

    wq = peer_wq[1].astype(BF16)
    sub = peer_subkeys[1].astype(BF16)
    u_tab = peer_u[1].astype(BF16)
    vt_tab = peer_v[1].T.astype(BF16)
    return _peer_ffn(x, fx, mod_x(1, 5), wq, peer_bq[1], sub, u_tab, vt_tab, final_w=final_norm_w)
```
